```python
import math
import jax, jax.numpy as jnp
from jax import lax
import numpy as np

D_MODEL = 1024
BATCH = 4
SEQ = 8192
DEPTH = 1
DEC_BATCH = 32
DEC_SEQ = 2048
PAST_LEN = 128

GRID_W = 64
D_MIX = D_MODEL
NA_HEADS = 8
NA_HEAD_DIM = D_MIX // 2 // NA_HEADS
NA_KH = 8
NA_KW = 16
DIFF_HEADS = 4
DIFF_HALF_DIM = D_MIX // 2 // DIFF_HEADS // 2
DIFF_V_DIM = 2 * DIFF_HALF_DIM
W_NA = NA_HEADS * NA_HEAD_DIM
W_DIFF = DIFF_HEADS * DIFF_V_DIM
D_IN = 3 * W_NA + 3 * W_DIFF
Q_BLOCK = 128
T5_BUCKETS = 32
T5_MAX_DIST = 128
N_EXPERTS = 16
CAPACITY_FACTOR = 2
D_EXPERT = 2816
EPS = 1e-6

kernel_name = 'hymba_na_diff_ec_encoder'


def rms_norm(x, w):
    xf = x.astype(jnp.float32)
    y = xf * lax.rsqrt(jnp.mean(xf * xf, axis=-1, keepdims=True) + EPS)
    return (y * w.astype(jnp.float32)).astype(x.dtype)


def t5_bucket(rel):
    nb = T5_BUCKETS // 2
    max_exact = nb // 2
    n = jnp.abs(rel)
    large = max_exact + (jnp.log(jnp.maximum(n, 1).astype(jnp.float32) / max_exact)
                         / math.log(T5_MAX_DIST / max_exact) * (nb - max_exact)).astype(jnp.int32)
    large = jnp.minimum(large, nb - 1)
    return jnp.where(rel > 0, nb, 0) + jnp.where(n < max_exact, n, large)


def neighbourhood_attention(q, k, v, rpb, norm_w):
    B, T = q.shape[0], q.shape[1]
    rows = T // GRID_W
    kh = min(NA_KH, rows)
    r = jnp.arange(rows)
    row_start = jnp.clip(r - kh // 2, 0, rows - kh)
    key_rows = row_start[:, None] + jnp.arange(kh)[None, :]
    c = jnp.arange(GRID_W)
    col_start = jnp.clip(c - NA_KW // 2, 0, GRID_W - NA_KW)
    in_win = (c[None, :] >= col_start[:, None]) & (c[None, :] < col_start[:, None] + NA_KW)
    dr = key_rows - r[:, None] + (NA_KH - 1)
    dc = jnp.clip(c[None, :] - c[:, None], -(NA_KW - 1), NA_KW - 1) + (NA_KW - 1)
    bias = rpb[:, dr[:, None, :, None], dc[None, :, None, :]].astype(jnp.float32)
    mask = in_win[:, None, :]
    scale = NA_HEAD_DIM ** -0.5

    def one_seq(args):
        qs, ks, vs = args
        qg = qs.reshape(rows, GRID_W, NA_HEADS, NA_HEAD_DIM)
        kg = ks.reshape(rows, GRID_W, NA_HEADS, NA_HEAD_DIM)[key_rows]
        vg = vs.reshape(rows, GRID_W, NA_HEADS, NA_HEAD_DIM)[key_rows]
        s = jnp.einsum('rqhd,rikhd->hrqik', qg, kg).astype(jnp.float32) * scale + bias
        s = jnp.where(mask, s, -jnp.inf)
        p = jax.nn.softmax(s.reshape(NA_HEADS, rows, GRID_W, kh * GRID_W), axis=-1)
        p = p.reshape(s.shape).astype(vs.dtype)
        o = jnp.einsum('hrqik,rikhd->rqhd', p, vg)
        return o.reshape(T, NA_HEADS, NA_HEAD_DIM)

    o = lax.map(one_seq, (q, k, v))
    o = rms_norm(o, norm_w)
    return o.reshape(B, T, W_NA)


def diff_attention(q, k, v, t5_table, lam_q1, lam_k1, lam_q2, lam_k2, subln_w, lambda_init):
    B, T = q.shape[0], q.shape[1]
    nblk = T // Q_BLOCK
    f32 = jnp.float32
    lam = (jnp.exp(jnp.sum(lam_q1.astype(f32) * lam_k1.astype(f32)))
           - jnp.exp(jnp.sum(lam_q2.astype(f32) * lam_k2.astype(f32))) + lambda_init)
    scale = DIFF_HALF_DIM ** -0.5
    k_pos = jnp.arange(T)
    qb = q.reshape(B, nblk, Q_BLOCK, DIFF_HEADS, 2, DIFF_HALF_DIM).transpose(1, 0, 2, 3, 4, 5)

    def one_block(args):
        q_blk, start = args
        q_pos = start + jnp.arange(Q_BLOCK)
        bias = t5_table[t5_bucket(k_pos[None, :] - q_pos[:, None])]
        bias = bias.transpose(2, 0, 1).astype(f32)[:, None]
        s = jnp.einsum('bqhcd,bkhcd->bhcqk', q_blk, k).astype(f32) * scale + bias
        p = jax.nn.softmax(s, axis=-1)
        a = (p[:, :, 0] - lam * p[:, :, 1]).astype(v.dtype)
        return jnp.einsum('bhqk,bkhe->bqhe', a, v)

    o = lax.map(one_block, (qb, jnp.arange(nblk) * Q_BLOCK))
    o = o.transpose(1, 0, 2, 3, 4).reshape(B, T, DIFF_HEADS, DIFF_V_DIM)
    o = rms_norm(o, subln_w) * (1.0 - lambda_init)
    return o.reshape(B, T, W_DIFF)


def expert_choice_ffn(h, w_router, w_gate, w_up, w_down):
    D = h.shape[-1]
    xt = h.reshape(-1, D)
    n = xt.shape[0]
    cap = CAPACITY_FACTOR * n // N_EXPERTS
    aff = jax.nn.softmax(jnp.einsum('nd,de->ne', xt, w_router).astype(jnp.float32), axis=-1)
    gates, idx = lax.top_k(aff.T, cap)

    def expert(args):
        idx_e, g_e, wg, wu, wd = args
        xe = xt[idx_e]
        he = jax.nn.silu(xe @ wg) * (xe @ wu)
        return (he @ wd) * g_e[:, None].astype(xt.dtype)

    ye = lax.map(expert, (idx, gates, w_gate, w_up, w_down))
    out = jnp.zeros_like(xt).at[idx.reshape(-1)].add(ye.reshape(-1, D))
    return out.reshape(h.shape)


def run_trunk(x, mix_norm_w, w_in, na_rpb, na_norm_w, t5_table, lambda_q1, lambda_k1,
              lambda_q2, lambda_k2, subln_w, w_out, ffn_norm_w, w_router, w_gate, w_up,
              w_down, final_norm_w):
    B, T = x.shape[0], x.shape[1]
    for l in range(DEPTH):
        lambda_init = 0.8 - 0.6 * math.exp(-0.3 * l)
        h = rms_norm(x, mix_norm_w[l])
        proj = jnp.einsum('btd,de->bte', h, w_in[l])
        qa, ka, va, qd, kd, vd = jnp.split(proj, 6, axis=-1)
        ya = neighbourhood_attention(
            qa.reshape(B, T, NA_HEADS, NA_HEAD_DIM), ka.reshape(B, T, NA_HEADS, NA_HEAD_DIM),
            va.reshape(B, T, NA_HEADS, NA_HEAD_DIM), na_rpb[l], na_norm_w[l])
        yb = diff_attention(
            qd.reshape(B, T, DIFF_HEADS, 2, DIFF_HALF_DIM), kd.reshape(B, T, DIFF_HEADS, 2, DIFF_HALF_DIM),
            vd.reshape(B, T, DIFF_HEADS, DIFF_V_DIM), t5_table, lambda_q1[l], lambda_k1[l],
            lambda_q2[l], lambda_k2[l], subln_w[l], lambda_init)
        x = x + jnp.einsum('bte,ed->btd', jnp.concatenate([ya, yb], axis=-1), w_out[l])
        x = x + expert_choice_ffn(rms_norm(x, ffn_norm_w[l]), w_router[l], w_gate[l], w_up[l], w_down[l])
    return rms_norm(x, final_norm_w)


def setup_inputs(seed: int = 0) -> dict:
    key = jax.random.key(seed)
    ks = jax.random.split(key, 20)
    f32 = jnp.float32
    nrm = lambda k, shape, s: (jax.random.normal(k, shape, f32) * s)
    return {
        'x_prompt': nrm(ks[0], (BATCH, SEQ, D_MODEL), 1.0),
        'x_sample': nrm(ks[1], (DEC_BATCH, DEC_SEQ, D_MODEL), 1.0),
        'mix_norm_w': 1.0 + nrm(ks[2], (DEPTH, D_MODEL), 0.02),
        'w_in': nrm(ks[3], (DEPTH, D_MODEL, D_IN), D_MODEL ** -0.5),
        'na_rpb': nrm(ks[4], (DEPTH, NA_HEADS, 2 * NA_KH - 1, 2 * NA_KW - 1), 0.2),
        'na_norm_w': 1.0 + nrm(ks[5], (DEPTH, NA_HEAD_DIM), 0.02),
        't5_table': nrm(ks[6], (T5_BUCKETS, DIFF_HEADS), 0.2),
        'lambda_q1': nrm(ks[7], (DEPTH, DIFF_HALF_DIM), 0.1),
        'lambda_k1': nrm(ks[8], (DEPTH, DIFF_HALF_DIM), 0.1),
        'lambda_q2': nrm(ks[9], (DEPTH, DIFF_HALF_DIM), 0.1),
        'lambda_k2': nrm(ks[10], (DEPTH, DIFF_HALF_DIM), 0.1),
        'subln_w': 1.0 + nrm(ks[11], (DEPTH, DIFF_V_DIM), 0.02),
        'w_out': nrm(ks[12], (DEPTH, D_MIX, D_MODEL), D_MIX ** -0.5),
        'ffn_norm_w': 1.0 + nrm(ks[13], (DEPTH, D_MODEL), 0.02),
        'w_router': nrm(ks[14], (DEPTH, D_MODEL, N_EXPERTS), D_MODEL ** -0.5),
        'w_gate': nrm(ks[15], (DEPTH, N_EXPERTS, D_MODEL, D_EXPERT), D_MODEL ** -0.5),
        'w_up': nrm(ks[16], (DEPTH, N_EXPERTS, D_MODEL, D_EXPERT), D_MODEL ** -0.5),
        'w_down': nrm(ks[17], (DEPTH, N_EXPERTS, D_EXPERT, D_MODEL), D_EXPERT ** -0.5),
        'final_norm_w': 1.0 + nrm(ks[18], (D_MODEL,), 0.02),
    }


def reference(x_prompt, x_sample, mix_norm_w, w_in, na_rpb, na_norm_w, t5_table, lambda_q1,
              lambda_k1, lambda_q2, lambda_k2, subln_w, w_out, ffn_norm_w, w_router, w_gate,
              w_up, w_down, final_norm_w):
    y_prompt = run_trunk(x_prompt, mix_norm_w, w_in, na_rpb, na_norm_w, t5_table, lambda_q1,
                         lambda_k1, lambda_q2, lambda_k2, subln_w, w_out, ffn_norm_w, w_router,
                         w_gate, w_up, w_down, final_norm_w)
    y_sample = run_trunk(x_sample, mix_norm_w, w_in, na_rpb, na_norm_w, t5_table, lambda_q1,
                         lambda_k1, lambda_q2, lambda_k2, subln_w, w_out, ffn_norm_w, w_router,
                         w_gate, w_up, w_down, final_norm_w)
    return (y_prompt, y_sample)
```

```python
import functools
import math

import jax
import jax.numpy as jnp
from jax import lax
from jax.experimental import pallas as pl
from jax.experimental.pallas import tpu as pltpu

F32 = jnp.float32
BF16 = jnp.bfloat16

D_MODEL = 1024
GRID_W = 64
NA_HEADS = 8
NA_HEAD_DIM = 64
NA_KH = 8
NA_KW = 16
DIFF_HEADS = 4
DIFF_HALF_DIM = 64
DIFF_V_DIM = 128
W_NA = 512
W_DIFF = 512
D_IN = 3072
T5_BUCKETS = 32
T5_MAX_DIST = 128
N_EXPERTS = 16
CAPACITY_FACTOR = 2
D_EXPERT = 2816
EPS = 1e-6
LAMBDA_INIT = 0.8 - 0.6 * math.exp(-0.3 * 0)

LANES = 128
TOKEN_TILE = 512
NA_ROWS_PER_STEP = 8
DIFF_BLOCK = 512
MOE_TILE = 512
FFN_CHUNKS = ((0, 512), (512, 512), (1024, 512), (1536, 512), (2048, 512), (2560, 256))
MASKED = -1e30
VMEM_LIMIT = 56 * 1024 * 1024

_NT = (((1,), (1,)), ((), ()))


def _params(sem, vmem=VMEM_LIMIT):
    return pltpu.CompilerParams(dimension_semantics=sem, vmem_limit_bytes=vmem)


def _inproj_body(x_ref, nw_ref, w_ref, o_ref):
    x = x_ref[...]
    ms = jnp.mean(x * x, axis=-1, keepdims=True)
    h = ((x * lax.rsqrt(ms + EPS)) * nw_ref[...]).astype(BF16)
    for c in range(D_IN // W_NA):
        cols = slice(c * W_NA, (c + 1) * W_NA)
        o_ref[:, cols] = jnp.dot(h, w_ref[:, cols], preferred_element_type=F32).astype(BF16)


def _inproj(x2d, norm_w, w_in):
    n = x2d.shape[0]
    tm = TOKEN_TILE
    return pl.pallas_call(
        _inproj_body,
        grid=(n // tm,),
        in_specs=[
            pl.BlockSpec((tm, D_MODEL), lambda i: (i, 0)),
            pl.BlockSpec((1, D_MODEL), lambda i: (0, 0)),
            pl.BlockSpec((D_MODEL, D_IN), lambda i: (0, 0)),
        ],
        out_specs=pl.BlockSpec((tm, D_IN), lambda i: (i, 0)),
        out_shape=jax.ShapeDtypeStruct((n, D_IN), BF16),
        compiler_params=_params(("parallel",)),
        name="inproj",
    )(x2d, norm_w, w_in)


def _na_bias_table(rpb):
    c = jnp.arange(GRID_W)
    col_start = jnp.clip(c - NA_KW // 2, 0, GRID_W - NA_KW)
    in_win = (c[None, :] >= col_start[:, None]) & (c[None, :] < col_start[:, None] + NA_KW)
    dc = jnp.clip(c[None, :] - c[:, None], -(NA_KW - 1), NA_KW - 1) + (NA_KW - 1)
    bc = rpb.astype(F32)[:, :, dc]
    bc = jnp.where(in_win[None, None], bc, MASKED)
    tabs = jnp.stack([bc[:, d0:d0 + NA_KH] for d0 in range(NA_KH)])
    tabs = tabs.transpose(0, 1, 3, 2, 4)
    return tabs.reshape(NA_KH, NA_HEADS, GRID_W, NA_KH * GRID_W)


def _na_body(q_ref, kp_ref, kc_ref, kn_ref, vp_ref, vc_ref, vn_ref, bias_ref, nw_ref, o_ref,
             kbuf, vbuf, *, rows):
    j = pl.program_id(1)
    blk = NA_ROWS_PER_STEP * GRID_W
    kbuf[0:blk] = kp_ref[...]
    kbuf[blk:2 * blk] = kc_ref[...]
    kbuf[2 * blk:3 * blk] = kn_ref[...]
    vbuf[0:blk] = vp_ref[...]
    vbuf[blk:2 * blk] = vc_ref[...]
    vbuf[2 * blk:3 * blk] = vn_ref[...]
    lo = lax.broadcasted_iota(jnp.int32, (GRID_W, LANES), 1) < NA_HEAD_DIM
    nw = nw_ref[...]
    nkeys = NA_KH * GRID_W

    def softmax_parts(s):
        m = jnp.max(s, axis=-1, keepdims=True)
        p = jnp.exp(s - m)
        return p.astype(BF16), jnp.sum(p, axis=-1, keepdims=True)

    def one_row(a, carry):
        r = j * NA_ROWS_PER_STEP + a
        row_start = jnp.clip(r - NA_KH // 2, 0, rows - NA_KH)
        d0 = row_start - r + (NA_KH - 1)
        koff = pl.multiple_of((row_start - (j - 1) * NA_ROWS_PER_STEP) * GRID_W, GRID_W)
        qoff = pl.multiple_of(a * GRID_W, GRID_W)
        for hp in range(NA_HEADS // 2):
            cs = slice(hp * LANES, (hp + 1) * LANES)
            qp = q_ref[pl.ds(qoff, GRID_W), cs]
            kk = kbuf[pl.ds(koff, nkeys), cs]
            vv = vbuf[pl.ds(koff, nkeys), cs]
            zero = jnp.zeros_like(qp)
            sa = lax.dot_general(jnp.where(lo, qp, zero), kk, _NT, preferred_element_type=F32)
            sb = lax.dot_general(jnp.where(lo, zero, qp), kk, _NT, preferred_element_type=F32)
            pa, la = softmax_parts(sa + bias_ref[d0, 2 * hp])
            pb, lb = softmax_parts(sb + bias_ref[d0, 2 * hp + 1])
            oa = jnp.dot(pa, vv, preferred_element_type=F32)
            ob = jnp.dot(pb, vv, preferred_element_type=F32)
            o = jnp.where(lo, oa / la, ob / lb)
            sq = o * o
            msa = jnp.sum(jnp.where(lo, sq, 0.0), axis=-1, keepdims=True) * (1.0 / NA_HEAD_DIM)
            msb = jnp.sum(jnp.where(lo, 0.0, sq), axis=-1, keepdims=True) * (1.0 / NA_HEAD_DIM)
            inv = jnp.where(lo, lax.rsqrt(msa + EPS), lax.rsqrt(msb + EPS))
            o_ref[pl.ds(qoff, GRID_W), cs] = ((o * inv) * nw).astype(BF16)
        return carry

    lax.fori_loop(0, NA_ROWS_PER_STEP, one_row, 0)


def _na_attention(proj, bias_tab, norm_w2, batch, seq):
    rows = seq // GRID_W
    assert rows % NA_ROWS_PER_STEP == 0 and rows >= NA_KH
    nb = rows // NA_ROWS_PER_STEP
    blk = NA_ROWS_PER_STEP * GRID_W
    n = batch * seq

    def at(col, shift):
        return pl.BlockSpec(
            (blk, W_NA), lambda b, j: (b * nb + jnp.clip(j + shift, 0, nb - 1), col))

    return pl.pallas_call(
        functools.partial(_na_body, rows=rows),
        grid=(batch, nb),
        in_specs=[
            at(0, 0),
            at(1, -1), at(1, 0), at(1, 1),
            at(2, -1), at(2, 0), at(2, 1),
            pl.BlockSpec((NA_KH, NA_HEADS, GRID_W, NA_KH * GRID_W), lambda b, j: (0, 0, 0, 0)),
            pl.BlockSpec((1, LANES), lambda b, j: (0, 0)),
        ],
        out_specs=pl.BlockSpec((blk, W_NA), lambda b, j: (b * nb + j, 0)),
        out_shape=jax.ShapeDtypeStruct((n, W_NA), BF16),
        scratch_shapes=[pltpu.VMEM((3 * blk, W_NA), BF16), pltpu.VMEM((3 * blk, W_NA), BF16)],
        compiler_params=_params(("parallel", "parallel")),
        name="na_attention",
    )(proj, proj, proj, proj, proj, proj, proj, bias_tab, norm_w2)


def _t5_bucket(rel):
    nb = T5_BUCKETS // 2
    max_exact = nb // 2
    n = jnp.abs(rel)
    large = max_exact + (jnp.log(jnp.maximum(n, 1).astype(F32) / max_exact)
                         / math.log(T5_MAX_DIST / max_exact) * (nb - max_exact)).astype(jnp.int32)
    large = jnp.minimum(large, nb - 1)
    return jnp.where(rel > 0, nb, 0) + jnp.where(n < max_exact, n, large)


def _diff_bias_table(t5_table):
    x = DIFF_BLOCK
    assert x + 1 >= T5_MAX_DIST
    pos = jnp.arange(x)
    tiles = []
    for d in range(-2, 3):
        rel = pos[None, :] + d * x - pos[:, None]
        tiles.append(t5_table.astype(F32)[_t5_bucket(rel)])
    return jnp.stack(tiles).transpose(3, 0, 1, 2)


def _diff_body(q_ref, k_ref, v_ref, bias_ref, lam_ref, sw_ref, o_ref, *, nblk):
    i = pl.program_id(2)
    x = DIFF_BLOCK
    q = q_ref[...]
    lo = lax.broadcasted_iota(jnp.int32, (x, LANES), 1) < DIFF_HALF_DIM
    zero = jnp.zeros_like(q)
    q_halves = (jnp.where(lo, q, zero), jnp.where(lo, zero, q))

    def body(j, carry):
        off = pl.multiple_of(j * x, x)
        kc = k_ref[pl.ds(off, x), :]
        vc = v_ref[pl.ds(off, x), :]
        bt = bias_ref[0, jnp.clip(j - i, -2, 2) + 2]
        out = []
        for c in range(2):
            m, l, acc = carry[3 * c:3 * c + 3]
            s = lax.dot_general(q_halves[c], kc, _NT, preferred_element_type=F32) + bt
            mn = jnp.maximum(m, jnp.max(s, axis=-1, keepdims=True))
            alpha = jnp.exp(m - mn)
            p = jnp.exp(s - mn)
            l = alpha * l + jnp.sum(p, axis=-1, keepdims=True)
            acc = alpha * acc + jnp.dot(p.astype(BF16), vc, preferred_element_type=F32)
            out += [mn, l, acc]
        return tuple(out)

    init = (jnp.full((x, 1), -jnp.inf, F32), jnp.zeros((x, 1), F32), jnp.zeros((x, LANES), F32)) * 2
    _, l1, acc1, _, l2, acc2 = lax.fori_loop(0, nblk, body, init)
    lam = (jnp.exp(jnp.sum(lam_ref[0:1, :] * lam_ref[1:2, :], axis=-1, keepdims=True))
           - jnp.exp(jnp.sum(lam_ref[2:3, :] * lam_ref[3:4, :], axis=-1, keepdims=True)) + LAMBDA_INIT)
    o = acc1 / l1 - lam * (acc2 / l2)
    ms = jnp.mean(o * o, axis=-1, keepdims=True)
    y = ((o * lax.rsqrt(ms + EPS)) * sw_ref[...]) * (1.0 - LAMBDA_INIT)
    o_ref[...] = y.astype(BF16)


def _diff_attention(proj, bias_tab, lam_vecs, subln_w, batch, seq):
    x = DIFF_BLOCK
    assert seq % x == 0
    nblk = seq // x
    n = batch * seq
    qcol, kcol, vcol = 3 * W_NA // LANES, (3 * W_NA + W_DIFF) // LANES, (3 * W_NA + 2 * W_DIFF) // LANES
    return pl.pallas_call(
        functools.partial(_diff_body, nblk=nblk),
        grid=(batch, DIFF_HEADS, nblk),
        in_specs=[
            pl.BlockSpec((x, LANES), lambda b, h, i: (b * nblk + i, qcol + h)),
            pl.BlockSpec((seq, LANES), lambda b, h, i: (b, kcol + h)),
            pl.BlockSpec((seq, LANES), lambda b, h, i: (b, vcol + h)),
            pl.BlockSpec((1, 5, x, x), lambda b, h, i: (h, 0, 0, 0)),
            pl.BlockSpec((4, DIFF_HALF_DIM), lambda b, h, i: (0, 0)),
            pl.BlockSpec((1, DIFF_V_DIM), lambda b, h, i: (0, 0)),
        ],
        out_specs=pl.BlockSpec((x, LANES), lambda b, h, i: (b * nblk + i, h)),
        out_shape=jax.ShapeDtypeStruct((n, W_DIFF), BF16),
        compiler_params=_params(("parallel", "parallel", "parallel")),
        name="diff_attention",
    )(proj, proj, proj, bias_tab, lam_vecs, subln_w)


def _outproj_body(x_ref, ya_ref, yb_ref, wa_ref, wb_ref, nw_ref, wr_ref, x2_ref, h2_ref, aff_ref):
    attn = (jnp.dot(ya_ref[...], wa_ref[...], preferred_element_type=F32)
            + jnp.dot(yb_ref[...], wb_ref[...], preferred_element_type=F32))
    x2 = x_ref[...] + attn
    x2_ref[...] = x2
    ms = jnp.mean(x2 * x2, axis=-1, keepdims=True)
    h2 = (x2 * lax.rsqrt(ms + EPS)) * nw_ref[...]
    h2_ref[...] = h2
    logits = lax.dot_general(wr_ref[...], h2, _NT, precision=lax.Precision.HIGHEST,
                             preferred_element_type=F32)
    e = jnp.exp(logits - jnp.max(logits, axis=0, keepdims=True))
    aff_ref[...] = e / jnp.sum(e, axis=0, keepdims=True)


def _outproj(x2d, ya, yb, w_out, norm_w, w_router_t):
    n = x2d.shape[0]
    tm = TOKEN_TILE
    row = lambda i: (i, 0)
    fixed = lambda i: (0, 0)
    return pl.pallas_call(
        _outproj_body,
        grid=(n // tm,),
        in_specs=[
            pl.BlockSpec((tm, D_MODEL), row),
            pl.BlockSpec((tm, W_NA), row),
            pl.BlockSpec((tm, W_DIFF), row),
            pl.BlockSpec((W_NA, D_MODEL), lambda i: (0, 0)),
            pl.BlockSpec((W_DIFF, D_MODEL), lambda i: (1, 0)),
            pl.BlockSpec((1, D_MODEL), fixed),
            pl.BlockSpec((N_EXPERTS, D_MODEL), fixed),
        ],
        out_specs=[
            pl.BlockSpec((tm, D_MODEL), row),
            pl.BlockSpec((tm, D_MODEL), row),
            pl.BlockSpec((N_EXPERTS, tm), lambda i: (0, i)),
        ],
        out_shape=[
            jax.ShapeDtypeStruct((n, D_MODEL), F32),
            jax.ShapeDtypeStruct((n, D_MODEL), F32),
            jax.ShapeDtypeStruct((N_EXPERTS, n), F32),
        ],
        compiler_params=_params(("parallel",)),
        name="outproj_router",
    )(x2d, ya, yb, w_out, w_out, norm_w, w_router_t)


def _moe_body(idx_ref, idx_next_ref, h2_hbm, g_ref, wg_ref, wu_ref, wd_ref, o_ref, xbuf, sem, *, tm):
    nt = pl.num_programs(1)
    step = pl.program_id(0) * nt + pl.program_id(1)
    total = pl.num_programs(0) * nt
    slot = step % 2

    def row_copy(tok, i, sl):
        return pltpu.make_async_copy(h2_hbm.at[pl.ds(tok, 1), :], xbuf.at[sl, pl.ds(i, 1), :], sem.at[sl])

    def issue(iref, sl):
        def body(i, carry):
            row_copy(iref[0, 0, i], i, sl).start()
            return carry
        lax.fori_loop(0, tm, body, 0, unroll=8)

    @pl.when(step == 0)
    def _():
        issue(idx_ref, 0)

    @pl.when(step + 1 < total)
    def _():
        issue(idx_next_ref, 1 - slot)

    pltpu.make_async_copy(h2_hbm.at[pl.ds(0, tm), :], xbuf.at[slot], sem.at[slot]).wait()
    x = xbuf[slot].astype(BF16)
    acc = jnp.zeros((tm, D_MODEL), F32)
    for f0, fc in FFN_CHUNKS:
        g = jnp.dot(x, wg_ref[0, :, f0:f0 + fc], preferred_element_type=F32)
        u = jnp.dot(x, wu_ref[0, :, f0:f0 + fc], preferred_element_type=F32)
        h = (jax.nn.silu(g) * u).astype(BF16)
        acc = acc + jnp.dot(h, wd_ref[0, f0:f0 + fc, :], preferred_element_type=F32)
    o_ref[...] = acc * g_ref[...]


def _moe_experts(h2, idx, gates, w_gate, w_up, w_down):
    n_exp, cap = idx.shape
    tm = min(MOE_TILE, cap)
    assert cap % tm == 0
    nt = cap // tm
    idx3 = idx.reshape(n_exp * nt, 1, tm)

    def next_tile(e, t):
        return (jnp.minimum(e * nt + t + 1, n_exp * nt - 1), 0, 0)

    return pl.pallas_call(
        functools.partial(_moe_body, tm=tm),
        grid=(n_exp, nt),
        in_specs=[
            pl.BlockSpec((1, 1, tm), lambda e, t: (e * nt + t, 0, 0), memory_space=pltpu.SMEM),
            pl.BlockSpec((1, 1, tm), next_tile, memory_space=pltpu.SMEM),
            pl.BlockSpec(memory_space=pl.ANY),
            pl.BlockSpec((tm, 1), lambda e, t: (e * nt + t, 0)),
            pl.BlockSpec((1, D_MODEL, D_EXPERT), lambda e, t: (e, 0, 0)),
            pl.BlockSpec((1, D_MODEL, D_EXPERT), lambda e, t: (e, 0, 0)),
            pl.BlockSpec((1, D_EXPERT, D_MODEL), lambda e, t: (e, 0, 0)),
        ],
        out_specs=pl.BlockSpec((tm, D_MODEL), lambda e, t: (e * nt + t, 0)),
        out_shape=jax.ShapeDtypeStruct((n_exp * cap, D_MODEL), F32),
        scratch_shapes=[pltpu.VMEM((2, tm, D_MODEL), F32), pltpu.SemaphoreType.DMA((2,))],
        compiler_params=_params(("arbitrary", "arbitrary")),
        name="moe_experts",
    )(idx3, idx3, h2, gates.reshape(n_exp * cap, 1), w_gate, w_up, w_down)


def _final_body(x2_ref, moe_ref, nw_ref, o_ref):
    x3 = x2_ref[...] + moe_ref[...]
    ms = jnp.mean(x3 * x3, axis=-1, keepdims=True)
    o_ref[...] = (x3 * lax.rsqrt(ms + EPS)) * nw_ref[...]


def _final(x2, moe, norm_w):
    n = x2.shape[0]
    tm = TOKEN_TILE
    row = lambda i: (i, 0)
    return pl.pallas_call(
        _final_body,
        grid=(n // tm,),
        in_specs=[pl.BlockSpec((tm, D_MODEL), row), pl.BlockSpec((tm, D_MODEL), row),
                  pl.BlockSpec((1, D_MODEL), lambda i: (0, 0))],
        out_specs=pl.BlockSpec((tm, D_MODEL), row),
        out_shape=jax.ShapeDtypeStruct((n, D_MODEL), F32),
        compiler_params=_params(("parallel",)),
        name="final_norm",
    )(x2, moe, norm_w)


def _trunk(x, p):
    batch, seq = x.shape[0], x.shape[1]
    n = batch * seq
    x2d = x.reshape(n, D_MODEL)
    proj = _inproj(x2d, p["mix_norm_w"], p["w_in"])
    ya = _na_attention(proj, p["na_bias"], p["na_norm_w2"], batch, seq)
    yb = _diff_attention(proj, p["diff_bias"], p["lam_vecs"], p["subln_w"], batch, seq)
    x2, h2, aff_t = _outproj(x2d, ya, yb, p["w_out"], p["ffn_norm_w"], p["w_router_t"])
    cap = CAPACITY_FACTOR * n // N_EXPERTS
    gates, idx = lax.top_k(aff_t, cap)
    ye = _moe_experts(h2, idx, gates, p["w_gate"], p["w_up"], p["w_down"])
    moe = jnp.zeros((n, D_MODEL), F32).at[idx.reshape(-1)].add(ye)
    y = _final(x2, moe, p["final_norm_w"])
    return y.reshape(batch, seq, D_MODEL)


def kernel(x_prompt, x_sample, mix_norm_w, w_in, na_rpb, na_norm_w, t5_table, lambda_q1, lambda_k1,
           lambda_q2, lambda_k2, subln_w, w_out, ffn_norm_w, w_router, w_gate, w_up, w_down, final_norm_w):
    qscale = jnp.ones((D_IN,), F32)
    qscale = qscale.at[0:W_NA].set(NA_HEAD_DIM ** -0.5).at[3 * W_NA:3 * W_NA + W_DIFF].set(DIFF_HALF_DIM ** -0.5)
    p = {
        "mix_norm_w": mix_norm_w[0].reshape(1, D_MODEL),
        "w_in": (w_in[0] * qscale[None, :]).astype(BF16),
        "na_bias": _na_bias_table(na_rpb[0]),
        "na_norm_w2": jnp.tile(na_norm_w[0], 2).reshape(1, LANES),
        "diff_bias": _diff_bias_table(t5_table),
        "lam_vecs": jnp.stack([lambda_q1[0], lambda_k1[0], lambda_q2[0], lambda_k2[0]]).astype(F32),
        "subln_w": subln_w[0].reshape(1, DIFF_V_DIM),
        "w_out": w_out[0].astype(BF16),
        "ffn_norm_w": ffn_norm_w[0].reshape(1, D_MODEL),
        "w_router_t": w_router[0].T,
        "w_gate": w_gate[0].astype(BF16),
        "w_up": w_up[0].astype(BF16),
        "w_down": w_down[0].astype(BF16),
        "final_norm_w": final_norm_w.reshape(1, D_MODEL),
    }
    return (_trunk(x_prompt, p), _trunk(x_sample, p))
```

```python
import functools
import math

import jax
import jax.numpy as jnp
from jax import lax
from jax.experimental import pallas as pl
from jax.experimental.pallas import tpu as pltpu

F32 = jnp.float32
BF16 = jnp.bfloat16

D_MODEL = 1024
GRID_W = 64
NA_HEADS = 8
NA_HEAD_DIM = 64
NA_KH = 8
NA_KW = 16
DIFF_HEADS = 4
DIFF_HALF_DIM = 64
DIFF_V_DIM = 128
W_NA = 512
W_DIFF = 512
D_IN = 3072
T5_BUCKETS = 32
T5_MAX_DIST = 128
N_EXPERTS = 16
CAPACITY_FACTOR = 2
D_EXPERT = 2816
EPS = 1e-6
LAMBDA_INIT = 0.8 - 0.6 * math.exp(-0.3 * 0)

LANES = 128
TOKEN_TILE = 512
NA_ROWS_PER_STEP = 8
DIFF_BLOCK = 512
MOE_TILE = 512
ROUTE_CHUNK = 256
ROUTE_CHUNKS_PER_STEP = 8
COMBINE_TILE = 256
COMBINE_DMA_ROWS = 32
FFN_CHUNKS = ((0, 512), (512, 512), (1024, 512), (1536, 512), (2048, 512), (2560, 256))
MASKED = -1e30
VMEM_LIMIT = 56 * 1024 * 1024

_NT = (((1,), (1,)), ((), ()))


def _params(sem, vmem=VMEM_LIMIT):
    return pltpu.CompilerParams(dimension_semantics=sem, vmem_limit_bytes=vmem)


def _inproj_body(x_ref, nw_ref, w_ref, o_ref):
    x = x_ref[...]
    ms = jnp.mean(x * x, axis=-1, keepdims=True)
    h = ((x * lax.rsqrt(ms + EPS)) * nw_ref[...]).astype(BF16)
    for c in range(D_IN // W_NA):
        cols = slice(c * W_NA, (c + 1) * W_NA)
        o_ref[:, cols] = jnp.dot(h, w_ref[:, cols], preferred_element_type=F32).astype(BF16)


def _inproj(x2d, norm_w, w_in):
    n = x2d.shape[0]
    tm = TOKEN_TILE
    return pl.pallas_call(
        _inproj_body,
        grid=(n // tm,),
        in_specs=[
            pl.BlockSpec((tm, D_MODEL), lambda i: (i, 0)),
            pl.BlockSpec((1, D_MODEL), lambda i: (0, 0)),
            pl.BlockSpec((D_MODEL, D_IN), lambda i: (0, 0)),
        ],
        out_specs=pl.BlockSpec((tm, D_IN), lambda i: (i, 0)),
        out_shape=jax.ShapeDtypeStruct((n, D_IN), BF16),
        compiler_params=_params(("parallel",)),
        name="inproj",
    )(x2d, norm_w, w_in)


def _na_bias_table(rpb):
    c = jnp.arange(GRID_W)
    col_start = jnp.clip(c - NA_KW // 2, 0, GRID_W - NA_KW)
    in_win = (c[None, :] >= col_start[:, None]) & (c[None, :] < col_start[:, None] + NA_KW)
    dc = jnp.clip(c[None, :] - c[:, None], -(NA_KW - 1), NA_KW - 1) + (NA_KW - 1)
    bc = rpb.astype(F32)[:, :, dc]
    bc = jnp.where(in_win[None, None], bc, MASKED)
    tabs = jnp.stack([bc[:, d0:d0 + NA_KH] for d0 in range(NA_KH)])
    tabs = tabs.transpose(0, 1, 3, 2, 4)
    return tabs.reshape(NA_KH, NA_HEADS, GRID_W, NA_KH * GRID_W)


def _na_body(q_ref, kp_ref, kc_ref, kn_ref, vp_ref, vc_ref, vn_ref, bias_ref, nw_ref, o_ref,
             kbuf, vbuf, *, rows):
    j = pl.program_id(1)
    blk = NA_ROWS_PER_STEP * GRID_W
    kbuf[0:blk] = kp_ref[...]
    kbuf[blk:2 * blk] = kc_ref[...]
    kbuf[2 * blk:3 * blk] = kn_ref[...]
    vbuf[0:blk] = vp_ref[...]
    vbuf[blk:2 * blk] = vc_ref[...]
    vbuf[2 * blk:3 * blk] = vn_ref[...]
    lo = lax.broadcasted_iota(jnp.int32, (GRID_W, LANES), 1) < NA_HEAD_DIM
    nw = nw_ref[...]
    nkeys = NA_KH * GRID_W

    def softmax_parts(s):
        m = jnp.max(s, axis=-1, keepdims=True)
        p = jnp.exp(s - m)
        return p.astype(BF16), jnp.sum(p, axis=-1, keepdims=True)

    def one_row(a, carry):
        r = j * NA_ROWS_PER_STEP + a
        row_start = jnp.clip(r - NA_KH // 2, 0, rows - NA_KH)
        d0 = row_start - r + (NA_KH - 1)
        koff = pl.multiple_of((row_start - (j - 1) * NA_ROWS_PER_STEP) * GRID_W, GRID_W)
        qoff = pl.multiple_of(a * GRID_W, GRID_W)
        for hp in range(NA_HEADS // 2):
            cs = slice(hp * LANES, (hp + 1) * LANES)
            qp = q_ref[pl.ds(qoff, GRID_W), cs]
            kk = kbuf[pl.ds(koff, nkeys), cs]
            vv = vbuf[pl.ds(koff, nkeys), cs]
            zero = jnp.zeros_like(qp)
            sa = lax.dot_general(jnp.where(lo, qp, zero), kk, _NT, preferred_element_type=F32)
            sb = lax.dot_general(jnp.where(lo, zero, qp), kk, _NT, preferred_element_type=F32)
            pa, la = softmax_parts(sa + bias_ref[d0, 2 * hp])
            pb, lb = softmax_parts(sb + bias_ref[d0, 2 * hp + 1])
            oa = jnp.dot(pa, vv, preferred_element_type=F32)
            ob = jnp.dot(pb, vv, preferred_element_type=F32)
            o = jnp.where(lo, oa / la, ob / lb)
            sq = o * o
            msa = jnp.sum(jnp.where(lo, sq, 0.0), axis=-1, keepdims=True) * (1.0 / NA_HEAD_DIM)
            msb = jnp.sum(jnp.where(lo, 0.0, sq), axis=-1, keepdims=True) * (1.0 / NA_HEAD_DIM)
            inv = jnp.where(lo, lax.rsqrt(msa + EPS), lax.rsqrt(msb + EPS))
            o_ref[pl.ds(qoff, GRID_W), cs] = ((o * inv) * nw).astype(BF16)
        return carry

    lax.fori_loop(0, NA_ROWS_PER_STEP, one_row, 0)


def _na_attention(proj, bias_tab, norm_w2, batch, seq):
    rows = seq // GRID_W
    assert rows % NA_ROWS_PER_STEP == 0 and rows >= NA_KH
    nb = rows // NA_ROWS_PER_STEP
    blk = NA_ROWS_PER_STEP * GRID_W
    n = batch * seq

    def at(col, shift):
        return pl.BlockSpec(
            (blk, W_NA), lambda b, j: (b * nb + jnp.clip(j + shift, 0, nb - 1), col))

    return pl.pallas_call(
        functools.partial(_na_body, rows=rows),
        grid=(batch, nb),
        in_specs=[
            at(0, 0),
            at(1, -1), at(1, 0), at(1, 1),
            at(2, -1), at(2, 0), at(2, 1),
            pl.BlockSpec((NA_KH, NA_HEADS, GRID_W, NA_KH * GRID_W), lambda b, j: (0, 0, 0, 0)),
            pl.BlockSpec((1, LANES), lambda b, j: (0, 0)),
        ],
        out_specs=pl.BlockSpec((blk, W_NA), lambda b, j: (b * nb + j, 0)),
        out_shape=jax.ShapeDtypeStruct((n, W_NA), BF16),
        scratch_shapes=[pltpu.VMEM((3 * blk, W_NA), BF16), pltpu.VMEM((3 * blk, W_NA), BF16)],
        compiler_params=_params(("parallel", "parallel")),
        name="na_attention",
    )(proj, proj, proj, proj, proj, proj, proj, bias_tab, norm_w2)


def _t5_bucket(rel):
    nb = T5_BUCKETS // 2
    max_exact = nb // 2
    n = jnp.abs(rel)
    large = max_exact + (jnp.log(jnp.maximum(n, 1).astype(F32) / max_exact)
                         / math.log(T5_MAX_DIST / max_exact) * (nb - max_exact)).astype(jnp.int32)
    large = jnp.minimum(large, nb - 1)
    return jnp.where(rel > 0, nb, 0) + jnp.where(n < max_exact, n, large)


def _diff_bias_table(t5_table):
    x = DIFF_BLOCK
    assert x + 1 >= T5_MAX_DIST
    span = 2 * x
    rel = jnp.arange(span)[None, :] - (x - 1) + jnp.arange(-2, 3)[:, None] * x
    diag = t5_table.astype(F32)[_t5_bucket(rel)].transpose(2, 0, 1)
    skew = jnp.tile(diag, (1, 1, x))[:, :, :x * (span - 1)].reshape(DIFF_HEADS, 5, x, span - 1)
    return skew[:, :, :, x - 1:2 * x - 1]


def _diff_body(q_ref, k_ref, v_ref, bias_ref, lam_ref, sw_ref, o_ref, *, nblk):
    i = pl.program_id(2)
    x = DIFF_BLOCK
    q = q_ref[...]
    lo = lax.broadcasted_iota(jnp.int32, (x, LANES), 1) < DIFF_HALF_DIM
    zero = jnp.zeros_like(q)
    q_halves = (jnp.where(lo, q, zero), jnp.where(lo, zero, q))

    def body(j, carry):
        off = pl.multiple_of(j * x, x)
        kc = k_ref[pl.ds(off, x), :]
        vc = v_ref[pl.ds(off, x), :]
        bt = bias_ref[0, jnp.clip(j - i, -2, 2) + 2]
        out = []
        for c in range(2):
            m, l, acc = carry[3 * c:3 * c + 3]
            s = lax.dot_general(q_halves[c], kc, _NT, preferred_element_type=F32) + bt
            mn = jnp.maximum(m, jnp.max(s, axis=-1, keepdims=True))
            alpha = jnp.exp(m - mn)
            p = jnp.exp(s - mn)
            l = alpha * l + jnp.sum(p, axis=-1, keepdims=True)
            acc = alpha * acc + jnp.dot(p.astype(BF16), vc, preferred_element_type=F32)
            out += [mn, l, acc]
        return tuple(out)

    init = (jnp.full((x, 1), -jnp.inf, F32), jnp.zeros((x, 1), F32), jnp.zeros((x, LANES), F32)) * 2
    _, l1, acc1, _, l2, acc2 = lax.fori_loop(0, nblk, body, init)
    lam = (jnp.exp(jnp.sum(lam_ref[0:1, :] * lam_ref[1:2, :], axis=-1, keepdims=True))
           - jnp.exp(jnp.sum(lam_ref[2:3, :] * lam_ref[3:4, :], axis=-1, keepdims=True)) + LAMBDA_INIT)
    o = acc1 / l1 - lam * (acc2 / l2)
    ms = jnp.mean(o * o, axis=-1, keepdims=True)
    y = ((o * lax.rsqrt(ms + EPS)) * sw_ref[...]) * (1.0 - LAMBDA_INIT)
    o_ref[...] = y.astype(BF16)


def _diff_attention(proj, bias_tab, lam_vecs, subln_w, batch, seq):
    x = DIFF_BLOCK
    assert seq % x == 0
    nblk = seq // x
    n = batch * seq
    qcol, kcol, vcol = 3 * W_NA // LANES, (3 * W_NA + W_DIFF) // LANES, (3 * W_NA + 2 * W_DIFF) // LANES
    return pl.pallas_call(
        functools.partial(_diff_body, nblk=nblk),
        grid=(batch, DIFF_HEADS, nblk),
        in_specs=[
            pl.BlockSpec((x, LANES), lambda b, h, i: (b * nblk + i, qcol + h)),
            pl.BlockSpec((seq, LANES), lambda b, h, i: (b, kcol + h)),
            pl.BlockSpec((seq, LANES), lambda b, h, i: (b, vcol + h)),
            pl.BlockSpec((1, 5, x, x), lambda b, h, i: (h, 0, 0, 0)),
            pl.BlockSpec((4, DIFF_HALF_DIM), lambda b, h, i: (0, 0)),
            pl.BlockSpec((1, DIFF_V_DIM), lambda b, h, i: (0, 0)),
        ],
        out_specs=pl.BlockSpec((x, LANES), lambda b, h, i: (b * nblk + i, h)),
        out_shape=jax.ShapeDtypeStruct((n, W_DIFF), BF16),
        compiler_params=_params(("parallel", "parallel", "parallel")),
        name="diff_attention",
    )(proj, proj, proj, bias_tab, lam_vecs, subln_w)


def _outproj_body(x_ref, ya_ref, yb_ref, wa_ref, wb_ref, nw_ref, wr_ref, x2_ref, h2_ref, aff_ref):
    attn = (jnp.dot(ya_ref[...], wa_ref[...], preferred_element_type=F32)
            + jnp.dot(yb_ref[...], wb_ref[...], preferred_element_type=F32))
    x2 = x_ref[...] + attn
    x2_ref[...] = x2
    ms = jnp.mean(x2 * x2, axis=-1, keepdims=True)
    h2 = (x2 * lax.rsqrt(ms + EPS)) * nw_ref[...]
    h2_ref[...] = h2
    logits = lax.dot_general(wr_ref[...], h2, _NT, precision=lax.Precision.HIGHEST,
                             preferred_element_type=F32)
    e = jnp.exp(logits - jnp.max(logits, axis=0, keepdims=True))
    aff_ref[...] = e / jnp.sum(e, axis=0, keepdims=True)


def _outproj(x2d, ya, yb, w_out, norm_w, w_router_t):
    n = x2d.shape[0]
    tm = TOKEN_TILE
    row = lambda i: (i, 0)
    fixed = lambda i: (0, 0)
    return pl.pallas_call(
        _outproj_body,
        grid=(n // tm,),
        in_specs=[
            pl.BlockSpec((tm, D_MODEL), row),
            pl.BlockSpec((tm, W_NA), row),
            pl.BlockSpec((tm, W_DIFF), row),
            pl.BlockSpec((W_NA, D_MODEL), lambda i: (0, 0)),
            pl.BlockSpec((W_DIFF, D_MODEL), lambda i: (1, 0)),
            pl.BlockSpec((1, D_MODEL), fixed),
            pl.BlockSpec((N_EXPERTS, D_MODEL), fixed),
        ],
        out_specs=[
            pl.BlockSpec((tm, D_MODEL), row),
            pl.BlockSpec((tm, D_MODEL), row),
            pl.BlockSpec((N_EXPERTS, tm), lambda i: (0, i)),
        ],
        out_shape=[
            jax.ShapeDtypeStruct((n, D_MODEL), F32),
            jax.ShapeDtypeStruct((n, D_MODEL), F32),
            jax.ShapeDtypeStruct((N_EXPERTS, n), F32),
        ],
        compiler_params=_params(("parallel",)),
        name="outproj_router",
    )(x2d, ya, yb, w_out, w_out, norm_w, w_router_t)


def _moe_body(idx_ref, idx_next_ref, h2_hbm, g_ref, wg_ref, wu_ref, wd_ref, o_ref, xbuf, sem, *, tm):
    nt = pl.num_programs(1)
    step = pl.program_id(0) * nt + pl.program_id(1)
    total = pl.num_programs(0) * nt
    slot = step % 2

    def row_copy(tok, i, sl):
        return pltpu.make_async_copy(h2_hbm.at[pl.ds(tok, 1), :], xbuf.at[sl, pl.ds(i, 1), :], sem.at[sl])

    def issue(iref, sl):
        def body(i, carry):
            row_copy(iref[0, 0, i], i, sl).start()
            return carry
        lax.fori_loop(0, tm, body, 0, unroll=8)

    @pl.when(step == 0)
    def _():
        issue(idx_ref, 0)

    @pl.when(step + 1 < total)
    def _():
        issue(idx_next_ref, 1 - slot)

    pltpu.make_async_copy(h2_hbm.at[pl.ds(0, tm), :], xbuf.at[slot], sem.at[slot]).wait()
    x = xbuf[slot].astype(BF16)
    acc = jnp.zeros((tm, D_MODEL), F32)
    for f0, fc in FFN_CHUNKS:
        g = jnp.dot(x, wg_ref[0, :, f0:f0 + fc], preferred_element_type=F32)
        u = jnp.dot(x, wu_ref[0, :, f0:f0 + fc], preferred_element_type=F32)
        h = (jax.nn.silu(g) * u).astype(BF16)
        acc = acc + jnp.dot(h, wd_ref[0, f0:f0 + fc, :], preferred_element_type=F32)
    o_ref[...] = (acc * g_ref[...]).astype(o_ref.dtype)


def _moe_experts(h2, idx, gates, w_gate, w_up, w_down):
    n_exp, cap = idx.shape
    tm = min(MOE_TILE, cap)
    assert cap % tm == 0
    nt = cap // tm
    idx3 = idx.reshape(n_exp * nt, 1, tm)

    def next_tile(e, t):
        return (jnp.minimum(e * nt + t + 1, n_exp * nt - 1), 0, 0)

    return pl.pallas_call(
        functools.partial(_moe_body, tm=tm),
        grid=(n_exp, nt),
        in_specs=[
            pl.BlockSpec((1, 1, tm), lambda e, t: (e * nt + t, 0, 0), memory_space=pltpu.SMEM),
            pl.BlockSpec((1, 1, tm), next_tile, memory_space=pltpu.SMEM),
            pl.BlockSpec(memory_space=pl.ANY),
            pl.BlockSpec((tm, 1), lambda e, t: (e * nt + t, 0)),
            pl.BlockSpec((1, D_MODEL, D_EXPERT), lambda e, t: (e, 0, 0)),
            pl.BlockSpec((1, D_MODEL, D_EXPERT), lambda e, t: (e, 0, 0)),
            pl.BlockSpec((1, D_EXPERT, D_MODEL), lambda e, t: (e, 0, 0)),
        ],
        out_specs=pl.BlockSpec((tm, D_MODEL), lambda e, t: (e * nt + t, 0)),
        out_shape=jax.ShapeDtypeStruct((n_exp * cap, D_MODEL), BF16),
        scratch_shapes=[pltpu.VMEM((2, tm, D_MODEL), F32), pltpu.SemaphoreType.DMA((2,))],
        compiler_params=_params(("arbitrary", "arbitrary")),
        name="moe_experts",
    )(idx3, idx3, h2, gates.reshape(n_exp * cap, 1), w_gate, w_up, w_down)


def _route_threshold_body(aff_ref, theta_ref, need_ref, *, cap):
    n_exp = aff_ref.shape[0]

    def count_ge(cand):
        bits = pltpu.bitcast(aff_ref[...], jnp.int32)
        return jnp.sum(jnp.where(bits >= cand, 1.0, 0.0), axis=1, keepdims=True)

    def step(b, theta):
        cand = theta | jnp.left_shift(jnp.int32(1), 30 - b)
        return jnp.where(count_ge(cand) >= cap, cand, theta)

    theta = lax.fori_loop(0, 31, step, jnp.zeros((n_exp, 1), jnp.int32))
    need = cap - count_ge(theta + 1)
    theta_ref[...] = jnp.broadcast_to(theta, theta_ref.shape)
    need_ref[...] = jnp.broadcast_to(need, need_ref.shape)


def _route_threshold(aff_t, cap):
    n_exp, n = aff_t.shape
    full = lambda i: (0, 0)
    return pl.pallas_call(
        functools.partial(_route_threshold_body, cap=cap),
        grid=(1,),
        in_specs=[pl.BlockSpec((n_exp, n), full)],
        out_specs=[pl.BlockSpec((n_exp, LANES), full), pl.BlockSpec((n_exp, LANES), full)],
        out_shape=[jax.ShapeDtypeStruct((n_exp, LANES), jnp.int32), jax.ShapeDtypeStruct((n_exp, LANES), F32)],
        compiler_params=_params(("arbitrary",)),
        name="route_threshold",
    )(aff_t)


def _route_prefix_body(aff_ref, tri_ref, theta_ref, need_ref, slot_ref, base_ref, run_sel, run_tie, *, cap):
    @pl.when(pl.program_id(0) == 0)
    def _():
        run_sel[...] = jnp.zeros_like(run_sel)
        run_tie[...] = jnp.zeros_like(run_tie)

    n_exp = aff_ref.shape[0]
    w = ROUTE_CHUNK
    theta = theta_ref[:, 0:1]
    need = need_ref[:, 0:1]
    expert_base = lax.broadcasted_iota(jnp.int32, (n_exp, 1), 0) * cap
    tri = tri_ref[...]
    for k in range(ROUTE_CHUNKS_PER_STEP):
        bits = pltpu.bitcast(aff_ref[:, k * w:(k + 1) * w], jnp.int32)
        tie = bits == theta
        tie_f = jnp.where(tie, 1.0, 0.0)
        tie_incl = jnp.dot(tie_f.astype(BF16), tri, preferred_element_type=F32)
        tie_rank = run_tie[:, 0:1] + tie_incl - tie_f
        sel = (bits > theta) | (tie & (tie_rank < need))
        sel_f = jnp.where(sel, 1.0, 0.0)
        incl = jnp.dot(sel_f.astype(BF16), tri, preferred_element_type=F32)
        rank = run_sel[:, 0:1] + incl - sel_f
        slot_ref[:, k * w:(k + 1) * w] = jnp.where(sel, rank.astype(jnp.int32) + expert_base, -1)
        base_ref[k] = run_sel[...].astype(jnp.int32)
        run_sel[...] = run_sel[...] + jnp.sum(sel_f, axis=1, keepdims=True)
        run_tie[...] = run_tie[...] + jnp.sum(tie_f, axis=1, keepdims=True)


def _route_prefix(aff_t, theta, need, cap):
    n_exp, n = aff_t.shape
    w = ROUTE_CHUNK
    span = w * ROUTE_CHUNKS_PER_STEP
    assert n % span == 0
    tri = (jnp.arange(w)[:, None] <= jnp.arange(w)[None, :]).astype(BF16)
    fixed = lambda i: (0, 0)
    return pl.pallas_call(
        functools.partial(_route_prefix_body, cap=cap),
        grid=(n // span,),
        in_specs=[
            pl.BlockSpec((n_exp, span), lambda i: (0, i)),
            pl.BlockSpec((w, w), fixed),
            pl.BlockSpec((n_exp, LANES), fixed),
            pl.BlockSpec((n_exp, LANES), fixed),
        ],
        out_specs=[
            pl.BlockSpec((n_exp, span), lambda i: (0, i)),
            pl.BlockSpec((ROUTE_CHUNKS_PER_STEP, n_exp, LANES), lambda i: (i, 0, 0)),
        ],
        out_shape=[
            jax.ShapeDtypeStruct((n_exp, n), jnp.int32),
            jax.ShapeDtypeStruct((n // w, n_exp, LANES), jnp.int32),
        ],
        scratch_shapes=[pltpu.VMEM((n_exp, LANES), F32), pltpu.VMEM((n_exp, LANES), F32)],
        compiler_params=_params(("arbitrary",)),
        name="route_prefix",
    )(aff_t, tri, theta, need)


def _route_compact_body(cb_ref, slot_ref, aff_ref, idx_ref, gate_ref, *, cap, nchunks):
    e = pl.program_id(0)
    w = ROUTE_CHUNK
    s = LANES
    slot_iota = lax.broadcasted_iota(jnp.int32, (s, w), 0)
    tok_iota = lax.broadcasted_iota(jnp.int32, (s, w), 1)
    eye = lax.broadcasted_iota(jnp.int32, (s, s), 0) == lax.broadcasted_iota(jnp.int32, (s, s), 1)

    def fold(v):
        out = v[:, 0:s]
        for q in range(1, w // s):
            out = out + v[:, q * s:(q + 1) * s]
        return out

    def to_row(acc):
        col = jnp.sum(acc, axis=1, keepdims=True)
        return jnp.sum(jnp.where(eye, col, 0.0), axis=0, keepdims=True)

    def one_tile(jt, c_first):
        j0 = jt * s
        c_first = lax.while_loop(lambda c: cb_ref[e, c + 1] <= j0, lambda c: c + 1, c_first)
        want = slot_iota + (e * cap + j0)

        def more(state):
            c = state[0]
            return jnp.logical_and(c < nchunks, cb_ref[e, jnp.minimum(c, nchunks - 1)] < j0 + s)

        def chunk(state):
            c, acc_i, acc_g = state
            hit = want == slot_ref[0, pl.ds(c, 1), :]
            tok = (tok_iota + c * w).astype(F32)
            acc_i = acc_i + fold(jnp.where(hit, tok, 0.0))
            acc_g = acc_g + fold(jnp.where(hit, aff_ref[0, pl.ds(c, 1), :], 0.0))
            return c + 1, acc_i, acc_g

        zero = jnp.zeros((s, s), F32)
        _, acc_i, acc_g = lax.while_loop(more, chunk, (c_first, zero, zero))
        idx_ref[0, pl.ds(jt, 1), :] = to_row(acc_i).astype(jnp.int32)
        gate_ref[0, pl.ds(jt, 1), :] = to_row(acc_g)
        return c_first

    lax.fori_loop(0, cap // s, one_tile, jnp.int32(0))


def _route_compact(chunk_base, slot3, aff3, cap):
    n_exp, nchunks, w = slot3.shape
    assert cap % LANES == 0
    per_e = lambda e, cb: (e, 0, 0)
    grid_spec = pltpu.PrefetchScalarGridSpec(
        num_scalar_prefetch=1,
        grid=(n_exp,),
        in_specs=[pl.BlockSpec((1, nchunks, w), per_e), pl.BlockSpec((1, nchunks, w), per_e)],
        out_specs=[pl.BlockSpec((1, cap // LANES, LANES), per_e), pl.BlockSpec((1, cap // LANES, LANES), per_e)],
    )
    return pl.pallas_call(
        functools.partial(_route_compact_body, cap=cap, nchunks=nchunks),
        grid_spec=grid_spec,
        out_shape=[
            jax.ShapeDtypeStruct((n_exp, cap // LANES, LANES), jnp.int32),
            jax.ShapeDtypeStruct((n_exp, cap // LANES, LANES), F32),
        ],
        compiler_params=_params(("arbitrary",)),
        name="route_compact",
    )(chunk_base, slot3, aff3)


def _combine_body(cb_ref, slot_ref, x2_ref, ye_hbm, nw_ref, o_ref, stage, chunk_row, chunk_count, sem,
                  *, cap, ntiles):
    tile = pl.program_id(0)
    cur = tile % 2
    tt = COMBINE_TILE
    ch = COMBINE_DMA_ROWS
    n_exp = slot_ref.shape[0]
    kblock = 256 // ch

    def chunk_copy(row, k, sl):
        return pltpu.make_async_copy(ye_hbm.at[pl.ds(row, ch), :], stage.at[sl, pl.ds(k * ch, ch), :], sem.at[sl])

    def issue(t, sl):
        def per_expert(e, k):
            lo = cb_ref[e, t]
            hi = cb_ref[e, t + 1]
            first = lo // ch
            nch = jnp.where(hi > lo, (hi - 1) // ch - first + 1, 0)

            def per_chunk(i, k):
                row = pl.multiple_of(e * cap + (first + i) * ch, ch)
                chunk_copy(row, k, sl).start()
                chunk_row[sl, k] = row
                return k + 1

            return lax.fori_loop(0, nch, per_chunk, k)

        chunk_count[sl] = lax.fori_loop(0, n_exp, per_expert, jnp.int32(0))

    @pl.when(tile == 0)
    def _():
        issue(0, 0)

    @pl.when(tile + 1 < ntiles)
    def _():
        issue(tile + 1, 1 - cur)

    nchunk = chunk_count[cur]

    def wait_one(k, carry):
        chunk_copy(0, k, cur).wait()
        return carry

    lax.fori_loop(0, nchunk, wait_one, 0)
    nblock = (nchunk + kblock - 1) // kblock

    def pad_one(k, carry):
        stage[cur, pl.ds(pl.multiple_of(k * ch, ch), ch), :] = jnp.zeros((ch, D_MODEL), BF16)
        chunk_row[cur, k] = -(1 << 30)
        return carry

    lax.fori_loop(nchunk, nblock * kblock, pad_one, 0)
    row_iota = lax.broadcasted_iota(jnp.int32, (ch, tt), 0)

    def one_block(b, acc):
        parts = []
        for i in range(kblock):
            row = chunk_row[cur, b * kblock + i]
            e = jnp.clip(row // cap, 0, n_exp - 1)
            hit = (row_iota + row) == slot_ref[pl.ds(e, 1), :]
            parts.append(jnp.where(hit, 1.0, 0.0).astype(BF16))
        onehot_t = jnp.concatenate(parts, axis=0)
        rows = stage[cur, pl.ds(pl.multiple_of(b * 256, 256), 256), :]
        return acc + lax.dot_general(onehot_t, rows, (((0,), (0,)), ((), ())), preferred_element_type=F32)

    moe = lax.fori_loop(0, nblock, one_block, jnp.zeros((tt, D_MODEL), F32))
    x3 = x2_ref[...] + moe
    ms = jnp.mean(x3 * x3, axis=-1, keepdims=True)
    o_ref[...] = (x3 * lax.rsqrt(ms + EPS)) * nw_ref[...]


def _combine(tile_base, slot, x2, ye, norm_w, cap):
    n_exp, n = slot.shape
    tt = COMBINE_TILE
    ch = COMBINE_DMA_ROWS
    ntiles = n // tt
    assert cap % ch == 0 and tt % ch == 0
    max_rows = n_exp * (tt + ch)
    max_rows = -(-max_rows // 256) * 256
    grid_spec = pltpu.PrefetchScalarGridSpec(
        num_scalar_prefetch=1,
        grid=(ntiles,),
        in_specs=[
            pl.BlockSpec((n_exp, tt), lambda t, cb: (0, t)),
            pl.BlockSpec((tt, D_MODEL), lambda t, cb: (t, 0)),
            pl.BlockSpec(memory_space=pl.ANY),
            pl.BlockSpec((1, D_MODEL), lambda t, cb: (0, 0)),
        ],
        out_specs=pl.BlockSpec((tt, D_MODEL), lambda t, cb: (t, 0)),
        scratch_shapes=[
            pltpu.VMEM((2, max_rows, D_MODEL), BF16),
            pltpu.SMEM((2, max_rows // ch), jnp.int32),
            pltpu.SMEM((2,), jnp.int32),
            pltpu.SemaphoreType.DMA((2,)),
        ],
    )
    return pl.pallas_call(
        functools.partial(_combine_body, cap=cap, ntiles=ntiles),
        grid_spec=grid_spec,
        out_shape=jax.ShapeDtypeStruct((n, D_MODEL), F32),
        compiler_params=_params(("arbitrary",)),
        name="combine_final",
    )(tile_base, slot, x2, ye, norm_w)


def _trunk(x, p):
    batch, seq = x.shape[0], x.shape[1]
    n = batch * seq
    x2d = x.reshape(n, D_MODEL)
    proj = _inproj(x2d, p["mix_norm_w"], p["w_in"])
    ya = _na_attention(proj, p["na_bias"], p["na_norm_w2"], batch, seq)
    yb = _diff_attention(proj, p["diff_bias"], p["lam_vecs"], p["subln_w"], batch, seq)
    x2, h2, aff_t = _outproj(x2d, ya, yb, p["w_out"], p["ffn_norm_w"], p["w_router_t"])
    cap = CAPACITY_FACTOR * n // N_EXPERTS
    theta, need = _route_threshold(aff_t, cap)
    slot, base = _route_prefix(aff_t, theta, need, cap)
    nchunks = n // ROUTE_CHUNK
    chunk_base = jnp.concatenate([base[:, :, 0].T, jnp.full((N_EXPERTS, 1), cap, jnp.int32)], axis=1)
    idx, gates = _route_compact(chunk_base, slot.reshape(N_EXPERTS, nchunks, ROUTE_CHUNK),
                                aff_t.reshape(N_EXPERTS, nchunks, ROUTE_CHUNK), cap)
    ye = _moe_experts(h2, idx.reshape(N_EXPERTS, cap), gates.reshape(N_EXPERTS, cap),
                      p["w_gate"], p["w_up"], p["w_down"])
    tile_base = chunk_base[:, ::COMBINE_TILE // ROUTE_CHUNK]
    y = _combine(tile_base, slot, x2, ye, p["final_norm_w"], cap)
    return y.reshape(batch, seq, D_MODEL)


def kernel(x_prompt, x_sample, mix_norm_w, w_in, na_rpb, na_norm_w, t5_table, lambda_q1, lambda_k1,
           lambda_q2, lambda_k2, subln_w, w_out, ffn_norm_w, w_router, w_gate, w_up, w_down, final_norm_w):
    qscale = jnp.ones((D_IN,), F32)
    qscale = qscale.at[0:W_NA].set(NA_HEAD_DIM ** -0.5).at[3 * W_NA:3 * W_NA + W_DIFF].set(DIFF_HALF_DIM ** -0.5)
    p = {
        "mix_norm_w": mix_norm_w[0].reshape(1, D_MODEL),
        "w_in": (w_in[0] * qscale[None, :]).astype(BF16),
        "na_bias": _na_bias_table(na_rpb[0]),
        "na_norm_w2": jnp.tile(na_norm_w[0], 2).reshape(1, LANES),
        "diff_bias": _diff_bias_table(t5_table),
        "lam_vecs": jnp.stack([lambda_q1[0], lambda_k1[0], lambda_q2[0], lambda_k2[0]]).astype(F32),
        "subln_w": subln_w[0].reshape(1, DIFF_V_DIM),
        "w_out": w_out[0].astype(BF16),
        "ffn_norm_w": ffn_norm_w[0].reshape(1, D_MODEL),
        "w_router_t": w_router[0].T,
        "w_gate": w_gate[0].astype(BF16),
        "w_up": w_up[0].astype(BF16),
        "w_down": w_down[0].astype(BF16),
        "final_norm_w": final_norm_w.reshape(1, D_MODEL),
    }
    return (_trunk(x_prompt, p), _trunk(x_sample, p))
```

```python
import functools
import math

import jax
import jax.numpy as jnp
from jax import lax
from jax.experimental import pallas as pl
from jax.experimental.pallas import tpu as pltpu

F32 = jnp.float32
BF16 = jnp.bfloat16

D_MODEL = 1024
GRID_W = 64
NA_HEADS = 8
NA_HEAD_DIM = 64
NA_KH = 8
NA_KW = 16
DIFF_HEADS = 4
DIFF_HALF_DIM = 64
DIFF_V_DIM = 128
W_NA = 512
W_DIFF = 512
D_IN = 3072
T5_BUCKETS = 32
T5_MAX_DIST = 128
N_EXPERTS = 16
CAPACITY_FACTOR = 2
D_EXPERT = 2816
EPS = 1e-6
LAMBDA_INIT = 0.8 - 0.6 * math.exp(-0.3 * 0)
LOG2E = math.log2(math.e)
TOKEN_MAJOR_COLS = 3 * W_NA + W_DIFF

LANES = 128
TOKEN_TILE = 512
NA_ROWS_PER_STEP = 8
DIFF_BLOCK = 512
MOE_TILE = 512
ROUTE_CHUNK = 256
ROUTE_CHUNKS_PER_STEP = 8
COMBINE_TILE = 256
COMBINE_DMA_ROWS = 32
FFN_CHUNKS = ((0, 512), (512, 512), (1024, 512), (1536, 512), (2048, 512), (2560, 256))
MASKED = -1e30
VMEM_LIMIT = 56 * 1024 * 1024

_NT = (((1,), (1,)), ((), ()))


def _params(sem, vmem=VMEM_LIMIT):
    return pltpu.CompilerParams(dimension_semantics=sem, vmem_limit_bytes=vmem)


def _inproj_body(x_ref, nw_ref, w_ref, wqt_ref, wvt_ref, o_ref, qt_ref, vt_ref):
    x = x_ref[...]
    ms = jnp.mean(x * x, axis=-1, keepdims=True)
    h = ((x * lax.rsqrt(ms + EPS)) * nw_ref[...]).astype(BF16)
    for c in range(TOKEN_MAJOR_COLS // W_NA):
        cols = slice(c * W_NA, (c + 1) * W_NA)
        o_ref[:, cols] = jnp.dot(h, w_ref[:, cols], preferred_element_type=F32).astype(BF16)
    qt = lax.dot_general(wqt_ref[...], h, _NT, preferred_element_type=F32)
    qt_ref[0] = (qt * LOG2E).astype(BF16)
    vt_ref[0] = lax.dot_general(wvt_ref[...], h, _NT, preferred_element_type=F32).astype(BF16)


def _inproj(x2d, norm_w, w_tok, w_qd_t, w_vd_t):
    n = x2d.shape[0]
    tm = TOKEN_TILE
    fixed = lambda i: (0, 0)
    return pl.pallas_call(
        _inproj_body,
        grid=(n // tm,),
        in_specs=[
            pl.BlockSpec((tm, D_MODEL), lambda i: (i, 0)),
            pl.BlockSpec((1, D_MODEL), fixed),
            pl.BlockSpec((D_MODEL, TOKEN_MAJOR_COLS), fixed),
            pl.BlockSpec((W_DIFF, D_MODEL), fixed),
            pl.BlockSpec((W_DIFF, D_MODEL), fixed),
        ],
        out_specs=[
            pl.BlockSpec((tm, TOKEN_MAJOR_COLS), lambda i: (i, 0)),
            pl.BlockSpec((1, W_DIFF, tm), lambda i: (i, 0, 0)),
            pl.BlockSpec((1, W_DIFF, tm), lambda i: (i, 0, 0)),
        ],
        out_shape=[
            jax.ShapeDtypeStruct((n, TOKEN_MAJOR_COLS), BF16),
            jax.ShapeDtypeStruct((n // tm, W_DIFF, tm), BF16),
            jax.ShapeDtypeStruct((n // tm, W_DIFF, tm), BF16),
        ],
        compiler_params=_params(("parallel",)),
        name="inproj",
    )(x2d, norm_w, w_tok, w_qd_t, w_vd_t)


def _na_bias_table(rpb):
    c = jnp.arange(GRID_W)
    col_start = jnp.clip(c - NA_KW // 2, 0, GRID_W - NA_KW)
    in_win = (c[None, :] >= col_start[:, None]) & (c[None, :] < col_start[:, None] + NA_KW)
    dc = jnp.clip(c[None, :] - c[:, None], -(NA_KW - 1), NA_KW - 1) + (NA_KW - 1)
    bc = rpb.astype(F32)[:, :, dc]
    bc = jnp.where(in_win[None, None], bc, MASKED)
    tabs = jnp.stack([bc[:, d0:d0 + NA_KH] for d0 in range(NA_KH)])
    tabs = tabs.transpose(0, 1, 3, 2, 4)
    return tabs.reshape(NA_KH, NA_HEADS // 2, 2 * GRID_W, NA_KH * GRID_W)


def _na_body(q_ref, kp_ref, kc_ref, kn_ref, vp_ref, vc_ref, vn_ref, bias_ref, nw_ref, o_ref,
             kbuf, vbuf, *, rows):
    j = pl.program_id(1)
    blk = NA_ROWS_PER_STEP * GRID_W
    kbuf[0:blk] = kp_ref[...]
    kbuf[blk:2 * blk] = kc_ref[...]
    kbuf[2 * blk:3 * blk] = kn_ref[...]
    vbuf[0:blk] = vp_ref[...]
    vbuf[blk:2 * blk] = vc_ref[...]
    vbuf[2 * blk:3 * blk] = vn_ref[...]
    lo = lax.broadcasted_iota(jnp.int32, (GRID_W, LANES), 1) < NA_HEAD_DIM
    nw = nw_ref[...]
    nkeys = NA_KH * GRID_W

    def softmax_parts(s):
        m = jnp.max(s, axis=-1, keepdims=True)
        p = jnp.exp(s - m)
        return p.astype(BF16), jnp.sum(p, axis=-1, keepdims=True)

    def one_row(a, carry):
        r = j * NA_ROWS_PER_STEP + a
        row_start = jnp.clip(r - NA_KH // 2, 0, rows - NA_KH)
        d0 = row_start - r + (NA_KH - 1)
        koff = pl.multiple_of((row_start - (j - 1) * NA_ROWS_PER_STEP) * GRID_W, GRID_W)
        qoff = pl.multiple_of(a * GRID_W, GRID_W)
        pairs = range(NA_HEADS // 2)
        cols = [slice(hp * LANES, (hp + 1) * LANES) for hp in pairs]
        scores = []
        for hp in pairs:
            qp = q_ref[pl.ds(qoff, GRID_W), cols[hp]]
            zero = jnp.zeros_like(qp)
            q2 = jnp.concatenate([jnp.where(lo, qp, zero), jnp.where(lo, zero, qp)], axis=0)
            kk = kbuf[pl.ds(koff, nkeys), cols[hp]]
            scores.append(lax.dot_general(q2, kk, _NT, preferred_element_type=F32) + bias_ref[d0, hp])
        probs = [softmax_parts(s) for s in scores]
        outs = [jnp.dot(p, vbuf[pl.ds(koff, nkeys), cols[hp]], preferred_element_type=F32) / l
                for hp, (p, l) in zip(pairs, probs)]
        for hp in pairs:
            cs = cols[hp]
            o = jnp.where(lo, outs[hp][0:GRID_W], outs[hp][GRID_W:2 * GRID_W])
            sq = o * o
            msa = jnp.sum(jnp.where(lo, sq, 0.0), axis=-1, keepdims=True) * (1.0 / NA_HEAD_DIM)
            msb = jnp.sum(jnp.where(lo, 0.0, sq), axis=-1, keepdims=True) * (1.0 / NA_HEAD_DIM)
            inv = jnp.where(lo, lax.rsqrt(msa + EPS), lax.rsqrt(msb + EPS))
            o_ref[pl.ds(qoff, GRID_W), cs] = ((o * inv) * nw).astype(BF16)
        return carry

    lax.fori_loop(0, NA_ROWS_PER_STEP, one_row, 0, unroll=2)


def _na_attention(proj, bias_tab, norm_w2, batch, seq):
    rows = seq // GRID_W
    assert rows % NA_ROWS_PER_STEP == 0 and rows >= NA_KH
    nb = rows // NA_ROWS_PER_STEP
    blk = NA_ROWS_PER_STEP * GRID_W
    n = batch * seq

    def at(col, shift):
        return pl.BlockSpec(
            (blk, W_NA), lambda b, j: (b * nb + jnp.clip(j + shift, 0, nb - 1), col))

    return pl.pallas_call(
        functools.partial(_na_body, rows=rows),
        grid=(batch, nb),
        in_specs=[
            at(0, 0),
            at(1, -1), at(1, 0), at(1, 1),
            at(2, -1), at(2, 0), at(2, 1),
            pl.BlockSpec((NA_KH, NA_HEADS // 2, 2 * GRID_W, NA_KH * GRID_W), lambda b, j: (0, 0, 0, 0)),
            pl.BlockSpec((1, LANES), lambda b, j: (0, 0)),
        ],
        out_specs=pl.BlockSpec((blk, W_NA), lambda b, j: (b * nb + j, 0)),
        out_shape=jax.ShapeDtypeStruct((n, W_NA), BF16),
        scratch_shapes=[pltpu.VMEM((3 * blk, W_NA), BF16), pltpu.VMEM((3 * blk, W_NA), BF16)],
        compiler_params=_params(("parallel", "parallel")),
        name="na_attention",
    )(proj, proj, proj, proj, proj, proj, proj, bias_tab, norm_w2)


def _t5_bucket(rel):
    nb = T5_BUCKETS // 2
    max_exact = nb // 2
    n = jnp.abs(rel)
    large = max_exact + (jnp.log(jnp.maximum(n, 1).astype(F32) / max_exact)
                         / math.log(T5_MAX_DIST / max_exact) * (nb - max_exact)).astype(jnp.int32)
    large = jnp.minimum(large, nb - 1)
    return jnp.where(rel > 0, nb, 0) + jnp.where(n < max_exact, n, large)


def _diff_bias_table(t5_table):
    x = DIFF_BLOCK
    assert x + 1 >= T5_MAX_DIST
    span = 2 * x
    rel = (x - 1) - jnp.arange(span)[None, :] + jnp.arange(-2, 3)[:, None] * x
    diag = (t5_table.astype(F32) * LOG2E)[_t5_bucket(rel)].transpose(2, 0, 1)
    skew = jnp.tile(diag, (1, 1, x))[:, :, :x * (span - 1)].reshape(DIFF_HEADS, 5, x, span - 1)
    return skew[:, :, :, x - 1:2 * x - 1]


def _diff_body(qt_ref, k_ref, vt_ref, bias_ref, lam_ref, sw_ref, o_ref, s_even, s_odd, *, nblk):
    i = pl.program_id(2)
    x = DIFF_BLOCK
    qt = qt_ref[0]
    first = lax.broadcasted_iota(jnp.int32, (LANES, x), 0) < DIFF_HALF_DIM
    zero = jnp.zeros_like(qt)
    qt_both = jnp.concatenate([jnp.where(first, qt, zero), jnp.where(first, zero, qt)], axis=1)

    def scores(j, s_ref):
        kc = k_ref[pl.ds(pl.multiple_of(j * x, x), x), :]
        s_ref[...] = jnp.dot(kc, qt_both, preferred_element_type=F32)

    def consume(j, s_ref, carry):
        vt = vt_ref[j]
        bt = bias_ref[0, jnp.clip(j - i, -2, 2) + 2]
        out = []
        for c in range(2):
            m, l, acc = carry[3 * c:3 * c + 3]
            s = s_ref[:, c * x:(c + 1) * x] + bt
            mn = jnp.maximum(m, jnp.max(s, axis=0, keepdims=True))
            alpha = jnp.exp2(m - mn)
            p = jnp.exp2(s - mn)
            l = alpha * l + jnp.sum(p, axis=0, keepdims=True)
            acc = alpha * acc + jnp.dot(vt, p.astype(BF16), preferred_element_type=F32)
            out += [mn, l, acc]
        return tuple(out)

    def body(jj, carry):
        j = 2 * jj
        scores(j + 1, s_odd)
        carry = consume(j, s_even, carry)
        scores(j + 2, s_even)
        return consume(j + 1, s_odd, carry)

    init = (jnp.full((1, x), -jnp.inf, F32), jnp.zeros((1, x), F32), jnp.zeros((DIFF_V_DIM, x), F32)) * 2
    scores(0, s_even)
    carry = lax.fori_loop(0, nblk // 2 - 1, body, init)
    scores(nblk - 1, s_odd)
    carry = consume(nblk - 2, s_even, carry)
    _, l1, acc1, _, l2, acc2 = consume(nblk - 1, s_odd, carry)
    lam = (jnp.exp(jnp.sum(lam_ref[0:1, :] * lam_ref[1:2, :], axis=-1, keepdims=True))
           - jnp.exp(jnp.sum(lam_ref[2:3, :] * lam_ref[3:4, :], axis=-1, keepdims=True)) + LAMBDA_INIT)
    o = acc1 / l1 - lam * (acc2 / l2)
    ms = jnp.mean(o * o, axis=0, keepdims=True)
    y = ((o * lax.rsqrt(ms + EPS)) * sw_ref[...]) * (1.0 - LAMBDA_INIT)
    o_ref[0] = y.astype(BF16)


def _diff_attention(proj, qd_t, vd_t, bias_tab, lam_vecs, subln_col, batch, seq):
    x = DIFF_BLOCK
    assert seq % (2 * x) == 0 and x == TOKEN_TILE
    nblk = seq // x
    n = batch * seq
    kcol = 3 * W_NA // LANES
    return pl.pallas_call(
        functools.partial(_diff_body, nblk=nblk),
        grid=(batch, DIFF_HEADS, nblk),
        in_specs=[
            pl.BlockSpec((1, LANES, x), lambda b, h, i: (b * nblk + i, h, 0)),
            pl.BlockSpec((seq, LANES), lambda b, h, i: (b, kcol + h)),
            pl.BlockSpec((nblk, LANES, x), lambda b, h, i: (b, h, 0)),
            pl.BlockSpec((1, 5, x, x), lambda b, h, i: (h, 0, 0, 0)),
            pl.BlockSpec((4, DIFF_HALF_DIM), lambda b, h, i: (0, 0)),
            pl.BlockSpec((DIFF_V_DIM, 1), lambda b, h, i: (0, 0)),
        ],
        out_specs=pl.BlockSpec((1, LANES, x), lambda b, h, i: (b * nblk + i, h, 0)),
        out_shape=jax.ShapeDtypeStruct((n // x, W_DIFF, x), BF16),
        scratch_shapes=[pltpu.VMEM((x, 2 * x), F32), pltpu.VMEM((x, 2 * x), F32)],
        compiler_params=_params(("parallel", "parallel", "parallel")),
        name="diff_attention",
    )(qd_t, proj, vd_t, bias_tab, lam_vecs, subln_col)


def _outproj_body(x_ref, ya_ref, yb_ref, wa_ref, wb_ref, nw_ref, wr_ref, x2_ref, h2_ref, aff_ref):
    attn = (jnp.dot(ya_ref[...], wa_ref[...], preferred_element_type=F32)
            + lax.dot_general(yb_ref[0], wb_ref[...], (((0,), (0,)), ((), ())), preferred_element_type=F32))
    x2 = x_ref[...] + attn
    x2_ref[...] = x2
    ms = jnp.mean(x2 * x2, axis=-1, keepdims=True)
    h2 = (x2 * lax.rsqrt(ms + EPS)) * nw_ref[...]
    h2_ref[...] = h2
    logits = lax.dot_general(wr_ref[...], h2, _NT, precision=lax.Precision.HIGHEST,
                             preferred_element_type=F32)
    e = jnp.exp(logits - jnp.max(logits, axis=0, keepdims=True))
    aff_ref[...] = e / jnp.sum(e, axis=0, keepdims=True)


def _outproj(x2d, ya, yb, w_out, norm_w, w_router_t):
    n = x2d.shape[0]
    tm = TOKEN_TILE
    row = lambda i: (i, 0)
    fixed = lambda i: (0, 0)
    return pl.pallas_call(
        _outproj_body,
        grid=(n // tm,),
        in_specs=[
            pl.BlockSpec((tm, D_MODEL), row),
            pl.BlockSpec((tm, W_NA), row),
            pl.BlockSpec((1, W_DIFF, tm), lambda i: (i, 0, 0)),
            pl.BlockSpec((W_NA, D_MODEL), lambda i: (0, 0)),
            pl.BlockSpec((W_DIFF, D_MODEL), lambda i: (1, 0)),
            pl.BlockSpec((1, D_MODEL), fixed),
            pl.BlockSpec((N_EXPERTS, D_MODEL), fixed),
        ],
        out_specs=[
            pl.BlockSpec((tm, D_MODEL), row),
            pl.BlockSpec((tm, D_MODEL), row),
            pl.BlockSpec((N_EXPERTS, tm), lambda i: (0, i)),
        ],
        out_shape=[
            jax.ShapeDtypeStruct((n, D_MODEL), F32),
            jax.ShapeDtypeStruct((n, D_MODEL), F32),
            jax.ShapeDtypeStruct((N_EXPERTS, n), F32),
        ],
        compiler_params=_params(("parallel",)),
        name="outproj_router",
    )(x2d, ya, yb, w_out, w_out, norm_w, w_router_t)


def _moe_body(idx_ref, idx_next_ref, h2_hbm, g_ref, wg_ref, wu_ref, wd_ref, o_ref, xbuf, sem, *, tm):
    nt = pl.num_programs(1)
    step = pl.program_id(0) * nt + pl.program_id(1)
    total = pl.num_programs(0) * nt
    slot = step % 2

    def row_copy(tok, i, sl):
        return pltpu.make_async_copy(h2_hbm.at[pl.ds(tok, 1), :], xbuf.at[sl, pl.ds(i, 1), :], sem.at[sl])

    def issue(iref, sl):
        def body(i, carry):
            row_copy(iref[0, 0, i], i, sl).start()
            return carry
        lax.fori_loop(0, tm, body, 0, unroll=8)

    @pl.when(step == 0)
    def _():
        issue(idx_ref, 0)

    @pl.when(step + 1 < total)
    def _():
        issue(idx_next_ref, 1 - slot)

    pltpu.make_async_copy(h2_hbm.at[pl.ds(0, tm), :], xbuf.at[slot], sem.at[slot]).wait()
    x = xbuf[slot].astype(BF16)
    acc = jnp.zeros((tm, D_MODEL), F32)
    for f0, fc in FFN_CHUNKS:
        g = jnp.dot(x, wg_ref[0, :, f0:f0 + fc], preferred_element_type=F32)
        u = jnp.dot(x, wu_ref[0, :, f0:f0 + fc], preferred_element_type=F32)
        h = (jax.nn.silu(g) * u).astype(BF16)
        acc = acc + jnp.dot(h, wd_ref[0, f0:f0 + fc, :], preferred_element_type=F32)
    o_ref[...] = (acc * g_ref[...]).astype(o_ref.dtype)


def _moe_experts(h2, idx, gates, w_gate, w_up, w_down):
    n_exp, cap = idx.shape
    tm = min(MOE_TILE, cap)
    assert cap % tm == 0
    nt = cap // tm
    idx3 = idx.reshape(n_exp * nt, 1, tm)

    def next_tile(e, t):
        return (jnp.minimum(e * nt + t + 1, n_exp * nt - 1), 0, 0)

    return pl.pallas_call(
        functools.partial(_moe_body, tm=tm),
        grid=(n_exp, nt),
        in_specs=[
            pl.BlockSpec((1, 1, tm), lambda e, t: (e * nt + t, 0, 0), memory_space=pltpu.SMEM),
            pl.BlockSpec((1, 1, tm), next_tile, memory_space=pltpu.SMEM),
            pl.BlockSpec(memory_space=pl.ANY),
            pl.BlockSpec((tm, 1), lambda e, t: (e * nt + t, 0)),
            pl.BlockSpec((1, D_MODEL, D_EXPERT), lambda e, t: (e, 0, 0)),
            pl.BlockSpec((1, D_MODEL, D_EXPERT), lambda e, t: (e, 0, 0)),
            pl.BlockSpec((1, D_EXPERT, D_MODEL), lambda e, t: (e, 0, 0)),
        ],
        out_specs=pl.BlockSpec((tm, D_MODEL), lambda e, t: (e * nt + t, 0)),
        out_shape=jax.ShapeDtypeStruct((n_exp * cap, D_MODEL), BF16),
        scratch_shapes=[pltpu.VMEM((2, tm, D_MODEL), F32), pltpu.SemaphoreType.DMA((2,))],
        compiler_params=_params(("arbitrary", "arbitrary")),
        name="moe_experts",
    )(idx3, idx3, h2, gates.reshape(n_exp * cap, 1), w_gate, w_up, w_down)


def _route_threshold_body(aff_ref, theta_ref, need_ref, *, cap):
    n_exp = aff_ref.shape[0]

    def count_ge(cand):
        bits = pltpu.bitcast(aff_ref[...], jnp.int32)
        return jnp.sum(jnp.where(bits >= cand, 1.0, 0.0), axis=1, keepdims=True)

    def step(b, theta):
        cand = theta | jnp.left_shift(jnp.int32(1), 30 - b)
        return jnp.where(count_ge(cand) >= cap, cand, theta)

    theta = lax.fori_loop(0, 31, step, jnp.zeros((n_exp, 1), jnp.int32))
    need = cap - count_ge(theta + 1)
    theta_ref[...] = jnp.broadcast_to(theta, theta_ref.shape)
    need_ref[...] = jnp.broadcast_to(need, need_ref.shape)


def _route_threshold(aff_t, cap):
    n_exp, n = aff_t.shape
    full = lambda i: (0, 0)
    return pl.pallas_call(
        functools.partial(_route_threshold_body, cap=cap),
        grid=(1,),
        in_specs=[pl.BlockSpec((n_exp, n), full)],
        out_specs=[pl.BlockSpec((n_exp, LANES), full), pl.BlockSpec((n_exp, LANES), full)],
        out_shape=[jax.ShapeDtypeStruct((n_exp, LANES), jnp.int32), jax.ShapeDtypeStruct((n_exp, LANES), F32)],
        compiler_params=_params(("arbitrary",)),
        name="route_threshold",
    )(aff_t)


def _route_prefix_body(aff_ref, tri_ref, theta_ref, need_ref, slot_ref, base_ref, run_sel, run_tie, *, cap):
    @pl.when(pl.program_id(0) == 0)
    def _():
        run_sel[...] = jnp.zeros_like(run_sel)
        run_tie[...] = jnp.zeros_like(run_tie)

    n_exp = aff_ref.shape[0]
    w = ROUTE_CHUNK
    theta = theta_ref[:, 0:1]
    need = need_ref[:, 0:1]
    expert_base = lax.broadcasted_iota(jnp.int32, (n_exp, 1), 0) * cap
    tri = tri_ref[...]
    for k in range(ROUTE_CHUNKS_PER_STEP):
        bits = pltpu.bitcast(aff_ref[:, k * w:(k + 1) * w], jnp.int32)
        tie = bits == theta
        tie_f = jnp.where(tie, 1.0, 0.0)
        tie_incl = jnp.dot(tie_f.astype(BF16), tri, preferred_element_type=F32)
        tie_rank = run_tie[:, 0:1] + tie_incl - tie_f
        sel = (bits > theta) | (tie & (tie_rank < need))
        sel_f = jnp.where(sel, 1.0, 0.0)
        incl = jnp.dot(sel_f.astype(BF16), tri, preferred_element_type=F32)
        rank = run_sel[:, 0:1] + incl - sel_f
        slot_ref[:, k * w:(k + 1) * w] = jnp.where(sel, rank.astype(jnp.int32) + expert_base, -1)
        base_ref[k] = run_sel[...].astype(jnp.int32)
        run_sel[...] = run_sel[...] + jnp.sum(sel_f, axis=1, keepdims=True)
        run_tie[...] = run_tie[...] + jnp.sum(tie_f, axis=1, keepdims=True)


def _route_prefix(aff_t, theta, need, cap):
    n_exp, n = aff_t.shape
    w = ROUTE_CHUNK
    span = w * ROUTE_CHUNKS_PER_STEP
    assert n % span == 0
    tri = (jnp.arange(w)[:, None] <= jnp.arange(w)[None, :]).astype(BF16)
    fixed = lambda i: (0, 0)
    return pl.pallas_call(
        functools.partial(_route_prefix_body, cap=cap),
        grid=(n // span,),
        in_specs=[
            pl.BlockSpec((n_exp, span), lambda i: (0, i)),
            pl.BlockSpec((w, w), fixed),
            pl.BlockSpec((n_exp, LANES), fixed),
            pl.BlockSpec((n_exp, LANES), fixed),
        ],
        out_specs=[
            pl.BlockSpec((n_exp, span), lambda i: (0, i)),
            pl.BlockSpec((ROUTE_CHUNKS_PER_STEP, n_exp, LANES), lambda i: (i, 0, 0)),
        ],
        out_shape=[
            jax.ShapeDtypeStruct((n_exp, n), jnp.int32),
            jax.ShapeDtypeStruct((n // w, n_exp, LANES), jnp.int32),
        ],
        scratch_shapes=[pltpu.VMEM((n_exp, LANES), F32), pltpu.VMEM((n_exp, LANES), F32)],
        compiler_params=_params(("arbitrary",)),
        name="route_prefix",
    )(aff_t, tri, theta, need)


def _route_compact_body(cb_ref, slot_ref, aff_ref, idx_ref, gate_ref, *, cap, nchunks):
    e = pl.program_id(0)
    w = ROUTE_CHUNK
    s = LANES
    slot_iota = lax.broadcasted_iota(jnp.int32, (s, w), 0)
    tok_iota = lax.broadcasted_iota(jnp.int32, (s, w), 1)
    eye = lax.broadcasted_iota(jnp.int32, (s, s), 0) == lax.broadcasted_iota(jnp.int32, (s, s), 1)

    def fold(v):
        out = v[:, 0:s]
        for q in range(1, w // s):
            out = out + v[:, q * s:(q + 1) * s]
        return out

    def to_row(acc):
        col = jnp.sum(acc, axis=1, keepdims=True)
        return jnp.sum(jnp.where(eye, col, 0.0), axis=0, keepdims=True)

    def one_tile(jt, c_first):
        j0 = jt * s
        c_first = lax.while_loop(lambda c: cb_ref[e, c + 1] <= j0, lambda c: c + 1, c_first)
        want = slot_iota + (e * cap + j0)

        def more(state):
            c = state[0]
            return jnp.logical_and(c < nchunks, cb_ref[e, jnp.minimum(c, nchunks - 1)] < j0 + s)

        def chunk(state):
            c, acc_i, acc_g = state
            hit = want == slot_ref[0, pl.ds(c, 1), :]
            tok = (tok_iota + c * w).astype(F32)
            acc_i = acc_i + fold(jnp.where(hit, tok, 0.0))
            acc_g = acc_g + fold(jnp.where(hit, aff_ref[0, pl.ds(c, 1), :], 0.0))
            return c + 1, acc_i, acc_g

        zero = jnp.zeros((s, s), F32)
        _, acc_i, acc_g = lax.while_loop(more, chunk, (c_first, zero, zero))
        idx_ref[0, pl.ds(jt, 1), :] = to_row(acc_i).astype(jnp.int32)
        gate_ref[0, pl.ds(jt, 1), :] = to_row(acc_g)
        return c_first

    lax.fori_loop(0, cap // s, one_tile, jnp.int32(0))


def _route_compact(chunk_base, slot3, aff3, cap):
    n_exp, nchunks, w = slot3.shape
    assert cap % LANES == 0
    per_e = lambda e, cb: (e, 0, 0)
    grid_spec = pltpu.PrefetchScalarGridSpec(
        num_scalar_prefetch=1,
        grid=(n_exp,),
        in_specs=[pl.BlockSpec((1, nchunks, w), per_e), pl.BlockSpec((1, nchunks, w), per_e)],
        out_specs=[pl.BlockSpec((1, cap // LANES, LANES), per_e), pl.BlockSpec((1, cap // LANES, LANES), per_e)],
    )
    return pl.pallas_call(
        functools.partial(_route_compact_body, cap=cap, nchunks=nchunks),
        grid_spec=grid_spec,
        out_shape=[
            jax.ShapeDtypeStruct((n_exp, cap // LANES, LANES), jnp.int32),
            jax.ShapeDtypeStruct((n_exp, cap // LANES, LANES), F32),
        ],
        compiler_params=_params(("arbitrary",)),
        name="route_compact",
    )(chunk_base, slot3, aff3)


def _combine_body(cb_ref, slot_ref, x2_ref, ye_hbm, nw_ref, o_ref, stage, chunk_row, chunk_count, sem,
                  *, cap, ntiles):
    tile = pl.program_id(0)
    cur = tile % 2
    tt = COMBINE_TILE
    ch = COMBINE_DMA_ROWS
    n_exp = slot_ref.shape[0]
    kblock = 256 // ch

    def chunk_copy(row, k, sl):
        return pltpu.make_async_copy(ye_hbm.at[pl.ds(row, ch), :], stage.at[sl, pl.ds(k * ch, ch), :], sem.at[sl])

    def issue(t, sl):
        def per_expert(e, k):
            lo = cb_ref[e, t]
            hi = cb_ref[e, t + 1]
            first = lo // ch
            nch = jnp.where(hi > lo, (hi - 1) // ch - first + 1, 0)

            def per_chunk(i, k):
                row = pl.multiple_of(e * cap + (first + i) * ch, ch)
                chunk_copy(row, k, sl).start()
                chunk_row[sl, k] = row
                return k + 1

            return lax.fori_loop(0, nch, per_chunk, k)

        chunk_count[sl] = lax.fori_loop(0, n_exp, per_expert, jnp.int32(0))

    @pl.when(tile == 0)
    def _():
        issue(0, 0)

    @pl.when(tile + 1 < ntiles)
    def _():
        issue(tile + 1, 1 - cur)

    nchunk = chunk_count[cur]

    def wait_one(k, carry):
        chunk_copy(0, k, cur).wait()
        return carry

    lax.fori_loop(0, nchunk, wait_one, 0)
    nblock = (nchunk + kblock - 1) // kblock

    def pad_one(k, carry):
        stage[cur, pl.ds(pl.multiple_of(k * ch, ch), ch), :] = jnp.zeros((ch, D_MODEL), BF16)
        chunk_row[cur, k] = -(1 << 30)
        return carry

    lax.fori_loop(nchunk, nblock * kblock, pad_one, 0)
    row_iota = lax.broadcasted_iota(jnp.int32, (ch, tt), 0)

    def one_block(b, acc):
        parts = []
        for i in range(kblock):
            row = chunk_row[cur, b * kblock + i]
            e = jnp.clip(row // cap, 0, n_exp - 1)
            hit = (row_iota + row) == slot_ref[pl.ds(e, 1), :]
            parts.append(jnp.where(hit, 1.0, 0.0).astype(BF16))
        onehot_t = jnp.concatenate(parts, axis=0)
        rows = stage[cur, pl.ds(pl.multiple_of(b * 256, 256), 256), :]
        return acc + lax.dot_general(onehot_t, rows, (((0,), (0,)), ((), ())), preferred_element_type=F32)

    moe = lax.fori_loop(0, nblock, one_block, jnp.zeros((tt, D_MODEL), F32))
    x3 = x2_ref[...] + moe
    ms = jnp.mean(x3 * x3, axis=-1, keepdims=True)
    o_ref[...] = (x3 * lax.rsqrt(ms + EPS)) * nw_ref[...]


def _combine(tile_base, slot, x2, ye, norm_w, cap):
    n_exp, n = slot.shape
    tt = COMBINE_TILE
    ch = COMBINE_DMA_ROWS
    ntiles = n // tt
    assert cap % ch == 0 and tt % ch == 0
    max_rows = n_exp * (tt + ch)
    max_rows = -(-max_rows // 256) * 256
    grid_spec = pltpu.PrefetchScalarGridSpec(
        num_scalar_prefetch=1,
        grid=(ntiles,),
        in_specs=[
            pl.BlockSpec((n_exp, tt), lambda t, cb: (0, t)),
            pl.BlockSpec((tt, D_MODEL), lambda t, cb: (t, 0)),
            pl.BlockSpec(memory_space=pl.ANY),
            pl.BlockSpec((1, D_MODEL), lambda t, cb: (0, 0)),
        ],
        out_specs=pl.BlockSpec((tt, D_MODEL), lambda t, cb: (t, 0)),
        scratch_shapes=[
            pltpu.VMEM((2, max_rows, D_MODEL), BF16),
            pltpu.SMEM((2, max_rows // ch), jnp.int32),
            pltpu.SMEM((2,), jnp.int32),
            pltpu.SemaphoreType.DMA((2,)),
        ],
    )
    return pl.pallas_call(
        functools.partial(_combine_body, cap=cap, ntiles=ntiles),
        grid_spec=grid_spec,
        out_shape=jax.ShapeDtypeStruct((n, D_MODEL), F32),
        compiler_params=_params(("arbitrary",)),
        name="combine_final",
    )(tile_base, slot, x2, ye, norm_w)


def _trunk(x, p):
    batch, seq = x.shape[0], x.shape[1]
    n = batch * seq
    x2d = x.reshape(n, D_MODEL)
    proj, qd_t, vd_t = _inproj(x2d, p["mix_norm_w"], p["w_tok"], p["w_qd_t"], p["w_vd_t"])
    ya = _na_attention(proj, p["na_bias"], p["na_norm_w2"], batch, seq)
    yb_t = _diff_attention(proj, qd_t, vd_t, p["diff_bias"], p["lam_vecs"], p["subln_col"], batch, seq)
    x2, h2, aff_t = _outproj(x2d, ya, yb_t, p["w_out"], p["ffn_norm_w"], p["w_router_t"])
    cap = CAPACITY_FACTOR * n // N_EXPERTS
    theta, need = _route_threshold(aff_t, cap)
    slot, base = _route_prefix(aff_t, theta, need, cap)
    nchunks = n // ROUTE_CHUNK
    chunk_base = jnp.concatenate([base[:, :, 0].T, jnp.full((N_EXPERTS, 1), cap, jnp.int32)], axis=1)
    idx, gates = _route_compact(chunk_base, slot.reshape(N_EXPERTS, nchunks, ROUTE_CHUNK),
                                aff_t.reshape(N_EXPERTS, nchunks, ROUTE_CHUNK), cap)
    ye = _moe_experts(h2, idx.reshape(N_EXPERTS, cap), gates.reshape(N_EXPERTS, cap),
                      p["w_gate"], p["w_up"], p["w_down"])
    tile_base = chunk_base[:, ::COMBINE_TILE // ROUTE_CHUNK]
    y = _combine(tile_base, slot, x2, ye, p["final_norm_w"], cap)
    return y.reshape(batch, seq, D_MODEL)


def _prepare_params(mix_norm_w, w_in, na_rpb, na_norm_w, t5_table, lambda_q1, lambda_k1, lambda_q2, lambda_k2,
                    subln_w, w_out, ffn_norm_w, w_router, w_gate, w_up, w_down, final_norm_w):
    w = w_in[0]
    qa, ka_va = w[:, 0:W_NA] * NA_HEAD_DIM ** -0.5, w[:, W_NA:3 * W_NA]
    qd = w[:, 3 * W_NA:3 * W_NA + W_DIFF] * DIFF_HALF_DIM ** -0.5
    kd, vd = w[:, 3 * W_NA + W_DIFF:3 * W_NA + 2 * W_DIFF], w[:, 3 * W_NA + 2 * W_DIFF:]
    return {
        "mix_norm_w": mix_norm_w[0].reshape(1, D_MODEL),
        "w_tok": jnp.concatenate([qa, ka_va, kd], axis=1).astype(BF16),
        "w_qd_t": qd.T.astype(BF16),
        "w_vd_t": vd.T.astype(BF16),
        "na_bias": _na_bias_table(na_rpb[0]),
        "na_norm_w2": jnp.tile(na_norm_w[0], 2).reshape(1, LANES),
        "diff_bias": _diff_bias_table(t5_table),
        "lam_vecs": jnp.stack([lambda_q1[0], lambda_k1[0], lambda_q2[0], lambda_k2[0]]).astype(F32),
        "subln_col": subln_w[0].reshape(DIFF_V_DIM, 1),
        "w_out": w_out[0].astype(BF16),
        "ffn_norm_w": ffn_norm_w[0].reshape(1, D_MODEL),
        "w_router_t": w_router[0].T,
        "w_gate": w_gate[0].astype(BF16),
        "w_up": w_up[0].astype(BF16),
        "w_down": w_down[0].astype(BF16),
        "final_norm_w": final_norm_w.reshape(1, D_MODEL),
    }


def kernel(x_prompt, x_sample, mix_norm_w, w_in, na_rpb, na_norm_w, t5_table, lambda_q1, lambda_k1,
           lambda_q2, lambda_k2, subln_w, w_out, ffn_norm_w, w_router, w_gate, w_up, w_down, final_norm_w):
    p = _prepare_params(mix_norm_w, w_in, na_rpb, na_norm_w, t5_table, lambda_q1, lambda_k1, lambda_q2,
                        lambda_k2, subln_w, w_out, ffn_norm_w, w_router, w_gate, w_up, w_down, final_norm_w)
    return (_trunk(x_prompt, p), _trunk(x_sample, p))
```

```python
import functools
import math

import jax
import jax.numpy as jnp
from jax import lax
from jax.experimental import pallas as pl
from jax.experimental.pallas import tpu as pltpu

F32 = jnp.float32
BF16 = jnp.bfloat16

D_MODEL = 1024
GRID_W = 64
NA_HEADS = 8
NA_HEAD_DIM = 64
NA_KH = 8
NA_KW = 16
DIFF_HEADS = 4
DIFF_HALF_DIM = 64
DIFF_V_DIM = 128
W_NA = 512
W_DIFF = 512
D_IN = 3072
T5_BUCKETS = 32
T5_MAX_DIST = 128
N_EXPERTS = 16
CAPACITY_FACTOR = 2
D_EXPERT = 2816
EPS = 1e-6
LAMBDA_INIT = 0.8 - 0.6 * math.exp(-0.3 * 0)
LOG2E = math.log2(math.e)
TOKEN_MAJOR_COLS = 3 * W_NA + W_DIFF

LANES = 128
TOKEN_TILE = 512
NA_ROWS_PER_STEP = 8
DIFF_BLOCK = 512
DIFF_ONES_ROWS = 16
MOE_TILE = 512
ROUTE_CHUNK = 256
ROUTE_CHUNKS_PER_STEP = 8
COMBINE_TILE = 256
COMBINE_DMA_ROWS = 32
FFN_CHUNKS = ((0, 512), (512, 512), (1024, 512), (1536, 512), (2048, 512), (2560, 256))
MASKED = -1e30
VMEM_LIMIT = 56 * 1024 * 1024

_NT = (((1,), (1,)), ((), ()))


def _params(sem, vmem=VMEM_LIMIT):
    return pltpu.CompilerParams(dimension_semantics=sem, vmem_limit_bytes=vmem)


def _inproj_body(x_ref, nw_ref, w_ref, wqt_ref, wvt_ref, o_ref, qt_ref, vt_ref):
    x = x_ref[...]
    ms = jnp.mean(x * x, axis=-1, keepdims=True)
    h = ((x * lax.rsqrt(ms + EPS)) * nw_ref[...]).astype(BF16)
    for c in range(TOKEN_MAJOR_COLS // W_NA):
        cols = slice(c * W_NA, (c + 1) * W_NA)
        o_ref[:, cols] = jnp.dot(h, w_ref[:, cols], preferred_element_type=F32).astype(BF16)
    qt = lax.dot_general(wqt_ref[...], h, _NT, preferred_element_type=F32)
    qt_ref[0] = (qt * LOG2E).astype(BF16)
    vt_ref[0] = lax.dot_general(wvt_ref[...], h, _NT, preferred_element_type=F32).astype(BF16)


def _inproj(x2d, norm_w, w_tok, w_qd_t, w_vd_t):
    n = x2d.shape[0]
    tm = TOKEN_TILE
    fixed = lambda i: (0, 0)
    return pl.pallas_call(
        _inproj_body,
        grid=(n // tm,),
        in_specs=[
            pl.BlockSpec((tm, D_MODEL), lambda i: (i, 0)),
            pl.BlockSpec((1, D_MODEL), fixed),
            pl.BlockSpec((D_MODEL, TOKEN_MAJOR_COLS), fixed),
            pl.BlockSpec((W_DIFF, D_MODEL), fixed),
            pl.BlockSpec((W_DIFF, D_MODEL), fixed),
        ],
        out_specs=[
            pl.BlockSpec((tm, TOKEN_MAJOR_COLS), lambda i: (i, 0)),
            pl.BlockSpec((1, W_DIFF, tm), lambda i: (i, 0, 0)),
            pl.BlockSpec((1, W_DIFF, tm), lambda i: (i, 0, 0)),
        ],
        out_shape=[
            jax.ShapeDtypeStruct((n, TOKEN_MAJOR_COLS), BF16),
            jax.ShapeDtypeStruct((n // tm, W_DIFF, tm), BF16),
            jax.ShapeDtypeStruct((n // tm, W_DIFF, tm), BF16),
        ],
        compiler_params=_params(("parallel",)),
        name="inproj",
    )(x2d, norm_w, w_tok, w_qd_t, w_vd_t)


def _na_bias_table(rpb):
    c = jnp.arange(GRID_W)
    col_start = jnp.clip(c - NA_KW // 2, 0, GRID_W - NA_KW)
    in_win = (c[None, :] >= col_start[:, None]) & (c[None, :] < col_start[:, None] + NA_KW)
    dc = jnp.clip(c[None, :] - c[:, None], -(NA_KW - 1), NA_KW - 1) + (NA_KW - 1)
    bc = rpb.astype(F32)[:, :, dc]
    bc = jnp.where(in_win[None, None], bc, MASKED)
    tabs = jnp.stack([bc[:, d0:d0 + NA_KH] for d0 in range(NA_KH)])
    tabs = tabs.transpose(0, 1, 3, 2, 4)
    return tabs.reshape(NA_KH, NA_HEADS // 2, 2 * GRID_W, NA_KH * GRID_W)


def _na_body(q_ref, kp_ref, kc_ref, kn_ref, vp_ref, vc_ref, vn_ref, bias_ref, nw_ref, o_ref,
             kbuf, vbuf, *, rows):
    j = pl.program_id(1)
    blk = NA_ROWS_PER_STEP * GRID_W
    kbuf[0:blk] = kp_ref[...]
    kbuf[blk:2 * blk] = kc_ref[...]
    kbuf[2 * blk:3 * blk] = kn_ref[...]
    vbuf[0:blk] = vp_ref[...]
    vbuf[blk:2 * blk] = vc_ref[...]
    vbuf[2 * blk:3 * blk] = vn_ref[...]
    lo = lax.broadcasted_iota(jnp.int32, (GRID_W, LANES), 1) < NA_HEAD_DIM
    nw = nw_ref[...]
    nkeys = NA_KH * GRID_W

    def softmax_parts(s):
        m = jnp.max(s, axis=-1, keepdims=True)
        p = jnp.exp(s - m)
        return p.astype(BF16), jnp.sum(p, axis=-1, keepdims=True)

    def one_row(a, carry):
        r = j * NA_ROWS_PER_STEP + a
        row_start = jnp.clip(r - NA_KH // 2, 0, rows - NA_KH)
        d0 = row_start - r + (NA_KH - 1)
        koff = pl.multiple_of((row_start - (j - 1) * NA_ROWS_PER_STEP) * GRID_W, GRID_W)
        qoff = pl.multiple_of(a * GRID_W, GRID_W)
        pairs = range(NA_HEADS // 2)
        cols = [slice(hp * LANES, (hp + 1) * LANES) for hp in pairs]
        scores = []
        for hp in pairs:
            qp = q_ref[pl.ds(qoff, GRID_W), cols[hp]]
            zero = jnp.zeros_like(qp)
            q2 = jnp.concatenate([jnp.where(lo, qp, zero), jnp.where(lo, zero, qp)], axis=0)
            kk = kbuf[pl.ds(koff, nkeys), cols[hp]]
            scores.append(lax.dot_general(q2, kk, _NT, preferred_element_type=F32) + bias_ref[d0, hp])
        probs = [softmax_parts(s) for s in scores]
        outs = [jnp.dot(p, vbuf[pl.ds(koff, nkeys), cols[hp]], preferred_element_type=F32) / l
                for hp, (p, l) in zip(pairs, probs)]
        for hp in pairs:
            cs = cols[hp]
            o = jnp.where(lo, outs[hp][0:GRID_W], outs[hp][GRID_W:2 * GRID_W])
            sq = o * o
            msa = jnp.sum(jnp.where(lo, sq, 0.0), axis=-1, keepdims=True) * (1.0 / NA_HEAD_DIM)
            msb = jnp.sum(jnp.where(lo, 0.0, sq), axis=-1, keepdims=True) * (1.0 / NA_HEAD_DIM)
            inv = jnp.where(lo, lax.rsqrt(msa + EPS), lax.rsqrt(msb + EPS))
            o_ref[pl.ds(qoff, GRID_W), cs] = ((o * inv) * nw).astype(BF16)
        return carry

    lax.fori_loop(0, NA_ROWS_PER_STEP, one_row, 0, unroll=2)


def _na_attention(proj, bias_tab, norm_w2, batch, seq):
    rows = seq // GRID_W
    assert rows % NA_ROWS_PER_STEP == 0 and rows >= NA_KH
    nb = rows // NA_ROWS_PER_STEP
    blk = NA_ROWS_PER_STEP * GRID_W
    n = batch * seq

    def at(col, shift):
        return pl.BlockSpec(
            (blk, W_NA), lambda b, j: (b * nb + jnp.clip(j + shift, 0, nb - 1), col))

    return pl.pallas_call(
        functools.partial(_na_body, rows=rows),
        grid=(batch, nb),
        in_specs=[
            at(0, 0),
            at(1, -1), at(1, 0), at(1, 1),
            at(2, -1), at(2, 0), at(2, 1),
            pl.BlockSpec((NA_KH, NA_HEADS // 2, 2 * GRID_W, NA_KH * GRID_W), lambda b, j: (0, 0, 0, 0)),
            pl.BlockSpec((1, LANES), lambda b, j: (0, 0)),
        ],
        out_specs=pl.BlockSpec((blk, W_NA), lambda b, j: (b * nb + j, 0)),
        out_shape=jax.ShapeDtypeStruct((n, W_NA), BF16),
        scratch_shapes=[pltpu.VMEM((3 * blk, W_NA), BF16), pltpu.VMEM((3 * blk, W_NA), BF16)],
        compiler_params=_params(("parallel", "parallel")),
        name="na_attention",
    )(proj, proj, proj, proj, proj, proj, proj, bias_tab, norm_w2)


def _t5_bucket(rel):
    nb = T5_BUCKETS // 2
    max_exact = nb // 2
    n = jnp.abs(rel)
    large = max_exact + (jnp.log(jnp.maximum(n, 1).astype(F32) / max_exact)
                         / math.log(T5_MAX_DIST / max_exact) * (nb - max_exact)).astype(jnp.int32)
    large = jnp.minimum(large, nb - 1)
    return jnp.where(rel > 0, nb, 0) + jnp.where(n < max_exact, n, large)


def _diff_bias_table(t5_table):
    x = DIFF_BLOCK
    assert x + 1 >= T5_MAX_DIST
    span = 2 * x
    rel = (x - 1) - jnp.arange(span)[None, :] + jnp.arange(-2, 3)[:, None] * x
    diag = (t5_table.astype(F32) * LOG2E)[_t5_bucket(rel)].transpose(2, 0, 1)
    skew = jnp.tile(diag, (1, 1, x))[:, :, :x * (span - 1)].reshape(DIFF_HEADS, 5, x, span - 1)
    return skew[:, :, :, x - 1:2 * x - 1]


def _diff_body(qt_ref, k_ref, vt_ref, bias_ref, lam_ref, sw_ref, o_ref, s_even, s_odd, *, nblk):
    i = pl.program_id(2)
    x = DIFF_BLOCK
    qt = qt_ref[0]
    first = lax.broadcasted_iota(jnp.int32, (LANES, x), 0) < DIFF_HALF_DIM
    zero = jnp.zeros_like(qt)
    qt_both = jnp.concatenate([jnp.where(first, qt, zero), jnp.where(first, zero, qt)], axis=1)

    def scores(j, s_ref):
        kc = k_ref[pl.ds(pl.multiple_of(j * x, x), x), :]
        s_ref[...] = jnp.dot(kc, qt_both, preferred_element_type=F32)

    ones_rows = jnp.ones((DIFF_ONES_ROWS, x), BF16)

    def consume(j, s_ref, carry):
        vt_ext = jnp.concatenate([vt_ref[j], ones_rows], axis=0)
        bt = bias_ref[0, jnp.clip(j - i, -2, 2) + 2]
        out = []
        for c in range(2):
            m, acc = carry[2 * c:2 * c + 2]
            s = s_ref[:, c * x:(c + 1) * x] + bt
            mn = jnp.maximum(m, jnp.max(s, axis=0, keepdims=True))
            p = jnp.exp2(s - mn).astype(BF16)
            acc = jnp.exp2(m - mn) * acc + jnp.dot(vt_ext, p, preferred_element_type=F32)
            out += [mn, acc]
        return tuple(out)

    def body(jj, carry):
        j = 2 * jj
        scores(j + 1, s_odd)
        carry = consume(j, s_even, carry)
        scores(j + 2, s_even)
        return consume(j + 1, s_odd, carry)

    init = (jnp.full((1, x), -jnp.inf, F32), jnp.zeros((DIFF_V_DIM + DIFF_ONES_ROWS, x), F32)) * 2
    scores(0, s_even)
    carry = lax.fori_loop(0, nblk // 2 - 1, body, init)
    scores(nblk - 1, s_odd)
    carry = consume(nblk - 2, s_even, carry)
    _, ext1, _, ext2 = consume(nblk - 1, s_odd, carry)
    acc1, l1 = ext1[0:DIFF_V_DIM], ext1[DIFF_V_DIM:DIFF_V_DIM + 1]
    acc2, l2 = ext2[0:DIFF_V_DIM], ext2[DIFF_V_DIM:DIFF_V_DIM + 1]
    lam = (jnp.exp(jnp.sum(lam_ref[0:1, :] * lam_ref[1:2, :], axis=-1, keepdims=True))
           - jnp.exp(jnp.sum(lam_ref[2:3, :] * lam_ref[3:4, :], axis=-1, keepdims=True)) + LAMBDA_INIT)
    o = acc1 / l1 - lam * (acc2 / l2)
    ms = jnp.mean(o * o, axis=0, keepdims=True)
    y = ((o * lax.rsqrt(ms + EPS)) * sw_ref[...]) * (1.0 - LAMBDA_INIT)
    o_ref[0] = y.astype(BF16)


def _diff_attention(proj, qd_t, vd_t, bias_tab, lam_vecs, subln_col, batch, seq):
    x = DIFF_BLOCK
    assert seq % (2 * x) == 0 and x == TOKEN_TILE
    nblk = seq // x
    n = batch * seq
    kcol = 3 * W_NA // LANES
    return pl.pallas_call(
        functools.partial(_diff_body, nblk=nblk),
        grid=(batch, DIFF_HEADS, nblk),
        in_specs=[
            pl.BlockSpec((1, LANES, x), lambda b, h, i: (b * nblk + i, h, 0)),
            pl.BlockSpec((seq, LANES), lambda b, h, i: (b, kcol + h)),
            pl.BlockSpec((nblk, LANES, x), lambda b, h, i: (b, h, 0)),
            pl.BlockSpec((1, 5, x, x), lambda b, h, i: (h, 0, 0, 0)),
            pl.BlockSpec((4, DIFF_HALF_DIM), lambda b, h, i: (0, 0)),
            pl.BlockSpec((DIFF_V_DIM, 1), lambda b, h, i: (0, 0)),
        ],
        out_specs=pl.BlockSpec((1, LANES, x), lambda b, h, i: (b * nblk + i, h, 0)),
        out_shape=jax.ShapeDtypeStruct((n // x, W_DIFF, x), BF16),
        scratch_shapes=[pltpu.VMEM((x, 2 * x), F32), pltpu.VMEM((x, 2 * x), F32)],
        compiler_params=_params(("parallel", "parallel", "parallel")),
        name="diff_attention",
    )(qd_t, proj, vd_t, bias_tab, lam_vecs, subln_col)


def _outproj_body(x_ref, ya_ref, yb_ref, wa_ref, wb_ref, nw_ref, wr_ref, x2_ref, h2_ref, aff_ref):
    attn = (jnp.dot(ya_ref[...], wa_ref[...], preferred_element_type=F32)
            + lax.dot_general(yb_ref[0], wb_ref[...], (((0,), (0,)), ((), ())), preferred_element_type=F32))
    x2 = x_ref[...] + attn
    x2_ref[...] = x2
    ms = jnp.mean(x2 * x2, axis=-1, keepdims=True)
    h2 = (x2 * lax.rsqrt(ms + EPS)) * nw_ref[...]
    h2_ref[...] = h2
    logits = lax.dot_general(wr_ref[...], h2, _NT, precision=lax.Precision.HIGHEST,
                             preferred_element_type=F32)
    e = jnp.exp(logits - jnp.max(logits, axis=0, keepdims=True))
    aff_ref[...] = e / jnp.sum(e, axis=0, keepdims=True)


def _outproj(x2d, ya, yb, w_out, norm_w, w_router_t):
    n = x2d.shape[0]
    tm = TOKEN_TILE
    row = lambda i: (i, 0)
    fixed = lambda i: (0, 0)
    return pl.pallas_call(
        _outproj_body,
        grid=(n // tm,),
        in_specs=[
            pl.BlockSpec((tm, D_MODEL), row),
            pl.BlockSpec((tm, W_NA), row),
            pl.BlockSpec((1, W_DIFF, tm), lambda i: (i, 0, 0)),
            pl.BlockSpec((W_NA, D_MODEL), lambda i: (0, 0)),
            pl.BlockSpec((W_DIFF, D_MODEL), lambda i: (1, 0)),
            pl.BlockSpec((1, D_MODEL), fixed),
            pl.BlockSpec((N_EXPERTS, D_MODEL), fixed),
        ],
        out_specs=[
            pl.BlockSpec((tm, D_MODEL), row),
            pl.BlockSpec((tm, D_MODEL), row),
            pl.BlockSpec((N_EXPERTS, tm), lambda i: (0, i)),
        ],
        out_shape=[
            jax.ShapeDtypeStruct((n, D_MODEL), F32),
            jax.ShapeDtypeStruct((n, D_MODEL), F32),
            jax.ShapeDtypeStruct((N_EXPERTS, n), F32),
        ],
        compiler_params=_params(("parallel",)),
        name="outproj_router",
    )(x2d, ya, yb, w_out, w_out, norm_w, w_router_t)


def _moe_body(idx_ref, idx_next_ref, h2_hbm, g_ref, wg_ref, wu_ref, wd_ref, o_ref, x_even, x_odd, sem, *, tm):
    nt = pl.num_programs(1)
    step = pl.program_id(0) * nt + pl.program_id(1)
    total = pl.num_programs(0) * nt

    def row_copy(tok, i, buf, sl):
        return pltpu.make_async_copy(h2_hbm.at[pl.ds(tok, 1), :], buf.at[pl.ds(i, 1), :], sem.at[sl])

    def wait_tile(buf, sl):
        pltpu.make_async_copy(h2_hbm.at[pl.ds(0, tm), :], buf, sem.at[sl]).wait()

    @pl.when(step == 0)
    def _():
        def body(i, carry):
            row_copy(idx_ref[0, 0, i], i, x_even, 0).start()
            return carry
        lax.fori_loop(0, tm, body, 0, unroll=8)

    def run(cur, cur_sl, nxt, nxt_sl):
        wait_tile(cur, cur_sl)
        for i in range(tm):
            row_copy(idx_next_ref[0, 0, i], i, nxt, nxt_sl).start()
        x = cur[...].astype(BF16)
        acc = jnp.zeros((tm, D_MODEL), F32)
        for f0, fc in FFN_CHUNKS:
            g = jnp.dot(x, wg_ref[0, :, f0:f0 + fc], preferred_element_type=F32)
            u = jnp.dot(x, wu_ref[0, :, f0:f0 + fc], preferred_element_type=F32)
            h = (jax.nn.silu(g) * u).astype(BF16)
            acc = acc + jnp.dot(h, wd_ref[0, f0:f0 + fc, :], preferred_element_type=F32)
        o_ref[...] = (acc * g_ref[...]).astype(o_ref.dtype)

        @pl.when(step == total - 1)
        def _():
            wait_tile(nxt, nxt_sl)

    @pl.when(step % 2 == 0)
    def _():
        run(x_even, 0, x_odd, 1)

    @pl.when(step % 2 == 1)
    def _():
        run(x_odd, 1, x_even, 0)


def _moe_experts(h2, idx, gates, w_gate, w_up, w_down):
    n_exp, cap = idx.shape
    tm = min(MOE_TILE, cap)
    assert cap % tm == 0
    nt = cap // tm
    idx3 = idx.reshape(n_exp * nt, 1, tm)

    def next_tile(e, t):
        return (jnp.minimum(e * nt + t + 1, n_exp * nt - 1), 0, 0)

    return pl.pallas_call(
        functools.partial(_moe_body, tm=tm),
        grid=(n_exp, nt),
        in_specs=[
            pl.BlockSpec((1, 1, tm), lambda e, t: (e * nt + t, 0, 0), memory_space=pltpu.SMEM),
            pl.BlockSpec((1, 1, tm), next_tile, memory_space=pltpu.SMEM),
            pl.BlockSpec(memory_space=pl.ANY),
            pl.BlockSpec((tm, 1), lambda e, t: (e * nt + t, 0)),
            pl.BlockSpec((1, D_MODEL, D_EXPERT), lambda e, t: (e, 0, 0)),
            pl.BlockSpec((1, D_MODEL, D_EXPERT), lambda e, t: (e, 0, 0)),
            pl.BlockSpec((1, D_EXPERT, D_MODEL), lambda e, t: (e, 0, 0)),
        ],
        out_specs=pl.BlockSpec((tm, D_MODEL), lambda e, t: (e * nt + t, 0)),
        out_shape=jax.ShapeDtypeStruct((n_exp * cap, D_MODEL), BF16),
        scratch_shapes=[pltpu.VMEM((tm, D_MODEL), F32), pltpu.VMEM((tm, D_MODEL), F32),
                        pltpu.SemaphoreType.DMA((2,))],
        compiler_params=_params(("arbitrary", "arbitrary")),
        name="moe_experts",
    )(idx3, idx3, h2, gates.reshape(n_exp * cap, 1), w_gate, w_up, w_down)


def _route_threshold_body(aff_ref, theta_ref, need_ref, *, cap):
    n_exp = aff_ref.shape[0]

    def count_ge(cand):
        bits = pltpu.bitcast(aff_ref[...], jnp.int32)
        return jnp.sum(jnp.where(bits >= cand, 1.0, 0.0), axis=1, keepdims=True)

    def step(b, theta):
        cand = theta | jnp.left_shift(jnp.int32(1), 30 - b)
        return jnp.where(count_ge(cand) >= cap, cand, theta)

    theta = lax.fori_loop(0, 31, step, jnp.zeros((n_exp, 1), jnp.int32))
    need = cap - count_ge(theta + 1)
    theta_ref[...] = jnp.broadcast_to(theta, theta_ref.shape)
    need_ref[...] = jnp.broadcast_to(need, need_ref.shape)


def _route_threshold(aff_t, cap):
    n_exp, n = aff_t.shape
    full = lambda i: (0, 0)
    return pl.pallas_call(
        functools.partial(_route_threshold_body, cap=cap),
        grid=(1,),
        in_specs=[pl.BlockSpec((n_exp, n), full)],
        out_specs=[pl.BlockSpec((n_exp, LANES), full), pl.BlockSpec((n_exp, LANES), full)],
        out_shape=[jax.ShapeDtypeStruct((n_exp, LANES), jnp.int32), jax.ShapeDtypeStruct((n_exp, LANES), F32)],
        compiler_params=_params(("arbitrary",)),
        name="route_threshold",
    )(aff_t)


def _route_prefix_body(aff_ref, tri_ref, theta_ref, need_ref, slot_ref, base_ref, run_sel, run_tie, *, cap):
    @pl.when(pl.program_id(0) == 0)
    def _():
        run_sel[...] = jnp.zeros_like(run_sel)
        run_tie[...] = jnp.zeros_like(run_tie)

    n_exp = aff_ref.shape[0]
    w = ROUTE_CHUNK
    theta = theta_ref[:, 0:1]
    need = need_ref[:, 0:1]
    expert_base = lax.broadcasted_iota(jnp.int32, (n_exp, 1), 0) * cap
    tri = tri_ref[...]
    for k in range(ROUTE_CHUNKS_PER_STEP):
        bits = pltpu.bitcast(aff_ref[:, k * w:(k + 1) * w], jnp.int32)
        tie = bits == theta
        tie_f = jnp.where(tie, 1.0, 0.0)
        tie_incl = jnp.dot(tie_f.astype(BF16), tri, preferred_element_type=F32)
        tie_rank = run_tie[:, 0:1] + tie_incl - tie_f
        sel = (bits > theta) | (tie & (tie_rank < need))
        sel_f = jnp.where(sel, 1.0, 0.0)
        incl = jnp.dot(sel_f.astype(BF16), tri, preferred_element_type=F32)
        rank = run_sel[:, 0:1] + incl - sel_f
        slot_ref[:, k * w:(k + 1) * w] = jnp.where(sel, rank.astype(jnp.int32) + expert_base, -1)
        base_ref[k] = run_sel[...].astype(jnp.int32)
        run_sel[...] = run_sel[...] + jnp.sum(sel_f, axis=1, keepdims=True)
        run_tie[...] = run_tie[...] + jnp.sum(tie_f, axis=1, keepdims=True)


def _route_prefix(aff_t, theta, need, cap):
    n_exp, n = aff_t.shape
    w = ROUTE_CHUNK
    span = w * ROUTE_CHUNKS_PER_STEP
    assert n % span == 0
    tri = (jnp.arange(w)[:, None] <= jnp.arange(w)[None, :]).astype(BF16)
    fixed = lambda i: (0, 0)
    return pl.pallas_call(
        functools.partial(_route_prefix_body, cap=cap),
        grid=(n // span,),
        in_specs=[
            pl.BlockSpec((n_exp, span), lambda i: (0, i)),
            pl.BlockSpec((w, w), fixed),
            pl.BlockSpec((n_exp, LANES), fixed),
            pl.BlockSpec((n_exp, LANES), fixed),
        ],
        out_specs=[
            pl.BlockSpec((n_exp, span), lambda i: (0, i)),
            pl.BlockSpec((ROUTE_CHUNKS_PER_STEP, n_exp, LANES), lambda i: (i, 0, 0)),
        ],
        out_shape=[
            jax.ShapeDtypeStruct((n_exp, n), jnp.int32),
            jax.ShapeDtypeStruct((n // w, n_exp, LANES), jnp.int32),
        ],
        scratch_shapes=[pltpu.VMEM((n_exp, LANES), F32), pltpu.VMEM((n_exp, LANES), F32)],
        compiler_params=_params(("arbitrary",)),
        name="route_prefix",
    )(aff_t, tri, theta, need)


def _route_compact_body(cb_ref, slot_ref, aff_ref, idx_ref, gate_ref, *, cap, nchunks):
    e = pl.program_id(0)
    w = ROUTE_CHUNK
    s = LANES
    slot_iota = lax.broadcasted_iota(jnp.int32, (s, w), 0)
    tok_iota = lax.broadcasted_iota(jnp.int32, (s, w), 1)
    eye = lax.broadcasted_iota(jnp.int32, (s, s), 0) == lax.broadcasted_iota(jnp.int32, (s, s), 1)

    def fold(v):
        out = v[:, 0:s]
        for q in range(1, w // s):
            out = out + v[:, q * s:(q + 1) * s]
        return out

    def to_row(acc):
        col = jnp.sum(acc, axis=1, keepdims=True)
        return jnp.sum(jnp.where(eye, col, 0.0), axis=0, keepdims=True)

    def one_tile(jt, c_first):
        j0 = jt * s
        c_first = lax.while_loop(lambda c: cb_ref[e, c + 1] <= j0, lambda c: c + 1, c_first)
        want = slot_iota + (e * cap + j0)

        def more(state):
            c = state[0]
            return jnp.logical_and(c < nchunks, cb_ref[e, jnp.minimum(c, nchunks - 1)] < j0 + s)

        def chunk(state):
            c, acc_i, acc_g = state
            hit = want == slot_ref[0, pl.ds(c, 1), :]
            tok = (tok_iota + c * w).astype(F32)
            acc_i = acc_i + fold(jnp.where(hit, tok, 0.0))
            acc_g = acc_g + fold(jnp.where(hit, aff_ref[0, pl.ds(c, 1), :], 0.0))
            return c + 1, acc_i, acc_g

        zero = jnp.zeros((s, s), F32)
        _, acc_i, acc_g = lax.while_loop(more, chunk, (c_first, zero, zero))
        idx_ref[0, pl.ds(jt, 1), :] = to_row(acc_i).astype(jnp.int32)
        gate_ref[0, pl.ds(jt, 1), :] = to_row(acc_g)
        return c_first

    lax.fori_loop(0, cap // s, one_tile, jnp.int32(0))


def _route_compact(chunk_base, slot3, aff3, cap):
    n_exp, nchunks, w = slot3.shape
    assert cap % LANES == 0
    per_e = lambda e, cb: (e, 0, 0)
    grid_spec = pltpu.PrefetchScalarGridSpec(
        num_scalar_prefetch=1,
        grid=(n_exp,),
        in_specs=[pl.BlockSpec((1, nchunks, w), per_e), pl.BlockSpec((1, nchunks, w), per_e)],
        out_specs=[pl.BlockSpec((1, cap // LANES, LANES), per_e), pl.BlockSpec((1, cap // LANES, LANES), per_e)],
    )
    return pl.pallas_call(
        functools.partial(_route_compact_body, cap=cap, nchunks=nchunks),
        grid_spec=grid_spec,
        out_shape=[
            jax.ShapeDtypeStruct((n_exp, cap // LANES, LANES), jnp.int32),
            jax.ShapeDtypeStruct((n_exp, cap // LANES, LANES), F32),
        ],
        compiler_params=_params(("arbitrary",)),
        name="route_compact",
    )(chunk_base, slot3, aff3)


def _combine_body(cb_ref, slot_ref, x2_ref, ye_hbm, nw_ref, o_ref, stage, chunk_row, chunk_count, sem,
                  *, cap, ntiles):
    tile = pl.program_id(0)
    cur = tile % 2
    tt = COMBINE_TILE
    ch = COMBINE_DMA_ROWS
    n_exp = slot_ref.shape[0]
    kblock = 256 // ch

    def chunk_copy(row, k, sl):
        return pltpu.make_async_copy(ye_hbm.at[pl.ds(row, ch), :], stage.at[sl, pl.ds(k * ch, ch), :], sem.at[sl])

    def issue(t, sl):
        def per_expert(e, k):
            lo = cb_ref[e, t]
            hi = cb_ref[e, t + 1]
            first = lo // ch
            nch = jnp.where(hi > lo, (hi - 1) // ch - first + 1, 0)

            def per_chunk(i, k):
                row = pl.multiple_of(e * cap + (first + i) * ch, ch)
                chunk_copy(row, k, sl).start()
                chunk_row[sl, k] = row
                return k + 1

            return lax.fori_loop(0, nch, per_chunk, k)

        chunk_count[sl] = lax.fori_loop(0, n_exp, per_expert, jnp.int32(0))

    @pl.when(tile == 0)
    def _():
        issue(0, 0)

    @pl.when(tile + 1 < ntiles)
    def _():
        issue(tile + 1, 1 - cur)

    nchunk = chunk_count[cur]

    def wait_one(k, carry):
        chunk_copy(0, k, cur).wait()
        return carry

    lax.fori_loop(0, nchunk, wait_one, 0)
    nblock = (nchunk + kblock - 1) // kblock

    def pad_one(k, carry):
        stage[cur, pl.ds(pl.multiple_of(k * ch, ch), ch), :] = jnp.zeros((ch, D_MODEL), BF16)
        chunk_row[cur, k] = -(1 << 30)
        return carry

    lax.fori_loop(nchunk, nblock * kblock, pad_one, 0)
    row_iota = lax.broadcasted_iota(jnp.int32, (ch, tt), 0)

    def one_block(b, acc):
        parts = []
        for i in range(kblock):
            row = chunk_row[cur, b * kblock + i]
            e = jnp.clip(row // cap, 0, n_exp - 1)
            hit = (row_iota + row) == slot_ref[pl.ds(e, 1), :]
            parts.append(jnp.where(hit, 1.0, 0.0).astype(BF16))
        onehot_t = jnp.concatenate(parts, axis=0)
        rows = stage[cur, pl.ds(pl.multiple_of(b * 256, 256), 256), :]
        return acc + lax.dot_general(onehot_t, rows, (((0,), (0,)), ((), ())), preferred_element_type=F32)

    moe = lax.fori_loop(0, nblock, one_block, jnp.zeros((tt, D_MODEL), F32))
    x3 = x2_ref[...] + moe
    ms = jnp.mean(x3 * x3, axis=-1, keepdims=True)
    o_ref[...] = (x3 * lax.rsqrt(ms + EPS)) * nw_ref[...]


def _combine(tile_base, slot, x2, ye, norm_w, cap):
    n_exp, n = slot.shape
    tt = COMBINE_TILE
    ch = COMBINE_DMA_ROWS
    ntiles = n // tt
    assert cap % ch == 0 and tt % ch == 0
    max_rows = n_exp * (tt + ch)
    max_rows = -(-max_rows // 256) * 256
    grid_spec = pltpu.PrefetchScalarGridSpec(
        num_scalar_prefetch=1,
        grid=(ntiles,),
        in_specs=[
            pl.BlockSpec((n_exp, tt), lambda t, cb: (0, t)),
            pl.BlockSpec((tt, D_MODEL), lambda t, cb: (t, 0)),
            pl.BlockSpec(memory_space=pl.ANY),
            pl.BlockSpec((1, D_MODEL), lambda t, cb: (0, 0)),
        ],
        out_specs=pl.BlockSpec((tt, D_MODEL), lambda t, cb: (t, 0)),
        scratch_shapes=[
            pltpu.VMEM((2, max_rows, D_MODEL), BF16),
            pltpu.SMEM((2, max_rows // ch), jnp.int32),
            pltpu.SMEM((2,), jnp.int32),
            pltpu.SemaphoreType.DMA((2,)),
        ],
    )
    return pl.pallas_call(
        functools.partial(_combine_body, cap=cap, ntiles=ntiles),
        grid_spec=grid_spec,
        out_shape=jax.ShapeDtypeStruct((n, D_MODEL), F32),
        compiler_params=_params(("arbitrary",)),
        name="combine_final",
    )(tile_base, slot, x2, ye, norm_w)


def _trunk(x, p):
    batch, seq = x.shape[0], x.shape[1]
    n = batch * seq
    x2d = x.reshape(n, D_MODEL)
    proj, qd_t, vd_t = _inproj(x2d, p["mix_norm_w"], p["w_tok"], p["w_qd_t"], p["w_vd_t"])
    ya = _na_attention(proj, p["na_bias"], p["na_norm_w2"], batch, seq)
    yb_t = _diff_attention(proj, qd_t, vd_t, p["diff_bias"], p["lam_vecs"], p["subln_col"], batch, seq)
    x2, h2, aff_t = _outproj(x2d, ya, yb_t, p["w_out"], p["ffn_norm_w"], p["w_router_t"])
    cap = CAPACITY_FACTOR * n // N_EXPERTS
    theta, need = _route_threshold(aff_t, cap)
    slot, base = _route_prefix(aff_t, theta, need, cap)
    nchunks = n // ROUTE_CHUNK
    chunk_base = jnp.concatenate([base[:, :, 0].T, jnp.full((N_EXPERTS, 1), cap, jnp.int32)], axis=1)
    idx, gates = _route_compact(chunk_base, slot.reshape(N_EXPERTS, nchunks, ROUTE_CHUNK),
                                aff_t.reshape(N_EXPERTS, nchunks, ROUTE_CHUNK), cap)
    ye = _moe_experts(h2, idx.reshape(N_EXPERTS, cap), gates.reshape(N_EXPERTS, cap),
                      p["w_gate"], p["w_up"], p["w_down"])
    tile_base = chunk_base[:, ::COMBINE_TILE // ROUTE_CHUNK]
    y = _combine(tile_base, slot, x2, ye, p["final_norm_w"], cap)
    return y.reshape(batch, seq, D_MODEL)


def _prepare_params(mix_norm_w, w_in, na_rpb, na_norm_w, t5_table, lambda_q1, lambda_k1, lambda_q2, lambda_k2,
                    subln_w, w_out, ffn_norm_w, w_router, w_gate, w_up, w_down, final_norm_w):
    w = w_in[0]
    qa, ka_va = w[:, 0:W_NA] * NA_HEAD_DIM ** -0.5, w[:, W_NA:3 * W_NA]
    qd = w[:, 3 * W_NA:3 * W_NA + W_DIFF] * DIFF_HALF_DIM ** -0.5
    kd, vd = w[:, 3 * W_NA + W_DIFF:3 * W_NA + 2 * W_DIFF], w[:, 3 * W_NA + 2 * W_DIFF:]
    return {
        "mix_norm_w": mix_norm_w[0].reshape(1, D_MODEL),
        "w_tok": jnp.concatenate([qa, ka_va, kd], axis=1).astype(BF16),
        "w_qd_t": qd.T.astype(BF16),
        "w_vd_t": vd.T.astype(BF16),
        "na_bias": _na_bias_table(na_rpb[0]),
        "na_norm_w2": jnp.tile(na_norm_w[0], 2).reshape(1, LANES),
        "diff_bias": _diff_bias_table(t5_table),
        "lam_vecs": jnp.stack([lambda_q1[0], lambda_k1[0], lambda_q2[0], lambda_k2[0]]).astype(F32),
        "subln_col": subln_w[0].reshape(DIFF_V_DIM, 1),
        "w_out": w_out[0].astype(BF16),
        "ffn_norm_w": ffn_norm_w[0].reshape(1, D_MODEL),
        "w_router_t": w_router[0].T,
        "w_gate": w_gate[0].astype(BF16),
        "w_up": w_up[0].astype(BF16),
        "w_down": w_down[0].astype(BF16),
        "final_norm_w": final_norm_w.reshape(1, D_MODEL),
    }


def kernel(x_prompt, x_sample, mix_norm_w, w_in, na_rpb, na_norm_w, t5_table, lambda_q1, lambda_k1,
           lambda_q2, lambda_k2, subln_w, w_out, ffn_norm_w, w_router, w_gate, w_up, w_down, final_norm_w):
    p = _prepare_params(mix_norm_w, w_in, na_rpb, na_norm_w, t5_table, lambda_q1, lambda_k1, lambda_q2,
                        lambda_k2, subln_w, w_out, ffn_norm_w, w_router, w_gate, w_up, w_down, final_norm_w)
    return (_trunk(x_prompt, p), _trunk(x_sample, p))
```

```python
import functools
import math

import jax
import jax.numpy as jnp
from jax import lax
from jax.experimental import pallas as pl
from jax.experimental.pallas import tpu as pltpu

F32 = jnp.float32
BF16 = jnp.bfloat16

D_MODEL = 1024
GRID_W = 64
NA_HEADS = 8
NA_HEAD_DIM = 64
NA_KH = 8
NA_KW = 16
DIFF_HEADS = 4
DIFF_HALF_DIM = 64
DIFF_V_DIM = 128
W_NA = 512
W_DIFF = 512
D_IN = 3072
T5_BUCKETS = 32
T5_MAX_DIST = 128
N_EXPERTS = 16
CAPACITY_FACTOR = 2
D_EXPERT = 2816
EPS = 1e-6
LAMBDA_INIT = 0.8 - 0.6 * math.exp(-0.3 * 0)
LOG2E = math.log2(math.e)
TOKEN_MAJOR_COLS = 3 * W_NA + W_DIFF

LANES = 128
TOKEN_TILE = 512
NA_ROWS_PER_STEP = 8
DIFF_BLOCK = 512
DIFF_ONES_ROWS = 16
MOE_TILE = 512
ROUTE_CHUNK = 256
ROUTE_CHUNKS_PER_STEP = 8
COMBINE_TILE = 256
COMBINE_DMA_ROWS = 32
FFN_CHUNKS = ((0, 512), (512, 512), (1024, 512), (1536, 512), (2048, 512), (2560, 256))
MASKED = -1e30
VMEM_LIMIT = 56 * 1024 * 1024

_NT = (((1,), (1,)), ((), ()))


def _params(sem, vmem=VMEM_LIMIT):
    return pltpu.CompilerParams(dimension_semantics=sem, vmem_limit_bytes=vmem)


def _inproj_body(x_ref, nw_ref, w_ref, wqt_ref, wvt_ref, o_ref, qt_ref, vt_ref):
    x = x_ref[...]
    ms = jnp.mean(x * x, axis=-1, keepdims=True)
    h = ((x * lax.rsqrt(ms + EPS)) * nw_ref[...]).astype(BF16)
    for c in range(TOKEN_MAJOR_COLS // W_NA):
        cols = slice(c * W_NA, (c + 1) * W_NA)
        o_ref[:, cols] = jnp.dot(h, w_ref[:, cols], preferred_element_type=F32).astype(BF16)
    qt = lax.dot_general(wqt_ref[...], h, _NT, preferred_element_type=F32)
    qt_ref[0] = (qt * LOG2E).astype(BF16)
    vt_ref[0] = lax.dot_general(wvt_ref[...], h, _NT, preferred_element_type=F32).astype(BF16)


def _inproj(x2d, norm_w, w_tok, w_qd_t, w_vd_t):
    n = x2d.shape[0]
    tm = TOKEN_TILE
    fixed = lambda i: (0, 0)
    return pl.pallas_call(
        _inproj_body,
        grid=(n // tm,),
        in_specs=[
            pl.BlockSpec((tm, D_MODEL), lambda i: (i, 0)),
            pl.BlockSpec((1, D_MODEL), fixed),
            pl.BlockSpec((D_MODEL, TOKEN_MAJOR_COLS), fixed),
            pl.BlockSpec((W_DIFF, D_MODEL), fixed),
            pl.BlockSpec((W_DIFF, D_MODEL), fixed),
        ],
        out_specs=[
            pl.BlockSpec((tm, TOKEN_MAJOR_COLS), lambda i: (i, 0)),
            pl.BlockSpec((1, W_DIFF, tm), lambda i: (i, 0, 0)),
            pl.BlockSpec((1, W_DIFF, tm), lambda i: (i, 0, 0)),
        ],
        out_shape=[
            jax.ShapeDtypeStruct((n, TOKEN_MAJOR_COLS), BF16),
            jax.ShapeDtypeStruct((n // tm, W_DIFF, tm), BF16),
            jax.ShapeDtypeStruct((n // tm, W_DIFF, tm), BF16),
        ],
        compiler_params=_params(("parallel",)),
        name="inproj",
    )(x2d, norm_w, w_tok, w_qd_t, w_vd_t)


def _na_bias_table(rpb):
    c = jnp.arange(GRID_W)
    col_start = jnp.clip(c - NA_KW // 2, 0, GRID_W - NA_KW)
    in_win = (c[None, :] >= col_start[:, None]) & (c[None, :] < col_start[:, None] + NA_KW)
    dc = jnp.clip(c[None, :] - c[:, None], -(NA_KW - 1), NA_KW - 1) + (NA_KW - 1)
    bc = rpb.astype(F32)[:, :, dc]
    bc = jnp.where(in_win[None, None], bc, MASKED)
    tabs = jnp.stack([bc[:, d0:d0 + NA_KH] for d0 in range(NA_KH)])
    tabs = tabs.transpose(0, 1, 3, 2, 4)
    return tabs.reshape(NA_KH, NA_HEADS // 2, 2 * GRID_W, NA_KH * GRID_W)


def _na_body(q_ref, kp_ref, kc_ref, kn_ref, vp_ref, vc_ref, vn_ref, bias_ref, nw_ref, o_ref,
             kbuf, vbuf, *, rows):
    j = pl.program_id(1)
    blk = NA_ROWS_PER_STEP * GRID_W
    kbuf[0:blk] = kp_ref[...]
    kbuf[blk:2 * blk] = kc_ref[...]
    kbuf[2 * blk:3 * blk] = kn_ref[...]
    vbuf[0:blk] = vp_ref[...]
    vbuf[blk:2 * blk] = vc_ref[...]
    vbuf[2 * blk:3 * blk] = vn_ref[...]
    lo = lax.broadcasted_iota(jnp.int32, (GRID_W, LANES), 1) < NA_HEAD_DIM
    nw = nw_ref[...]
    nkeys = NA_KH * GRID_W

    def softmax_parts(s):
        m = jnp.max(s, axis=-1, keepdims=True)
        p = jnp.exp(s - m)
        return p.astype(BF16), jnp.sum(p, axis=-1, keepdims=True)

    def one_row(a, carry):
        r = j * NA_ROWS_PER_STEP + a
        row_start = jnp.clip(r - NA_KH // 2, 0, rows - NA_KH)
        d0 = row_start - r + (NA_KH - 1)
        koff = pl.multiple_of((row_start - (j - 1) * NA_ROWS_PER_STEP) * GRID_W, GRID_W)
        qoff = pl.multiple_of(a * GRID_W, GRID_W)
        pairs = range(NA_HEADS // 2)
        cols = [slice(hp * LANES, (hp + 1) * LANES) for hp in pairs]
        scores = []
        for hp in pairs:
            qp = q_ref[pl.ds(qoff, GRID_W), cols[hp]]
            zero = jnp.zeros_like(qp)
            q2 = jnp.concatenate([jnp.where(lo, qp, zero), jnp.where(lo, zero, qp)], axis=0)
            kk = kbuf[pl.ds(koff, nkeys), cols[hp]]
            scores.append(lax.dot_general(q2, kk, _NT, preferred_element_type=F32) + bias_ref[d0, hp])
        probs = [softmax_parts(s) for s in scores]
        outs = [jnp.dot(p, vbuf[pl.ds(koff, nkeys), cols[hp]], preferred_element_type=F32) / l
                for hp, (p, l) in zip(pairs, probs)]
        for hp in pairs:
            cs = cols[hp]
            o = jnp.where(lo, outs[hp][0:GRID_W], outs[hp][GRID_W:2 * GRID_W])
            sq = o * o
            msa = jnp.sum(jnp.where(lo, sq, 0.0), axis=-1, keepdims=True) * (1.0 / NA_HEAD_DIM)
            msb = jnp.sum(jnp.where(lo, 0.0, sq), axis=-1, keepdims=True) * (1.0 / NA_HEAD_DIM)
            inv = jnp.where(lo, lax.rsqrt(msa + EPS), lax.rsqrt(msb + EPS))
            o_ref[pl.ds(qoff, GRID_W), cs] = ((o * inv) * nw).astype(BF16)
        return carry

    lax.fori_loop(0, NA_ROWS_PER_STEP, one_row, 0, unroll=2)


def _na_attention(proj, bias_tab, norm_w2, batch, seq):
    rows = seq // GRID_W
    assert rows % NA_ROWS_PER_STEP == 0 and rows >= NA_KH
    nb = rows // NA_ROWS_PER_STEP
    blk = NA_ROWS_PER_STEP * GRID_W
    n = batch * seq

    def at(col, shift):
        return pl.BlockSpec(
            (blk, W_NA), lambda b, j: (b * nb + jnp.clip(j + shift, 0, nb - 1), col))

    return pl.pallas_call(
        functools.partial(_na_body, rows=rows),
        grid=(batch, nb),
        in_specs=[
            at(0, 0),
            at(1, -1), at(1, 0), at(1, 1),
            at(2, -1), at(2, 0), at(2, 1),
            pl.BlockSpec((NA_KH, NA_HEADS // 2, 2 * GRID_W, NA_KH * GRID_W), lambda b, j: (0, 0, 0, 0)),
            pl.BlockSpec((1, LANES), lambda b, j: (0, 0)),
        ],
        out_specs=pl.BlockSpec((blk, W_NA), lambda b, j: (b * nb + j, 0)),
        out_shape=jax.ShapeDtypeStruct((n, W_NA), BF16),
        scratch_shapes=[pltpu.VMEM((3 * blk, W_NA), BF16), pltpu.VMEM((3 * blk, W_NA), BF16)],
        compiler_params=_params(("parallel", "parallel")),
        name="na_attention",
    )(proj, proj, proj, proj, proj, proj, proj, bias_tab, norm_w2)


def _t5_bucket(rel):
    nb = T5_BUCKETS // 2
    max_exact = nb // 2
    n = jnp.abs(rel)
    large = max_exact + (jnp.log(jnp.maximum(n, 1).astype(F32) / max_exact)
                         / math.log(T5_MAX_DIST / max_exact) * (nb - max_exact)).astype(jnp.int32)
    large = jnp.minimum(large, nb - 1)
    return jnp.where(rel > 0, nb, 0) + jnp.where(n < max_exact, n, large)


def _diff_bias_table(t5_table):
    x = DIFF_BLOCK
    assert x + 1 >= T5_MAX_DIST
    span = 2 * x
    rel = (x - 1) - jnp.arange(span)[None, :] + jnp.arange(-2, 3)[:, None] * x
    diag = (t5_table.astype(F32) * LOG2E)[_t5_bucket(rel)].transpose(2, 0, 1)
    skew = jnp.tile(diag, (1, 1, x))[:, :, :x * (span - 1)].reshape(DIFF_HEADS, 5, x, span - 1)
    return skew[:, :, :, x - 1:2 * x - 1]


def _diff_body(qt_ref, k_ref, vt_ref, bias_ref, lam_ref, sw_ref, o_ref, s_even, s_odd, *, nblk):
    i = pl.program_id(2)
    x = DIFF_BLOCK
    qt = qt_ref[0]
    first = lax.broadcasted_iota(jnp.int32, (LANES, x), 0) < DIFF_HALF_DIM
    zero = jnp.zeros_like(qt)
    qt_both = jnp.concatenate([jnp.where(first, qt, zero), jnp.where(first, zero, qt)], axis=1)

    def scores(j, s_ref):
        kc = k_ref[pl.ds(pl.multiple_of(j * x, x), x), :]
        s_ref[...] = jnp.dot(kc, qt_both, preferred_element_type=F32)

    ones_rows = jnp.ones((DIFF_ONES_ROWS, x), BF16)

    def consume(j, s_ref, carry):
        vt_ext = jnp.concatenate([vt_ref[j], ones_rows], axis=0)
        bt = bias_ref[0, jnp.clip(j - i, -2, 2) + 2]
        out = []
        for c in range(2):
            m, acc = carry[2 * c:2 * c + 2]
            s = s_ref[:, c * x:(c + 1) * x] + bt
            mn = jnp.maximum(m, jnp.max(s, axis=0, keepdims=True))
            p = jnp.exp2(s - mn).astype(BF16)
            acc = jnp.exp2(m - mn) * acc + jnp.dot(vt_ext, p, preferred_element_type=F32)
            out += [mn, acc]
        return tuple(out)

    def body(jj, carry):
        j = 2 * jj
        scores(j + 1, s_odd)
        carry = consume(j, s_even, carry)
        scores(j + 2, s_even)
        return consume(j + 1, s_odd, carry)

    init = (jnp.full((1, x), -jnp.inf, F32), jnp.zeros((DIFF_V_DIM + DIFF_ONES_ROWS, x), F32)) * 2
    scores(0, s_even)
    carry = lax.fori_loop(0, nblk // 2 - 1, body, init)
    scores(nblk - 1, s_odd)
    carry = consume(nblk - 2, s_even, carry)
    _, ext1, _, ext2 = consume(nblk - 1, s_odd, carry)
    acc1, l1 = ext1[0:DIFF_V_DIM], ext1[DIFF_V_DIM:DIFF_V_DIM + 1]
    acc2, l2 = ext2[0:DIFF_V_DIM], ext2[DIFF_V_DIM:DIFF_V_DIM + 1]
    lam = (jnp.exp(jnp.sum(lam_ref[0:1, :] * lam_ref[1:2, :], axis=-1, keepdims=True))
           - jnp.exp(jnp.sum(lam_ref[2:3, :] * lam_ref[3:4, :], axis=-1, keepdims=True)) + LAMBDA_INIT)
    o = acc1 / l1 - lam * (acc2 / l2)
    ms = jnp.mean(o * o, axis=0, keepdims=True)
    y = ((o * lax.rsqrt(ms + EPS)) * sw_ref[...]) * (1.0 - LAMBDA_INIT)
    o_ref[0] = y.astype(BF16)


def _diff_attention(proj, qd_t, vd_t, bias_tab, lam_vecs, subln_col, batch, seq):
    x = DIFF_BLOCK
    assert seq % (2 * x) == 0 and x == TOKEN_TILE
    nblk = seq // x
    n = batch * seq
    kcol = 3 * W_NA // LANES
    return pl.pallas_call(
        functools.partial(_diff_body, nblk=nblk),
        grid=(batch, DIFF_HEADS, nblk),
        in_specs=[
            pl.BlockSpec((1, LANES, x), lambda b, h, i: (b * nblk + i, h, 0)),
            pl.BlockSpec((seq, LANES), lambda b, h, i: (b, kcol + h)),
            pl.BlockSpec((nblk, LANES, x), lambda b, h, i: (b, h, 0)),
            pl.BlockSpec((1, 5, x, x), lambda b, h, i: (h, 0, 0, 0)),
            pl.BlockSpec((4, DIFF_HALF_DIM), lambda b, h, i: (0, 0)),
            pl.BlockSpec((DIFF_V_DIM, 1), lambda b, h, i: (0, 0)),
        ],
        out_specs=pl.BlockSpec((1, LANES, x), lambda b, h, i: (b * nblk + i, h, 0)),
        out_shape=jax.ShapeDtypeStruct((n // x, W_DIFF, x), BF16),
        scratch_shapes=[pltpu.VMEM((x, 2 * x), F32), pltpu.VMEM((x, 2 * x), F32)],
        compiler_params=_params(("parallel", "parallel", "parallel")),
        name="diff_attention",
    )(qd_t, proj, vd_t, bias_tab, lam_vecs, subln_col)


def _outproj_body(x_ref, ya_ref, yb_ref, wa_ref, wb_ref, nw_ref, wr_ref, x2_ref, h2_ref, aff_ref):
    attn = (jnp.dot(ya_ref[...], wa_ref[...], preferred_element_type=F32)
            + lax.dot_general(yb_ref[0], wb_ref[...], (((0,), (0,)), ((), ())), preferred_element_type=F32))
    x2 = x_ref[...] + attn
    x2_ref[...] = x2
    ms = jnp.mean(x2 * x2, axis=-1, keepdims=True)
    h2 = (x2 * lax.rsqrt(ms + EPS)) * nw_ref[...]
    for s in range(D_MODEL // LANES):
        h2_ref[:, s, :] = h2[:, s * LANES:(s + 1) * LANES]
    logits = lax.dot_general(wr_ref[...], h2, _NT, precision=lax.Precision.HIGHEST,
                             preferred_element_type=F32)
    e = jnp.exp(logits - jnp.max(logits, axis=0, keepdims=True))
    aff_ref[...] = e / jnp.sum(e, axis=0, keepdims=True)


def _outproj(x2d, ya, yb, w_out, norm_w, w_router_t):
    n = x2d.shape[0]
    tm = TOKEN_TILE
    row = lambda i: (i, 0)
    fixed = lambda i: (0, 0)
    return pl.pallas_call(
        _outproj_body,
        grid=(n // tm,),
        in_specs=[
            pl.BlockSpec((tm, D_MODEL), row),
            pl.BlockSpec((tm, W_NA), row),
            pl.BlockSpec((1, W_DIFF, tm), lambda i: (i, 0, 0)),
            pl.BlockSpec((W_NA, D_MODEL), lambda i: (0, 0)),
            pl.BlockSpec((W_DIFF, D_MODEL), lambda i: (1, 0)),
            pl.BlockSpec((1, D_MODEL), fixed),
            pl.BlockSpec((N_EXPERTS, D_MODEL), fixed),
        ],
        out_specs=[
            pl.BlockSpec((tm, D_MODEL), row),
            pl.BlockSpec((tm, D_MODEL // LANES, LANES), lambda i: (i, 0, 0)),
            pl.BlockSpec((N_EXPERTS, tm), lambda i: (0, i)),
        ],
        out_shape=[
            jax.ShapeDtypeStruct((n, D_MODEL), F32),
            jax.ShapeDtypeStruct((n, D_MODEL // LANES, LANES), F32),
            jax.ShapeDtypeStruct((N_EXPERTS, n), F32),
        ],
        compiler_params=_params(("parallel",)),
        name="outproj_router",
    )(x2d, ya, yb, w_out, w_out, norm_w, w_router_t)


def _moe_body(idx_ref, idx_next_ref, h2_hbm, g_ref, wg_ref, wu_ref, wd_ref, o_ref, x_even, x_odd, sem, *, tm):
    nt = pl.num_programs(1)
    step = pl.program_id(0) * nt + pl.program_id(1)
    total = pl.num_programs(0) * nt

    def row_copy(tok, i, buf, sl):
        return pltpu.make_async_copy(h2_hbm.at[pl.ds(tok, 1)], buf.at[pl.ds(i, 1)], sem.at[sl])

    def wait_tile(buf, sl):
        pltpu.make_async_copy(h2_hbm.at[pl.ds(0, tm)], buf, sem.at[sl]).wait()

    @pl.when(step == 0)
    def _():
        def body(i, carry):
            row_copy(idx_ref[0, 0, i], i, x_even, 0).start()
            return carry
        lax.fori_loop(0, tm, body, 0, unroll=8)

    def run(cur, cur_sl, nxt, nxt_sl):
        wait_tile(cur, cur_sl)
        for i in range(tm):
            row_copy(idx_next_ref[0, 0, i], i, nxt, nxt_sl).start()
        x = jnp.concatenate([cur[:, s, :] for s in range(D_MODEL // LANES)], axis=1).astype(BF16)
        acc = jnp.zeros((tm, D_MODEL), F32)
        for f0, fc in FFN_CHUNKS:
            g = jnp.dot(x, wg_ref[0, :, f0:f0 + fc], preferred_element_type=F32)
            u = jnp.dot(x, wu_ref[0, :, f0:f0 + fc], preferred_element_type=F32)
            h = (jax.nn.silu(g) * u).astype(BF16)
            acc = acc + jnp.dot(h, wd_ref[0, f0:f0 + fc, :], preferred_element_type=F32)
        o_ref[...] = (acc * g_ref[...]).astype(o_ref.dtype)

        @pl.when(step == total - 1)
        def _():
            wait_tile(nxt, nxt_sl)

    @pl.when(step % 2 == 0)
    def _():
        run(x_even, 0, x_odd, 1)

    @pl.when(step % 2 == 1)
    def _():
        run(x_odd, 1, x_even, 0)


def _moe_experts(h2, idx, gates, w_gate, w_up, w_down):
    n_exp, cap = idx.shape
    tm = min(MOE_TILE, cap)
    assert cap % tm == 0
    nt = cap // tm
    idx3 = idx.reshape(n_exp * nt, 1, tm)

    def next_tile(e, t):
        return (jnp.minimum(e * nt + t + 1, n_exp * nt - 1), 0, 0)

    return pl.pallas_call(
        functools.partial(_moe_body, tm=tm),
        grid=(n_exp, nt),
        in_specs=[
            pl.BlockSpec((1, 1, tm), lambda e, t: (e * nt + t, 0, 0), memory_space=pltpu.SMEM),
            pl.BlockSpec((1, 1, tm), next_tile, memory_space=pltpu.SMEM),
            pl.BlockSpec(memory_space=pl.ANY),
            pl.BlockSpec((tm, 1), lambda e, t: (e * nt + t, 0)),
            pl.BlockSpec((1, D_MODEL, D_EXPERT), lambda e, t: (e, 0, 0)),
            pl.BlockSpec((1, D_MODEL, D_EXPERT), lambda e, t: (e, 0, 0)),
            pl.BlockSpec((1, D_EXPERT, D_MODEL), lambda e, t: (e, 0, 0)),
        ],
        out_specs=pl.BlockSpec((tm, D_MODEL), lambda e, t: (e * nt + t, 0)),
        out_shape=jax.ShapeDtypeStruct((n_exp * cap, D_MODEL), BF16),
        scratch_shapes=[pltpu.VMEM((tm, D_MODEL // LANES, LANES), F32),
                        pltpu.VMEM((tm, D_MODEL // LANES, LANES), F32),
                        pltpu.SemaphoreType.DMA((2,))],
        compiler_params=_params(("arbitrary", "arbitrary")),
        name="moe_experts",
    )(idx3, idx3, h2, gates.reshape(n_exp * cap, 1), w_gate, w_up, w_down)


def _route_threshold_body(aff_ref, theta_ref, need_ref, *, cap):
    n_exp = aff_ref.shape[0]

    def count_ge(cand):
        bits = pltpu.bitcast(aff_ref[...], jnp.int32)
        return jnp.sum(jnp.where(bits >= cand, 1.0, 0.0), axis=1, keepdims=True)

    def step(b, theta):
        cand = theta | jnp.left_shift(jnp.int32(1), 30 - b)
        return jnp.where(count_ge(cand) >= cap, cand, theta)

    theta = lax.fori_loop(0, 31, step, jnp.zeros((n_exp, 1), jnp.int32))
    need = cap - count_ge(theta + 1)
    theta_ref[...] = jnp.broadcast_to(theta, theta_ref.shape)
    need_ref[...] = jnp.broadcast_to(need, need_ref.shape)


def _route_threshold(aff_t, cap):
    n_exp, n = aff_t.shape
    full = lambda i: (0, 0)
    return pl.pallas_call(
        functools.partial(_route_threshold_body, cap=cap),
        grid=(1,),
        in_specs=[pl.BlockSpec((n_exp, n), full)],
        out_specs=[pl.BlockSpec((n_exp, LANES), full), pl.BlockSpec((n_exp, LANES), full)],
        out_shape=[jax.ShapeDtypeStruct((n_exp, LANES), jnp.int32), jax.ShapeDtypeStruct((n_exp, LANES), F32)],
        compiler_params=_params(("arbitrary",)),
        name="route_threshold",
    )(aff_t)


def _route_prefix_body(aff_ref, tri_ref, theta_ref, need_ref, slot_ref, base_ref, run_sel, run_tie, *, cap):
    @pl.when(pl.program_id(0) == 0)
    def _():
        run_sel[...] = jnp.zeros_like(run_sel)
        run_tie[...] = jnp.zeros_like(run_tie)

    n_exp = aff_ref.shape[0]
    w = ROUTE_CHUNK
    theta = theta_ref[:, 0:1]
    need = need_ref[:, 0:1]
    expert_base = lax.broadcasted_iota(jnp.int32, (n_exp, 1), 0) * cap
    tri = tri_ref[...]
    for k in range(ROUTE_CHUNKS_PER_STEP):
        bits = pltpu.bitcast(aff_ref[:, k * w:(k + 1) * w], jnp.int32)
        tie = bits == theta
        tie_f = jnp.where(tie, 1.0, 0.0)
        tie_incl = jnp.dot(tie_f.astype(BF16), tri, preferred_element_type=F32)
        tie_rank = run_tie[:, 0:1] + tie_incl - tie_f
        sel = (bits > theta) | (tie & (tie_rank < need))
        sel_f = jnp.where(sel, 1.0, 0.0)
        incl = jnp.dot(sel_f.astype(BF16), tri, preferred_element_type=F32)
        rank = run_sel[:, 0:1] + incl - sel_f
        slot_ref[:, k * w:(k + 1) * w] = jnp.where(sel, rank.astype(jnp.int32) + expert_base, -1)
        base_ref[k] = run_sel[...].astype(jnp.int32)
        run_sel[...] = run_sel[...] + jnp.sum(sel_f, axis=1, keepdims=True)
        run_tie[...] = run_tie[...] + jnp.sum(tie_f, axis=1, keepdims=True)


def _route_prefix(aff_t, theta, need, cap):
    n_exp, n = aff_t.shape
    w = ROUTE_CHUNK
    span = w * ROUTE_CHUNKS_PER_STEP
    assert n % span == 0
    tri = (jnp.arange(w)[:, None] <= jnp.arange(w)[None, :]).astype(BF16)
    fixed = lambda i: (0, 0)
    return pl.pallas_call(
        functools.partial(_route_prefix_body, cap=cap),
        grid=(n // span,),
        in_specs=[
            pl.BlockSpec((n_exp, span), lambda i: (0, i)),
            pl.BlockSpec((w, w), fixed),
            pl.BlockSpec((n_exp, LANES), fixed),
            pl.BlockSpec((n_exp, LANES), fixed),
        ],
        out_specs=[
            pl.BlockSpec((n_exp, span), lambda i: (0, i)),
            pl.BlockSpec((ROUTE_CHUNKS_PER_STEP, n_exp, LANES), lambda i: (i, 0, 0)),
        ],
        out_shape=[
            jax.ShapeDtypeStruct((n_exp, n), jnp.int32),
            jax.ShapeDtypeStruct((n // w, n_exp, LANES), jnp.int32),
        ],
        scratch_shapes=[pltpu.VMEM((n_exp, LANES), F32), pltpu.VMEM((n_exp, LANES), F32)],
        compiler_params=_params(("arbitrary",)),
        name="route_prefix",
    )(aff_t, tri, theta, need)


def _route_compact_body(cb_ref, slot_ref, aff_ref, idx_ref, gate_ref, *, cap, nchunks):
    e = pl.program_id(0)
    w = ROUTE_CHUNK
    s = LANES
    slot_iota = lax.broadcasted_iota(jnp.int32, (s, w), 0)
    tok_iota = lax.broadcasted_iota(jnp.int32, (s, w), 1)
    eye = lax.broadcasted_iota(jnp.int32, (s, s), 0) == lax.broadcasted_iota(jnp.int32, (s, s), 1)

    def fold(v):
        out = v[:, 0:s]
        for q in range(1, w // s):
            out = out + v[:, q * s:(q + 1) * s]
        return out

    def to_row(acc):
        col = jnp.sum(acc, axis=1, keepdims=True)
        return jnp.sum(jnp.where(eye, col, 0.0), axis=0, keepdims=True)

    def one_tile(jt, c_first):
        j0 = jt * s
        c_first = lax.while_loop(lambda c: cb_ref[e, c + 1] <= j0, lambda c: c + 1, c_first)
        want = slot_iota + (e * cap + j0)

        def more(state):
            c = state[0]
            return jnp.logical_and(c < nchunks, cb_ref[e, jnp.minimum(c, nchunks - 1)] < j0 + s)

        def chunk(state):
            c, acc_i, acc_g = state
            hit = want == slot_ref[0, pl.ds(c, 1), :]
            tok = (tok_iota + c * w).astype(F32)
            acc_i = acc_i + fold(jnp.where(hit, tok, 0.0))
            acc_g = acc_g + fold(jnp.where(hit, aff_ref[0, pl.ds(c, 1), :], 0.0))
            return c + 1, acc_i, acc_g

        zero = jnp.zeros((s, s), F32)
        _, acc_i, acc_g = lax.while_loop(more, chunk, (c_first, zero, zero))
        idx_ref[0, pl.ds(jt, 1), :] = to_row(acc_i).astype(jnp.int32)
        gate_ref[0, pl.ds(jt, 1), :] = to_row(acc_g)
        return c_first

    lax.fori_loop(0, cap // s, one_tile, jnp.int32(0))


def _route_compact(chunk_base, slot3, aff3, cap):
    n_exp, nchunks, w = slot3.shape
    assert cap % LANES == 0
    per_e = lambda e, cb: (e, 0, 0)
    grid_spec = pltpu.PrefetchScalarGridSpec(
        num_scalar_prefetch=1,
        grid=(n_exp,),
        in_specs=[pl.BlockSpec((1, nchunks, w), per_e), pl.BlockSpec((1, nchunks, w), per_e)],
        out_specs=[pl.BlockSpec((1, cap // LANES, LANES), per_e), pl.BlockSpec((1, cap // LANES, LANES), per_e)],
    )
    return pl.pallas_call(
        functools.partial(_route_compact_body, cap=cap, nchunks=nchunks),
        grid_spec=grid_spec,
        out_shape=[
            jax.ShapeDtypeStruct((n_exp, cap // LANES, LANES), jnp.int32),
            jax.ShapeDtypeStruct((n_exp, cap // LANES, LANES), F32),
        ],
        compiler_params=_params(("arbitrary",)),
        name="route_compact",
    )(chunk_base, slot3, aff3)


def _combine_body(cb_ref, slot_ref, x2_ref, ye_hbm, nw_ref, o_ref, stage, chunk_row, chunk_count, sem,
                  *, cap, ntiles):
    tile = pl.program_id(0)
    cur = tile % 2
    tt = COMBINE_TILE
    ch = COMBINE_DMA_ROWS
    n_exp = slot_ref.shape[0]
    kblock = 256 // ch

    def chunk_copy(row, k, sl):
        return pltpu.make_async_copy(ye_hbm.at[pl.ds(row, ch), :], stage.at[sl, pl.ds(k * ch, ch), :], sem.at[sl])

    def issue(t, sl):
        def per_expert(e, k):
            lo = cb_ref[e, t]
            hi = cb_ref[e, t + 1]
            first = lo // ch
            nch = jnp.where(hi > lo, (hi - 1) // ch - first + 1, 0)

            def per_chunk(i, k):
                row = pl.multiple_of(e * cap + (first + i) * ch, ch)
                chunk_copy(row, k, sl).start()
                chunk_row[sl, k] = row
                return k + 1

            return lax.fori_loop(0, nch, per_chunk, k)

        chunk_count[sl] = lax.fori_loop(0, n_exp, per_expert, jnp.int32(0))

    @pl.when(tile == 0)
    def _():
        issue(0, 0)

    @pl.when(tile + 1 < ntiles)
    def _():
        issue(tile + 1, 1 - cur)

    nchunk = chunk_count[cur]

    def wait_one(k, carry):
        chunk_copy(0, k, cur).wait()
        return carry

    lax.fori_loop(0, nchunk, wait_one, 0)
    nblock = (nchunk + kblock - 1) // kblock

    def pad_one(k, carry):
        stage[cur, pl.ds(pl.multiple_of(k * ch, ch), ch), :] = jnp.zeros((ch, D_MODEL), BF16)
        chunk_row[cur, k] = -(1 << 30)
        return carry

    lax.fori_loop(nchunk, nblock * kblock, pad_one, 0)
    row_iota = lax.broadcasted_iota(jnp.int32, (ch, tt), 0)

    def one_block(b, acc):
        parts = []
        for i in range(kblock):
            row = chunk_row[cur, b * kblock + i]
            e = jnp.clip(row // cap, 0, n_exp - 1)
            hit = (row_iota + row) == slot_ref[pl.ds(e, 1), :]
            parts.append(jnp.where(hit, 1.0, 0.0).astype(BF16))
        onehot_t = jnp.concatenate(parts, axis=0)
        rows = stage[cur, pl.ds(pl.multiple_of(b * 256, 256), 256), :]
        return acc + lax.dot_general(onehot_t, rows, (((0,), (0,)), ((), ())), preferred_element_type=F32)

    moe = lax.fori_loop(0, nblock, one_block, jnp.zeros((tt, D_MODEL), F32))
    x3 = x2_ref[...] + moe
    ms = jnp.mean(x3 * x3, axis=-1, keepdims=True)
    o_ref[...] = (x3 * lax.rsqrt(ms + EPS)) * nw_ref[...]


def _combine(tile_base, slot, x2, ye, norm_w, cap):
    n_exp, n = slot.shape
    tt = COMBINE_TILE
    ch = COMBINE_DMA_ROWS
    ntiles = n // tt
    assert cap % ch == 0 and tt % ch == 0
    max_rows = n_exp * (tt + ch)
    max_rows = -(-max_rows // 256) * 256
    grid_spec = pltpu.PrefetchScalarGridSpec(
        num_scalar_prefetch=1,
        grid=(ntiles,),
        in_specs=[
            pl.BlockSpec((n_exp, tt), lambda t, cb: (0, t)),
            pl.BlockSpec((tt, D_MODEL), lambda t, cb: (t, 0)),
            pl.BlockSpec(memory_space=pl.ANY),
            pl.BlockSpec((1, D_MODEL), lambda t, cb: (0, 0)),
        ],
        out_specs=pl.BlockSpec((tt, D_MODEL), lambda t, cb: (t, 0)),
        scratch_shapes=[
            pltpu.VMEM((2, max_rows, D_MODEL), BF16),
            pltpu.SMEM((2, max_rows // ch), jnp.int32),
            pltpu.SMEM((2,), jnp.int32),
            pltpu.SemaphoreType.DMA((2,)),
        ],
    )
    return pl.pallas_call(
        functools.partial(_combine_body, cap=cap, ntiles=ntiles),
        grid_spec=grid_spec,
        out_shape=jax.ShapeDtypeStruct((n, D_MODEL), F32),
        compiler_params=_params(("arbitrary",)),
        name="combine_final",
    )(tile_base, slot, x2, ye, norm_w)


def _trunk(x, p):
    batch, seq = x.shape[0], x.shape[1]
    n = batch * seq
    x2d = x.reshape(n, D_MODEL)
    proj, qd_t, vd_t = _inproj(x2d, p["mix_norm_w"], p["w_tok"], p["w_qd_t"], p["w_vd_t"])
    ya = _na_attention(proj, p["na_bias"], p["na_norm_w2"], batch, seq)
    yb_t = _diff_attention(proj, qd_t, vd_t, p["diff_bias"], p["lam_vecs"], p["subln_col"], batch, seq)
    x2, h2, aff_t = _outproj(x2d, ya, yb_t, p["w_out"], p["ffn_norm_w"], p["w_router_t"])
    cap = CAPACITY_FACTOR * n // N_EXPERTS
    theta, need = _route_threshold(aff_t, cap)
    slot, base = _route_prefix(aff_t, theta, need, cap)
    nchunks = n // ROUTE_CHUNK
    chunk_base = jnp.concatenate([base[:, :, 0].T, jnp.full((N_EXPERTS, 1), cap, jnp.int32)], axis=1)
    idx, gates = _route_compact(chunk_base, slot.reshape(N_EXPERTS, nchunks, ROUTE_CHUNK),
                                aff_t.reshape(N_EXPERTS, nchunks, ROUTE_CHUNK), cap)
    ye = _moe_experts(h2, idx.reshape(N_EXPERTS, cap), gates.reshape(N_EXPERTS, cap),
                      p["w_gate"], p["w_up"], p["w_down"])
    tile_base = chunk_base[:, ::COMBINE_TILE // ROUTE_CHUNK]
    y = _combine(tile_base, slot, x2, ye, p["final_norm_w"], cap)
    return y.reshape(batch, seq, D_MODEL)


def _prepare_params(mix_norm_w, w_in, na_rpb, na_norm_w, t5_table, lambda_q1, lambda_k1, lambda_q2, lambda_k2,
                    subln_w, w_out, ffn_norm_w, w_router, w_gate, w_up, w_down, final_norm_w):
    w = w_in[0]
    qa, ka_va = w[:, 0:W_NA] * NA_HEAD_DIM ** -0.5, w[:, W_NA:3 * W_NA]
    qd = w[:, 3 * W_NA:3 * W_NA + W_DIFF] * DIFF_HALF_DIM ** -0.5
    kd, vd = w[:, 3 * W_NA + W_DIFF:3 * W_NA + 2 * W_DIFF], w[:, 3 * W_NA + 2 * W_DIFF:]
    return {
        "mix_norm_w": mix_norm_w[0].reshape(1, D_MODEL),
        "w_tok": jnp.concatenate([qa, ka_va, kd], axis=1).astype(BF16),
        "w_qd_t": qd.T.astype(BF16),
        "w_vd_t": vd.T.astype(BF16),
        "na_bias": _na_bias_table(na_rpb[0]),
        "na_norm_w2": jnp.tile(na_norm_w[0], 2).reshape(1, LANES),
        "diff_bias": _diff_bias_table(t5_table),
        "lam_vecs": jnp.stack([lambda_q1[0], lambda_k1[0], lambda_q2[0], lambda_k2[0]]).astype(F32),
        "subln_col": subln_w[0].reshape(DIFF_V_DIM, 1),
        "w_out": w_out[0].astype(BF16),
        "ffn_norm_w": ffn_norm_w[0].reshape(1, D_MODEL),
        "w_router_t": w_router[0].T,
        "w_gate": w_gate[0].astype(BF16),
        "w_up": w_up[0].astype(BF16),
        "w_down": w_down[0].astype(BF16),
        "final_norm_w": final_norm_w.reshape(1, D_MODEL),
    }


def kernel(x_prompt, x_sample, mix_norm_w, w_in, na_rpb, na_norm_w, t5_table, lambda_q1, lambda_k1,
           lambda_q2, lambda_k2, subln_w, w_out, ffn_norm_w, w_router, w_gate, w_up, w_down, final_norm_w):
    p = _prepare_params(mix_norm_w, w_in, na_rpb, na_norm_w, t5_table, lambda_q1, lambda_k1, lambda_q2,
                        lambda_k2, subln_w, w_out, ffn_norm_w, w_router, w_gate, w_up, w_down, final_norm_w)
    return (_trunk(x_prompt, p), _trunk(x_sample, p))
```

```python
import functools
import math

import jax
import jax.numpy as jnp
from jax import lax
from jax.experimental import pallas as pl
from jax.experimental.pallas import tpu as pltpu

F32 = jnp.float32
BF16 = jnp.bfloat16

D_MODEL = 1024
GRID_W = 64
NA_HEADS = 8
NA_HEAD_DIM = 64
NA_KH = 8
NA_KW = 16
DIFF_HEADS = 4
DIFF_HALF_DIM = 64
DIFF_V_DIM = 128
W_NA = 512
W_DIFF = 512
D_IN = 3072
T5_BUCKETS = 32
T5_MAX_DIST = 128
N_EXPERTS = 16
CAPACITY_FACTOR = 2
D_EXPERT = 2816
EPS = 1e-6
LAMBDA_INIT = 0.8 - 0.6 * math.exp(-0.3 * 0)
LOG2E = math.log2(math.e)
TOKEN_MAJOR_COLS = 3 * W_NA + W_DIFF

LANES = 128
TOKEN_TILE = 512
NA_ROWS_PER_STEP = 8
DIFF_BLOCK = 512
DIFF_ONES_ROWS = 16
MOE_TILE = 512
ROUTE_CHUNK = 256
ROUTE_CHUNKS_PER_STEP = 8
COMBINE_TILE = 512
COMBINE_DMA_ROWS = 32
FFN_CHUNKS = ((0, 512), (512, 512), (1024, 512), (1536, 512), (2048, 512), (2560, 256))
MASKED = -1e30
VMEM_LIMIT = 56 * 1024 * 1024

_NT = (((1,), (1,)), ((), ()))


def _params(sem, vmem=VMEM_LIMIT):
    return pltpu.CompilerParams(dimension_semantics=sem, vmem_limit_bytes=vmem)


def _inproj_body(x_ref, nw_ref, w_ref, wqt_ref, wvt_ref, o_ref, qt_ref, vt_ref):
    x = x_ref[...]
    ms = jnp.mean(x * x, axis=-1, keepdims=True)
    h = ((x * lax.rsqrt(ms + EPS)) * nw_ref[...]).astype(BF16)
    for c in range(TOKEN_MAJOR_COLS // W_NA):
        cols = slice(c * W_NA, (c + 1) * W_NA)
        o_ref[:, cols] = jnp.dot(h, w_ref[:, cols], preferred_element_type=F32).astype(BF16)
    qt = lax.dot_general(wqt_ref[...], h, _NT, preferred_element_type=F32)
    qt_ref[0] = (qt * LOG2E).astype(BF16)
    vt_ref[0] = lax.dot_general(wvt_ref[...], h, _NT, preferred_element_type=F32).astype(BF16)


def _inproj(x2d, norm_w, w_tok, w_qd_t, w_vd_t):
    n = x2d.shape[0]
    tm = TOKEN_TILE
    fixed = lambda i: (0, 0)
    return pl.pallas_call(
        _inproj_body,
        grid=(n // tm,),
        in_specs=[
            pl.BlockSpec((tm, D_MODEL), lambda i: (i, 0)),
            pl.BlockSpec((1, D_MODEL), fixed),
            pl.BlockSpec((D_MODEL, TOKEN_MAJOR_COLS), fixed),
            pl.BlockSpec((W_DIFF, D_MODEL), fixed),
            pl.BlockSpec((W_DIFF, D_MODEL), fixed),
        ],
        out_specs=[
            pl.BlockSpec((tm, TOKEN_MAJOR_COLS), lambda i: (i, 0)),
            pl.BlockSpec((1, W_DIFF, tm), lambda i: (i, 0, 0)),
            pl.BlockSpec((1, W_DIFF, tm), lambda i: (i, 0, 0)),
        ],
        out_shape=[
            jax.ShapeDtypeStruct((n, TOKEN_MAJOR_COLS), BF16),
            jax.ShapeDtypeStruct((n // tm, W_DIFF, tm), BF16),
            jax.ShapeDtypeStruct((n // tm, W_DIFF, tm), BF16),
        ],
        compiler_params=_params(("parallel",)),
        name="inproj",
    )(x2d, norm_w, w_tok, w_qd_t, w_vd_t)


def _na_bias_table(rpb):
    c = jnp.arange(GRID_W)
    col_start = jnp.clip(c - NA_KW // 2, 0, GRID_W - NA_KW)
    in_win = (c[None, :] >= col_start[:, None]) & (c[None, :] < col_start[:, None] + NA_KW)
    dc = jnp.clip(c[None, :] - c[:, None], -(NA_KW - 1), NA_KW - 1) + (NA_KW - 1)
    bc = rpb.astype(F32)[:, :, dc]
    bc = jnp.where(in_win[None, None], bc, MASKED)
    tabs = jnp.stack([bc[:, d0:d0 + NA_KH] for d0 in range(NA_KH)])
    tabs = tabs.transpose(0, 1, 3, 2, 4)
    return tabs.reshape(NA_KH, NA_HEADS // 2, 2 * GRID_W, NA_KH * GRID_W)


def _na_body(q_ref, kp_ref, kc_ref, kn_ref, vp_ref, vc_ref, vn_ref, bias_ref, nw_ref, o_ref,
             kbuf, vbuf, *, rows):
    j = pl.program_id(1)
    blk = NA_ROWS_PER_STEP * GRID_W
    kbuf[0:blk] = kp_ref[...]
    kbuf[blk:2 * blk] = kc_ref[...]
    kbuf[2 * blk:3 * blk] = kn_ref[...]
    vbuf[0:blk] = vp_ref[...]
    vbuf[blk:2 * blk] = vc_ref[...]
    vbuf[2 * blk:3 * blk] = vn_ref[...]
    lo = lax.broadcasted_iota(jnp.int32, (GRID_W, LANES), 1) < NA_HEAD_DIM
    nw = nw_ref[...]
    nkeys = NA_KH * GRID_W

    def softmax_parts(s):
        m = jnp.max(s, axis=-1, keepdims=True)
        p = jnp.exp(s - m)
        return p.astype(BF16), jnp.sum(p, axis=-1, keepdims=True)

    def one_row(a, carry):
        r = j * NA_ROWS_PER_STEP + a
        row_start = jnp.clip(r - NA_KH // 2, 0, rows - NA_KH)
        d0 = row_start - r + (NA_KH - 1)
        koff = pl.multiple_of((row_start - (j - 1) * NA_ROWS_PER_STEP) * GRID_W, GRID_W)
        qoff = pl.multiple_of(a * GRID_W, GRID_W)
        pairs = range(NA_HEADS // 2)
        cols = [slice(hp * LANES, (hp + 1) * LANES) for hp in pairs]
        scores = []
        for hp in pairs:
            qp = q_ref[pl.ds(qoff, GRID_W), cols[hp]]
            zero = jnp.zeros_like(qp)
            q2 = jnp.concatenate([jnp.where(lo, qp, zero), jnp.where(lo, zero, qp)], axis=0)
            kk = kbuf[pl.ds(koff, nkeys), cols[hp]]
            scores.append(lax.dot_general(q2, kk, _NT, preferred_element_type=F32) + bias_ref[d0, hp])
        probs = [softmax_parts(s) for s in scores]
        outs = [jnp.dot(p, vbuf[pl.ds(koff, nkeys), cols[hp]], preferred_element_type=F32) / l
                for hp, (p, l) in zip(pairs, probs)]
        for hp in pairs:
            cs = cols[hp]
            o = jnp.where(lo, outs[hp][0:GRID_W], outs[hp][GRID_W:2 * GRID_W])
            sq = o * o
            msa = jnp.sum(jnp.where(lo, sq, 0.0), axis=-1, keepdims=True) * (1.0 / NA_HEAD_DIM)
            msb = jnp.sum(jnp.where(lo, 0.0, sq), axis=-1, keepdims=True) * (1.0 / NA_HEAD_DIM)
            inv = jnp.where(lo, lax.rsqrt(msa + EPS), lax.rsqrt(msb + EPS))
            o_ref[pl.ds(qoff, GRID_W), cs] = ((o * inv) * nw).astype(BF16)
        return carry

    lax.fori_loop(0, NA_ROWS_PER_STEP, one_row, 0, unroll=4)


def _na_attention(proj, bias_tab, norm_w2, batch, seq):
    rows = seq // GRID_W
    assert rows % NA_ROWS_PER_STEP == 0 and rows >= NA_KH
    nb = rows // NA_ROWS_PER_STEP
    blk = NA_ROWS_PER_STEP * GRID_W
    n = batch * seq

    def at(col, shift):
        return pl.BlockSpec(
            (blk, W_NA), lambda b, j: (b * nb + jnp.clip(j + shift, 0, nb - 1), col))

    return pl.pallas_call(
        functools.partial(_na_body, rows=rows),
        grid=(batch, nb),
        in_specs=[
            at(0, 0),
            at(1, -1), at(1, 0), at(1, 1),
            at(2, -1), at(2, 0), at(2, 1),
            pl.BlockSpec((NA_KH, NA_HEADS // 2, 2 * GRID_W, NA_KH * GRID_W), lambda b, j: (0, 0, 0, 0)),
            pl.BlockSpec((1, LANES), lambda b, j: (0, 0)),
        ],
        out_specs=pl.BlockSpec((blk, W_NA), lambda b, j: (b * nb + j, 0)),
        out_shape=jax.ShapeDtypeStruct((n, W_NA), BF16),
        scratch_shapes=[pltpu.VMEM((3 * blk, W_NA), BF16), pltpu.VMEM((3 * blk, W_NA), BF16)],
        compiler_params=_params(("parallel", "parallel")),
        name="na_attention",
    )(proj, proj, proj, proj, proj, proj, proj, bias_tab, norm_w2)


def _t5_bucket(rel):
    nb = T5_BUCKETS // 2
    max_exact = nb // 2
    n = jnp.abs(rel)
    large = max_exact + (jnp.log(jnp.maximum(n, 1).astype(F32) / max_exact)
                         / math.log(T5_MAX_DIST / max_exact) * (nb - max_exact)).astype(jnp.int32)
    large = jnp.minimum(large, nb - 1)
    return jnp.where(rel > 0, nb, 0) + jnp.where(n < max_exact, n, large)


def _diff_bias_table(t5_table):
    x = DIFF_BLOCK
    assert x + 1 >= T5_MAX_DIST
    span = 2 * x
    rel = (x - 1) - jnp.arange(span)[None, :] + jnp.arange(-2, 3)[:, None] * x
    diag = (t5_table.astype(F32) * LOG2E)[_t5_bucket(rel)].transpose(2, 0, 1)
    skew = jnp.tile(diag, (1, 1, x))[:, :, :x * (span - 1)].reshape(DIFF_HEADS, 5, x, span - 1)
    return skew[:, :, :, x - 1:2 * x - 1]


def _diff_body(qt_ref, k_ref, vt_ref, bias_ref, lam_ref, sw_ref, o_ref, s_even, s_odd, *, nblk):
    i = pl.program_id(2)
    x = DIFF_BLOCK
    qt = qt_ref[0]
    first = lax.broadcasted_iota(jnp.int32, (LANES, x), 0) < DIFF_HALF_DIM
    zero = jnp.zeros_like(qt)
    qt_both = jnp.concatenate([jnp.where(first, qt, zero), jnp.where(first, zero, qt)], axis=1)

    def scores(j, s_ref):
        kc = k_ref[pl.ds(pl.multiple_of(j * x, x), x), :]
        bt = bias_ref[0, jnp.clip(j - i, -2, 2) + 2]
        for c in range(2):
            half = slice(c * x, (c + 1) * x)
            s_ref[:, half] = jnp.dot(kc, qt_both[:, half], preferred_element_type=F32) + bt

    ones_rows = jnp.ones((DIFF_ONES_ROWS, x), BF16)

    def consume(j, s_ref, carry):
        vt_ext = jnp.concatenate([vt_ref[j], ones_rows], axis=0)
        out = []
        for c in range(2):
            m, acc = carry[2 * c:2 * c + 2]
            half = slice(c * x, (c + 1) * x)
            mn = jnp.maximum(m, jnp.max(s_ref[:, half], axis=0, keepdims=True))
            p = jnp.exp2(s_ref[:, half] - mn).astype(BF16)
            acc = jnp.exp2(m - mn) * acc + jnp.dot(vt_ext, p, preferred_element_type=F32)
            out += [mn, acc]
        return tuple(out)

    def body(jj, carry):
        j = 2 * jj
        scores(j + 1, s_odd)
        carry = consume(j, s_even, carry)
        scores(j + 2, s_even)
        return consume(j + 1, s_odd, carry)

    init = (jnp.full((1, x), -jnp.inf, F32), jnp.zeros((DIFF_V_DIM + DIFF_ONES_ROWS, x), F32)) * 2
    scores(0, s_even)
    carry = lax.fori_loop(0, nblk // 2 - 1, body, init)
    scores(nblk - 1, s_odd)
    carry = consume(nblk - 2, s_even, carry)
    _, ext1, _, ext2 = consume(nblk - 1, s_odd, carry)
    acc1, l1 = ext1[0:DIFF_V_DIM], ext1[DIFF_V_DIM:DIFF_V_DIM + 1]
    acc2, l2 = ext2[0:DIFF_V_DIM], ext2[DIFF_V_DIM:DIFF_V_DIM + 1]
    lam = (jnp.exp(jnp.sum(lam_ref[0:1, :] * lam_ref[1:2, :], axis=-1, keepdims=True))
           - jnp.exp(jnp.sum(lam_ref[2:3, :] * lam_ref[3:4, :], axis=-1, keepdims=True)) + LAMBDA_INIT)
    o = acc1 / l1 - lam * (acc2 / l2)
    ms = jnp.mean(o * o, axis=0, keepdims=True)
    y = ((o * lax.rsqrt(ms + EPS)) * sw_ref[...]) * (1.0 - LAMBDA_INIT)
    o_ref[0] = y.astype(BF16)


def _diff_attention(proj, qd_t, vd_t, bias_tab, lam_vecs, subln_col, batch, seq):
    x = DIFF_BLOCK
    assert seq % (2 * x) == 0 and x == TOKEN_TILE
    nblk = seq // x
    n = batch * seq
    kcol = 3 * W_NA // LANES
    return pl.pallas_call(
        functools.partial(_diff_body, nblk=nblk),
        grid=(batch, DIFF_HEADS, nblk),
        in_specs=[
            pl.BlockSpec((1, LANES, x), lambda b, h, i: (b * nblk + i, h, 0)),
            pl.BlockSpec((seq, LANES), lambda b, h, i: (b, kcol + h)),
            pl.BlockSpec((nblk, LANES, x), lambda b, h, i: (b, h, 0)),
            pl.BlockSpec((1, 5, x, x), lambda b, h, i: (h, 0, 0, 0)),
            pl.BlockSpec((4, DIFF_HALF_DIM), lambda b, h, i: (0, 0)),
            pl.BlockSpec((DIFF_V_DIM, 1), lambda b, h, i: (0, 0)),
        ],
        out_specs=pl.BlockSpec((1, LANES, x), lambda b, h, i: (b * nblk + i, h, 0)),
        out_shape=jax.ShapeDtypeStruct((n // x, W_DIFF, x), BF16),
        scratch_shapes=[pltpu.VMEM((x, 2 * x), F32), pltpu.VMEM((x, 2 * x), F32)],
        compiler_params=_params(("parallel", "parallel", "parallel")),
        name="diff_attention",
    )(qd_t, proj, vd_t, bias_tab, lam_vecs, subln_col)


def _outproj_body(x_ref, ya_ref, yb_ref, wa_ref, wb_ref, nw_ref, wr_ref, x2_ref, h2_ref, aff_ref):
    attn = (jnp.dot(ya_ref[...], wa_ref[...], preferred_element_type=F32)
            + lax.dot_general(yb_ref[0], wb_ref[...], (((0,), (0,)), ((), ())), preferred_element_type=F32))
    x2 = x_ref[...] + attn
    x2_ref[...] = x2
    ms = jnp.mean(x2 * x2, axis=-1, keepdims=True)
    h2 = (x2 * lax.rsqrt(ms + EPS)) * nw_ref[...]
    for s in range(D_MODEL // LANES):
        h2_ref[:, s, :] = h2[:, s * LANES:(s + 1) * LANES]
    logits = lax.dot_general(wr_ref[...], h2.astype(BF16), _NT, preferred_element_type=F32)
    e = jnp.exp(logits - jnp.max(logits, axis=0, keepdims=True))
    aff_ref[...] = e / jnp.sum(e, axis=0, keepdims=True)


def _outproj(x2d, ya, yb, w_out, norm_w, w_router_t):
    n = x2d.shape[0]
    tm = TOKEN_TILE
    row = lambda i: (i, 0)
    fixed = lambda i: (0, 0)
    return pl.pallas_call(
        _outproj_body,
        grid=(n // tm,),
        in_specs=[
            pl.BlockSpec((tm, D_MODEL), row),
            pl.BlockSpec((tm, W_NA), row),
            pl.BlockSpec((1, W_DIFF, tm), lambda i: (i, 0, 0)),
            pl.BlockSpec((W_NA, D_MODEL), lambda i: (0, 0)),
            pl.BlockSpec((W_DIFF, D_MODEL), lambda i: (1, 0)),
            pl.BlockSpec((1, D_MODEL), fixed),
            pl.BlockSpec((N_EXPERTS, D_MODEL), fixed),
        ],
        out_specs=[
            pl.BlockSpec((tm, D_MODEL), row),
            pl.BlockSpec((tm, D_MODEL // LANES, LANES), lambda i: (i, 0, 0)),
            pl.BlockSpec((N_EXPERTS, tm), lambda i: (0, i)),
        ],
        out_shape=[
            jax.ShapeDtypeStruct((n, D_MODEL), F32),
            jax.ShapeDtypeStruct((n, D_MODEL // LANES, LANES), F32),
            jax.ShapeDtypeStruct((N_EXPERTS, n), F32),
        ],
        compiler_params=_params(("parallel",)),
        name="outproj_router",
    )(x2d, ya, yb, w_out, w_out, norm_w, w_router_t)


def _moe_body(idx_ref, idx_next_ref, h2_hbm, g_ref, wg_ref, wu_ref, wd_ref, o_ref, x_even, x_odd, sem, *, tm):
    nt = pl.num_programs(1)
    step = pl.program_id(0) * nt + pl.program_id(1)
    total = pl.num_programs(0) * nt

    def row_copy(tok, i, buf, sl):
        return pltpu.make_async_copy(h2_hbm.at[pl.ds(tok, 1)], buf.at[pl.ds(i, 1)], sem.at[sl])

    def wait_tile(buf, sl):
        pltpu.make_async_copy(h2_hbm.at[pl.ds(0, tm)], buf, sem.at[sl]).wait()

    @pl.when(step == 0)
    def _():
        def body(i, carry):
            row_copy(idx_ref[0, 0, i], i, x_even, 0).start()
            return carry
        lax.fori_loop(0, tm, body, 0, unroll=8)

    def run(cur, cur_sl, nxt, nxt_sl):
        wait_tile(cur, cur_sl)
        for i in range(tm):
            row_copy(idx_next_ref[0, 0, i], i, nxt, nxt_sl).start()
        x = jnp.concatenate([cur[:, s, :] for s in range(D_MODEL // LANES)], axis=1).astype(BF16)
        acc = jnp.zeros((tm, D_MODEL), F32)
        for f0, fc in FFN_CHUNKS:
            g = jnp.dot(x, wg_ref[0, :, f0:f0 + fc], preferred_element_type=F32)
            u = jnp.dot(x, wu_ref[0, :, f0:f0 + fc], preferred_element_type=F32)
            h = (jax.nn.silu(g) * u).astype(BF16)
            acc = acc + jnp.dot(h, wd_ref[0, f0:f0 + fc, :], preferred_element_type=F32)
        o_ref[...] = (acc * g_ref[...]).astype(o_ref.dtype)

        @pl.when(step == total - 1)
        def _():
            wait_tile(nxt, nxt_sl)

    @pl.when(step % 2 == 0)
    def _():
        run(x_even, 0, x_odd, 1)

    @pl.when(step % 2 == 1)
    def _():
        run(x_odd, 1, x_even, 0)


def _moe_experts(h2, idx, gates, w_gate, w_up, w_down):
    n_exp, cap = idx.shape
    tm = min(MOE_TILE, cap)
    assert cap % tm == 0
    nt = cap // tm
    idx3 = idx.reshape(n_exp * nt, 1, tm)

    def next_tile(e, t):
        return (jnp.minimum(e * nt + t + 1, n_exp * nt - 1), 0, 0)

    return pl.pallas_call(
        functools.partial(_moe_body, tm=tm),
        grid=(n_exp, nt),
        in_specs=[
            pl.BlockSpec((1, 1, tm), lambda e, t: (e * nt + t, 0, 0), memory_space=pltpu.SMEM),
            pl.BlockSpec((1, 1, tm), next_tile, memory_space=pltpu.SMEM),
            pl.BlockSpec(memory_space=pl.ANY),
            pl.BlockSpec((tm, 1), lambda e, t: (e * nt + t, 0)),
            pl.BlockSpec((1, D_MODEL, D_EXPERT), lambda e, t: (e, 0, 0)),
            pl.BlockSpec((1, D_MODEL, D_EXPERT), lambda e, t: (e, 0, 0)),
            pl.BlockSpec((1, D_EXPERT, D_MODEL), lambda e, t: (e, 0, 0)),
        ],
        out_specs=pl.BlockSpec((tm, D_MODEL), lambda e, t: (e * nt + t, 0)),
        out_shape=jax.ShapeDtypeStruct((n_exp * cap, D_MODEL), BF16),
        scratch_shapes=[pltpu.VMEM((tm, D_MODEL // LANES, LANES), F32),
                        pltpu.VMEM((tm, D_MODEL // LANES, LANES), F32),
                        pltpu.SemaphoreType.DMA((2,))],
        compiler_params=_params(("arbitrary", "arbitrary")),
        name="moe_experts",
    )(idx3, idx3, h2, gates.reshape(n_exp * cap, 1), w_gate, w_up, w_down)


def _route_threshold_body(aff_ref, theta_ref, need_ref, *, cap):
    n_exp = aff_ref.shape[0]

    def count_ge(cand):
        bits = pltpu.bitcast(aff_ref[...], jnp.int32)
        return jnp.sum(jnp.where(bits >= cand, 1.0, 0.0), axis=1, keepdims=True)

    def step(b, theta):
        cand = theta | jnp.left_shift(jnp.int32(1), 30 - b)
        return jnp.where(count_ge(cand) >= cap, cand, theta)

    theta = lax.fori_loop(0, 31, step, jnp.zeros((n_exp, 1), jnp.int32))
    need = cap - count_ge(theta + 1)
    theta_ref[...] = jnp.broadcast_to(theta, theta_ref.shape)
    need_ref[...] = jnp.broadcast_to(need, need_ref.shape)


def _route_threshold(aff_t, cap):
    n_exp, n = aff_t.shape
    full = lambda i: (0, 0)
    return pl.pallas_call(
        functools.partial(_route_threshold_body, cap=cap),
        grid=(1,),
        in_specs=[pl.BlockSpec((n_exp, n), full)],
        out_specs=[pl.BlockSpec((n_exp, LANES), full), pl.BlockSpec((n_exp, LANES), full)],
        out_shape=[jax.ShapeDtypeStruct((n_exp, LANES), jnp.int32), jax.ShapeDtypeStruct((n_exp, LANES), F32)],
        compiler_params=_params(("arbitrary",)),
        name="route_threshold",
    )(aff_t)


def _route_prefix_body(aff_ref, tri_ref, theta_ref, need_ref, slot_ref, base_ref, run_sel, run_tie, *, cap):
    @pl.when(pl.program_id(0) == 0)
    def _():
        run_sel[...] = jnp.zeros_like(run_sel)
        run_tie[...] = jnp.zeros_like(run_tie)

    n_exp = aff_ref.shape[0]
    w = ROUTE_CHUNK
    theta = theta_ref[:, 0:1]
    need = need_ref[:, 0:1]
    expert_base = lax.broadcasted_iota(jnp.int32, (n_exp, 1), 0) * cap
    tri = tri_ref[...]
    for k in range(ROUTE_CHUNKS_PER_STEP):
        bits = pltpu.bitcast(aff_ref[:, k * w:(k + 1) * w], jnp.int32)
        tie = bits == theta
        tie_f = jnp.where(tie, 1.0, 0.0)
        tie_incl = jnp.dot(tie_f.astype(BF16), tri, preferred_element_type=F32)
        tie_rank = run_tie[:, 0:1] + tie_incl - tie_f
        sel = (bits > theta) | (tie & (tie_rank < need))
        sel_f = jnp.where(sel, 1.0, 0.0)
        incl = jnp.dot(sel_f.astype(BF16), tri, preferred_element_type=F32)
        rank = run_sel[:, 0:1] + incl - sel_f
        slot_ref[:, k * w:(k + 1) * w] = jnp.where(sel, rank.astype(jnp.int32) + expert_base, -1)
        base_ref[k] = run_sel[...].astype(jnp.int32)
        run_sel[...] = run_sel[...] + jnp.sum(sel_f, axis=1, keepdims=True)
        run_tie[...] = run_tie[...] + jnp.sum(tie_f, axis=1, keepdims=True)


def _route_prefix(aff_t, theta, need, cap):
    n_exp, n = aff_t.shape
    w = ROUTE_CHUNK
    span = w * ROUTE_CHUNKS_PER_STEP
    assert n % span == 0
    tri = (jnp.arange(w)[:, None] <= jnp.arange(w)[None, :]).astype(BF16)
    fixed = lambda i: (0, 0)
    return pl.pallas_call(
        functools.partial(_route_prefix_body, cap=cap),
        grid=(n // span,),
        in_specs=[
            pl.BlockSpec((n_exp, span), lambda i: (0, i)),
            pl.BlockSpec((w, w), fixed),
            pl.BlockSpec((n_exp, LANES), fixed),
            pl.BlockSpec((n_exp, LANES), fixed),
        ],
        out_specs=[
            pl.BlockSpec((n_exp, span), lambda i: (0, i)),
            pl.BlockSpec((ROUTE_CHUNKS_PER_STEP, n_exp, LANES), lambda i: (i, 0, 0)),
        ],
        out_shape=[
            jax.ShapeDtypeStruct((n_exp, n), jnp.int32),
            jax.ShapeDtypeStruct((n // w, n_exp, LANES), jnp.int32),
        ],
        scratch_shapes=[pltpu.VMEM((n_exp, LANES), F32), pltpu.VMEM((n_exp, LANES), F32)],
        compiler_params=_params(("arbitrary",)),
        name="route_prefix",
    )(aff_t, tri, theta, need)


def _route_compact_body(cb_ref, slot_ref, aff_ref, idx_ref, gate_ref, *, cap, nchunks):
    e = pl.program_id(0)
    w = ROUTE_CHUNK
    s = LANES
    slot_iota = lax.broadcasted_iota(jnp.int32, (s, w), 0)
    tok_iota = lax.broadcasted_iota(jnp.int32, (s, w), 1)
    eye = lax.broadcasted_iota(jnp.int32, (s, s), 0) == lax.broadcasted_iota(jnp.int32, (s, s), 1)

    def fold(v):
        out = v[:, 0:s]
        for q in range(1, w // s):
            out = out + v[:, q * s:(q + 1) * s]
        return out

    def to_row(acc):
        col = jnp.sum(acc, axis=1, keepdims=True)
        return jnp.sum(jnp.where(eye, col, 0.0), axis=0, keepdims=True)

    def one_tile(jt, c_first):
        j0 = jt * s
        c_first = lax.while_loop(lambda c: cb_ref[e, c + 1] <= j0, lambda c: c + 1, c_first)
        want = slot_iota + (e * cap + j0)

        def more(state):
            c = state[0]
            return jnp.logical_and(c < nchunks, cb_ref[e, jnp.minimum(c, nchunks - 1)] < j0 + s)

        def chunk(state):
            c, acc_i, acc_g = state
            hit = want == slot_ref[0, pl.ds(c, 1), :]
            tok = (tok_iota + c * w).astype(F32)
            acc_i = acc_i + fold(jnp.where(hit, tok, 0.0))
            acc_g = acc_g + fold(jnp.where(hit, aff_ref[0, pl.ds(c, 1), :], 0.0))
            return c + 1, acc_i, acc_g

        zero = jnp.zeros((s, s), F32)
        _, acc_i, acc_g = lax.while_loop(more, chunk, (c_first, zero, zero))
        idx_ref[0, pl.ds(jt, 1), :] = to_row(acc_i).astype(jnp.int32)
        gate_ref[0, pl.ds(jt, 1), :] = to_row(acc_g)
        return c_first

    lax.fori_loop(0, cap // s, one_tile, jnp.int32(0))


def _route_compact(chunk_base, slot3, aff3, cap):
    n_exp, nchunks, w = slot3.shape
    assert cap % LANES == 0
    per_e = lambda e, cb: (e, 0, 0)
    grid_spec = pltpu.PrefetchScalarGridSpec(
        num_scalar_prefetch=1,
        grid=(n_exp,),
        in_specs=[pl.BlockSpec((1, nchunks, w), per_e), pl.BlockSpec((1, nchunks, w), per_e)],
        out_specs=[pl.BlockSpec((1, cap // LANES, LANES), per_e), pl.BlockSpec((1, cap // LANES, LANES), per_e)],
    )
    return pl.pallas_call(
        functools.partial(_route_compact_body, cap=cap, nchunks=nchunks),
        grid_spec=grid_spec,
        out_shape=[
            jax.ShapeDtypeStruct((n_exp, cap // LANES, LANES), jnp.int32),
            jax.ShapeDtypeStruct((n_exp, cap // LANES, LANES), F32),
        ],
        compiler_params=_params(("arbitrary",)),
        name="route_compact",
    )(chunk_base, slot3, aff3)


def _combine_body(cb_ref, slot_ref, x2_ref, ye_hbm, nw_ref, o_ref, stage, chunk_row, chunk_count, sem,
                  *, cap, ntiles):
    tile = pl.program_id(0)
    cur = tile % 2
    tt = COMBINE_TILE
    ch = COMBINE_DMA_ROWS
    n_exp = slot_ref.shape[0]
    kblock = 256 // ch

    def chunk_copy(row, k, sl):
        return pltpu.make_async_copy(ye_hbm.at[pl.ds(row, ch), :], stage.at[sl, pl.ds(k * ch, ch), :], sem.at[sl])

    def issue(t, sl):
        def per_expert(e, k):
            lo = cb_ref[e, t]
            hi = cb_ref[e, t + 1]
            first = lo // ch
            nch = jnp.where(hi > lo, (hi - 1) // ch - first + 1, 0)

            def per_chunk(i, k):
                row = pl.multiple_of(e * cap + (first + i) * ch, ch)
                chunk_copy(row, k, sl).start()
                chunk_row[sl, k] = row
                return k + 1

            return lax.fori_loop(0, nch, per_chunk, k)

        chunk_count[sl] = lax.fori_loop(0, n_exp, per_expert, jnp.int32(0))

    @pl.when(tile == 0)
    def _():
        issue(0, 0)

    @pl.when(tile + 1 < ntiles)
    def _():
        issue(tile + 1, 1 - cur)

    nchunk = chunk_count[cur]

    def wait_one(k, carry):
        chunk_copy(0, k, cur).wait()
        return carry

    lax.fori_loop(0, nchunk, wait_one, 0)
    nblock = (nchunk + kblock - 1) // kblock

    def pad_one(k, carry):
        stage[cur, pl.ds(pl.multiple_of(k * ch, ch), ch), :] = jnp.zeros((ch, D_MODEL), BF16)
        chunk_row[cur, k] = -(1 << 30)
        return carry

    lax.fori_loop(nchunk, nblock * kblock, pad_one, 0)
    row_iota = lax.broadcasted_iota(jnp.int32, (ch, tt), 0)

    o_ref[...] = x2_ref[...]

    def one_block(b, carry):
        parts = []
        for i in range(kblock):
            row = chunk_row[cur, b * kblock + i]
            e = jnp.clip(row // cap, 0, n_exp - 1)
            hit = (row_iota + row) == slot_ref[pl.ds(e, 1), :]
            parts.append(jnp.where(hit, 1.0, 0.0).astype(BF16))
        onehot_t = jnp.concatenate(parts, axis=0)
        rows = stage[cur, pl.ds(pl.multiple_of(b * 256, 256), 256), :]
        o_ref[...] += lax.dot_general(onehot_t, rows, (((0,), (0,)), ((), ())), preferred_element_type=F32)
        return carry

    lax.fori_loop(0, nblock, one_block, 0)
    x3 = o_ref[...]
    ms = jnp.mean(x3 * x3, axis=-1, keepdims=True)
    o_ref[...] = (x3 * lax.rsqrt(ms + EPS)) * nw_ref[...]


def _combine(tile_base, slot, x2, ye, norm_w, cap):
    n_exp, n = slot.shape
    tt = COMBINE_TILE
    ch = COMBINE_DMA_ROWS
    ntiles = n // tt
    assert cap % ch == 0 and tt % ch == 0
    max_rows = n_exp * (tt + ch)
    max_rows = -(-max_rows // 256) * 256
    grid_spec = pltpu.PrefetchScalarGridSpec(
        num_scalar_prefetch=1,
        grid=(ntiles,),
        in_specs=[
            pl.BlockSpec((n_exp, tt), lambda t, cb: (0, t)),
            pl.BlockSpec((tt, D_MODEL), lambda t, cb: (t, 0)),
            pl.BlockSpec(memory_space=pl.ANY),
            pl.BlockSpec((1, D_MODEL), lambda t, cb: (0, 0)),
        ],
        out_specs=pl.BlockSpec((tt, D_MODEL), lambda t, cb: (t, 0)),
        scratch_shapes=[
            pltpu.VMEM((2, max_rows, D_MODEL), BF16),
            pltpu.SMEM((2, max_rows // ch), jnp.int32),
            pltpu.SMEM((2,), jnp.int32),
            pltpu.SemaphoreType.DMA((2,)),
        ],
    )
    return pl.pallas_call(
        functools.partial(_combine_body, cap=cap, ntiles=ntiles),
        grid_spec=grid_spec,
        out_shape=jax.ShapeDtypeStruct((n, D_MODEL), F32),
        compiler_params=_params(("arbitrary",)),
        name="combine_final",
    )(tile_base, slot, x2, ye, norm_w)


def _trunk(x, p):
    batch, seq = x.shape[0], x.shape[1]
    n = batch * seq
    x2d = x.reshape(n, D_MODEL)
    proj, qd_t, vd_t = _inproj(x2d, p["mix_norm_w"], p["w_tok"], p["w_qd_t"], p["w_vd_t"])
    ya = _na_attention(proj, p["na_bias"], p["na_norm_w2"], batch, seq)
    yb_t = _diff_attention(proj, qd_t, vd_t, p["diff_bias"], p["lam_vecs"], p["subln_col"], batch, seq)
    x2, h2, aff_t = _outproj(x2d, ya, yb_t, p["w_out"], p["ffn_norm_w"], p["w_router_t"])
    cap = CAPACITY_FACTOR * n // N_EXPERTS
    theta, need = _route_threshold(aff_t, cap)
    slot, base = _route_prefix(aff_t, theta, need, cap)
    nchunks = n // ROUTE_CHUNK
    chunk_base = jnp.concatenate([base[:, :, 0].T, jnp.full((N_EXPERTS, 1), cap, jnp.int32)], axis=1)
    idx, gates = _route_compact(chunk_base, slot.reshape(N_EXPERTS, nchunks, ROUTE_CHUNK),
                                aff_t.reshape(N_EXPERTS, nchunks, ROUTE_CHUNK), cap)
    ye = _moe_experts(h2, idx.reshape(N_EXPERTS, cap), gates.reshape(N_EXPERTS, cap),
                      p["w_gate"], p["w_up"], p["w_down"])
    tile_base = chunk_base[:, ::COMBINE_TILE // ROUTE_CHUNK]
    y = _combine(tile_base, slot, x2, ye, p["final_norm_w"], cap)
    return y.reshape(batch, seq, D_MODEL)


def _prepare_params(mix_norm_w, w_in, na_rpb, na_norm_w, t5_table, lambda_q1, lambda_k1, lambda_q2, lambda_k2,
                    subln_w, w_out, ffn_norm_w, w_router, w_gate, w_up, w_down, final_norm_w):
    w = w_in[0]
    qa, ka_va = w[:, 0:W_NA] * NA_HEAD_DIM ** -0.5, w[:, W_NA:3 * W_NA]
    qd = w[:, 3 * W_NA:3 * W_NA + W_DIFF] * DIFF_HALF_DIM ** -0.5
    kd, vd = w[:, 3 * W_NA + W_DIFF:3 * W_NA + 2 * W_DIFF], w[:, 3 * W_NA + 2 * W_DIFF:]
    return {
        "mix_norm_w": mix_norm_w[0].reshape(1, D_MODEL),
        "w_tok": jnp.concatenate([qa, ka_va, kd], axis=1).astype(BF16),
        "w_qd_t": qd.T.astype(BF16),
        "w_vd_t": vd.T.astype(BF16),
        "na_bias": _na_bias_table(na_rpb[0]),
        "na_norm_w2": jnp.tile(na_norm_w[0], 2).reshape(1, LANES),
        "diff_bias": _diff_bias_table(t5_table),
        "lam_vecs": jnp.stack([lambda_q1[0], lambda_k1[0], lambda_q2[0], lambda_k2[0]]).astype(F32),
        "subln_col": subln_w[0].reshape(DIFF_V_DIM, 1),
        "w_out": w_out[0].astype(BF16),
        "ffn_norm_w": ffn_norm_w[0].reshape(1, D_MODEL),
        "w_router_t": w_router[0].T.astype(BF16),
        "w_gate": w_gate[0].astype(BF16),
        "w_up": w_up[0].astype(BF16),
        "w_down": w_down[0].astype(BF16),
        "final_norm_w": final_norm_w.reshape(1, D_MODEL),
    }


def kernel(x_prompt, x_sample, mix_norm_w, w_in, na_rpb, na_norm_w, t5_table, lambda_q1, lambda_k1,
           lambda_q2, lambda_k2, subln_w, w_out, ffn_norm_w, w_router, w_gate, w_up, w_down, final_norm_w):
    p = _prepare_params(mix_norm_w, w_in, na_rpb, na_norm_w, t5_table, lambda_q1, lambda_k1, lambda_q2,
                        lambda_k2, subln_w, w_out, ffn_norm_w, w_router, w_gate, w_up, w_down, final_norm_w)
    return (_trunk(x_prompt, p), _trunk(x_sample, p))
```

```python
import functools
import math

import jax
import jax.numpy as jnp
from jax import lax
from jax.experimental import pallas as pl
from jax.experimental.pallas import tpu as pltpu

F32 = jnp.float32
BF16 = jnp.bfloat16

D_MODEL = 1024
GRID_W = 64
NA_HEADS = 8
NA_HEAD_DIM = 64
NA_KH = 8
NA_KW = 16
DIFF_HEADS = 4
DIFF_HALF_DIM = 64
DIFF_V_DIM = 128
W_NA = 512
W_DIFF = 512
D_IN = 3072
T5_BUCKETS = 32
T5_MAX_DIST = 128
N_EXPERTS = 16
CAPACITY_FACTOR = 2
D_EXPERT = 2816
EPS = 1e-6
LAMBDA_INIT = 0.8 - 0.6 * math.exp(-0.3 * 0)
LOG2E = math.log2(math.e)
TOKEN_MAJOR_COLS = 3 * W_NA + W_DIFF

LANES = 128
TOKEN_TILE = 512
NA_ROWS_PER_STEP = 8
DIFF_BLOCK = 512
DIFF_ONES_ROWS = 16
MOE_TILE = 512
ROUTE_CHUNK = 256
ROUTE_CHUNKS_PER_STEP = 8
COMBINE_TILE = 512
COMBINE_DMA_ROWS = 32
FFN_CHUNKS = ((0, 512), (512, 512), (1024, 512), (1536, 512), (2048, 512), (2560, 256))
MASKED = -1e30
VMEM_LIMIT = 56 * 1024 * 1024

_NT = (((1,), (1,)), ((), ()))


def _params(sem, vmem=VMEM_LIMIT):
    return pltpu.CompilerParams(dimension_semantics=sem, vmem_limit_bytes=vmem)


def _inproj_body(x_ref, nw_ref, w_ref, wqt_ref, wvt_ref, o_ref, qt_ref, vt_ref):
    x = x_ref[...]
    ms = jnp.mean(x * x, axis=-1, keepdims=True)
    h = ((x * lax.rsqrt(ms + EPS)) * nw_ref[...]).astype(BF16)
    for c in range(TOKEN_MAJOR_COLS // W_NA):
        cols = slice(c * W_NA, (c + 1) * W_NA)
        o_ref[:, cols] = jnp.dot(h, w_ref[:, cols], preferred_element_type=F32).astype(BF16)
    qt = lax.dot_general(wqt_ref[...], h, _NT, preferred_element_type=F32)
    qt_ref[0] = (qt * LOG2E).astype(BF16)
    vt_ref[0] = lax.dot_general(wvt_ref[...], h, _NT, preferred_element_type=F32).astype(BF16)


def _inproj(x2d, norm_w, w_tok, w_qd_t, w_vd_t):
    n = x2d.shape[0]
    tm = TOKEN_TILE
    fixed = lambda i: (0, 0)
    return pl.pallas_call(
        _inproj_body,
        grid=(n // tm,),
        in_specs=[
            pl.BlockSpec((tm, D_MODEL), lambda i: (i, 0)),
            pl.BlockSpec((1, D_MODEL), fixed),
            pl.BlockSpec((D_MODEL, TOKEN_MAJOR_COLS), fixed),
            pl.BlockSpec((W_DIFF, D_MODEL), fixed),
            pl.BlockSpec((W_DIFF, D_MODEL), fixed),
        ],
        out_specs=[
            pl.BlockSpec((tm, TOKEN_MAJOR_COLS), lambda i: (i, 0)),
            pl.BlockSpec((1, W_DIFF, tm), lambda i: (i, 0, 0)),
            pl.BlockSpec((1, W_DIFF, tm), lambda i: (i, 0, 0)),
        ],
        out_shape=[
            jax.ShapeDtypeStruct((n, TOKEN_MAJOR_COLS), BF16),
            jax.ShapeDtypeStruct((n // tm, W_DIFF, tm), BF16),
            jax.ShapeDtypeStruct((n // tm, W_DIFF, tm), BF16),
        ],
        compiler_params=_params(("parallel",)),
        name="inproj",
    )(x2d, norm_w, w_tok, w_qd_t, w_vd_t)


def _na_bias_table(rpb):
    c = jnp.arange(GRID_W)
    col_start = jnp.clip(c - NA_KW // 2, 0, GRID_W - NA_KW)
    in_win = (c[None, :] >= col_start[:, None]) & (c[None, :] < col_start[:, None] + NA_KW)
    dc = jnp.clip(c[None, :] - c[:, None], -(NA_KW - 1), NA_KW - 1) + (NA_KW - 1)
    bc = rpb.astype(F32)[:, :, dc]
    bc = jnp.where(in_win[None, None], bc, MASKED)
    tabs = jnp.stack([bc[:, d0:d0 + NA_KH] for d0 in range(NA_KH)])
    tabs = tabs.transpose(0, 1, 3, 2, 4)
    return tabs.reshape(NA_KH, NA_HEADS // 2, 2 * GRID_W, NA_KH * GRID_W)


def _na_body(q_ref, kp_ref, kc_ref, kn_ref, vp_ref, vc_ref, vn_ref, bias_ref, nw_ref, o_ref,
             kbuf, vbuf, *, rows):
    j = pl.program_id(1)
    blk = NA_ROWS_PER_STEP * GRID_W
    kbuf[0:blk] = kp_ref[...]
    kbuf[blk:2 * blk] = kc_ref[...]
    kbuf[2 * blk:3 * blk] = kn_ref[...]
    vbuf[0:blk] = vp_ref[...]
    vbuf[blk:2 * blk] = vc_ref[...]
    vbuf[2 * blk:3 * blk] = vn_ref[...]
    lo = lax.broadcasted_iota(jnp.int32, (GRID_W, LANES), 1) < NA_HEAD_DIM
    nw = nw_ref[...]
    nkeys = NA_KH * GRID_W

    def softmax_parts(s):
        m = jnp.max(s, axis=-1, keepdims=True)
        p = jnp.exp(s - m)
        return p.astype(BF16), jnp.sum(p, axis=-1, keepdims=True)

    def one_row(a, carry):
        r = j * NA_ROWS_PER_STEP + a
        row_start = jnp.clip(r - NA_KH // 2, 0, rows - NA_KH)
        d0 = row_start - r + (NA_KH - 1)
        koff = pl.multiple_of((row_start - (j - 1) * NA_ROWS_PER_STEP) * GRID_W, GRID_W)
        qoff = pl.multiple_of(a * GRID_W, GRID_W)
        pairs = range(NA_HEADS // 2)
        cols = [slice(hp * LANES, (hp + 1) * LANES) for hp in pairs]
        scores = []
        for hp in pairs:
            qp = q_ref[pl.ds(qoff, GRID_W), cols[hp]]
            zero = jnp.zeros_like(qp)
            q2 = jnp.concatenate([jnp.where(lo, qp, zero), jnp.where(lo, zero, qp)], axis=0)
            kk = kbuf[pl.ds(koff, nkeys), cols[hp]]
            scores.append(lax.dot_general(q2, kk, _NT, preferred_element_type=F32) + bias_ref[d0, hp])
        probs = [softmax_parts(s) for s in scores]
        outs = [jnp.dot(p, vbuf[pl.ds(koff, nkeys), cols[hp]], preferred_element_type=F32) / l
                for hp, (p, l) in zip(pairs, probs)]
        for hp in pairs:
            cs = cols[hp]
            o = jnp.where(lo, outs[hp][0:GRID_W], outs[hp][GRID_W:2 * GRID_W])
            sq = o * o
            msa = jnp.sum(jnp.where(lo, sq, 0.0), axis=-1, keepdims=True) * (1.0 / NA_HEAD_DIM)
            msb = jnp.sum(jnp.where(lo, 0.0, sq), axis=-1, keepdims=True) * (1.0 / NA_HEAD_DIM)
            inv = jnp.where(lo, lax.rsqrt(msa + EPS), lax.rsqrt(msb + EPS))
            o_ref[pl.ds(qoff, GRID_W), cs] = ((o * inv) * nw).astype(BF16)
        return carry

    lax.fori_loop(0, NA_ROWS_PER_STEP, one_row, 0, unroll=4)


def _na_attention(proj, bias_tab, norm_w2, batch, seq):
    rows = seq // GRID_W
    assert rows % NA_ROWS_PER_STEP == 0 and rows >= NA_KH
    nb = rows // NA_ROWS_PER_STEP
    blk = NA_ROWS_PER_STEP * GRID_W
    n = batch * seq

    def at(col, shift):
        return pl.BlockSpec(
            (blk, W_NA), lambda b, j: (b * nb + jnp.clip(j + shift, 0, nb - 1), col))

    return pl.pallas_call(
        functools.partial(_na_body, rows=rows),
        grid=(batch, nb),
        in_specs=[
            at(0, 0),
            at(1, -1), at(1, 0), at(1, 1),
            at(2, -1), at(2, 0), at(2, 1),
            pl.BlockSpec((NA_KH, NA_HEADS // 2, 2 * GRID_W, NA_KH * GRID_W), lambda b, j: (0, 0, 0, 0)),
            pl.BlockSpec((1, LANES), lambda b, j: (0, 0)),
        ],
        out_specs=pl.BlockSpec((blk, W_NA), lambda b, j: (b * nb + j, 0)),
        out_shape=jax.ShapeDtypeStruct((n, W_NA), BF16),
        scratch_shapes=[pltpu.VMEM((3 * blk, W_NA), BF16), pltpu.VMEM((3 * blk, W_NA), BF16)],
        compiler_params=_params(("parallel", "parallel")),
        name="na_attention",
    )(proj, proj, proj, proj, proj, proj, proj, bias_tab, norm_w2)


def _t5_bucket(rel):
    nb = T5_BUCKETS // 2
    max_exact = nb // 2
    n = jnp.abs(rel)
    large = max_exact + (jnp.log(jnp.maximum(n, 1).astype(F32) / max_exact)
                         / math.log(T5_MAX_DIST / max_exact) * (nb - max_exact)).astype(jnp.int32)
    large = jnp.minimum(large, nb - 1)
    return jnp.where(rel > 0, nb, 0) + jnp.where(n < max_exact, n, large)


def _diff_bias_table(t5_table):
    x = DIFF_BLOCK
    assert x + 1 >= T5_MAX_DIST
    span = 2 * x
    rel = (x - 1) - jnp.arange(span)[None, :] + jnp.arange(-2, 3)[:, None] * x
    diag = (t5_table.astype(F32) * LOG2E)[_t5_bucket(rel)].transpose(2, 0, 1)
    skew = jnp.tile(diag, (1, 1, x))[:, :, :x * (span - 1)].reshape(DIFF_HEADS, 5, x, span - 1)
    return skew[:, :, :, x - 1:2 * x - 1]


def _diff_body(qt_ref, k_ref, vt_ref, bias_ref, lam_ref, sw_ref, o_ref, s_even, s_odd, *, nblk):
    i = pl.program_id(2)
    x = DIFF_BLOCK
    qt = qt_ref[0]
    first = lax.broadcasted_iota(jnp.int32, (LANES, x), 0) < DIFF_HALF_DIM
    zero = jnp.zeros_like(qt)
    qt_both = jnp.concatenate([jnp.where(first, qt, zero), jnp.where(first, zero, qt)], axis=1)

    def scores(j, s_ref):
        kc = k_ref[pl.ds(pl.multiple_of(j * x, x), x), :]
        s_ref[...] = jnp.dot(kc, qt_both, preferred_element_type=F32)

    ones_rows = jnp.ones((DIFF_ONES_ROWS, x), BF16)

    def consume(j, s_ref, carry, far_bias):
        vt_ext = jnp.concatenate([vt_ref[j], ones_rows], axis=0)
        out = []
        for c in range(2):
            m, acc = carry[2 * c:2 * c + 2]
            s = s_ref[:, c * x:(c + 1) * x]
            if far_bias is None:
                s = s + bias_ref[0, jnp.clip(j - i, -2, 2) + 2]
                shift = 0.0
            else:
                shift = far_bias
            m_in = m - shift
            mn = jnp.maximum(m_in, jnp.max(s, axis=0, keepdims=True))
            p = jnp.exp2(s - mn).astype(BF16)
            acc = jnp.exp2(m_in - mn) * acc + jnp.dot(vt_ext, p, preferred_element_type=F32)
            out += [mn + shift, acc]
        return tuple(out)

    def pair_loop(lo, hi, far_bias, carry):
        def body(jj, carry):
            j = 2 * jj
            scores(j + 1, s_odd)
            carry = consume(j, s_even, carry, far_bias)
            scores(j + 2, s_even)
            return consume(j + 1, s_odd, carry, far_bias)
        return lax.fori_loop(lo, hi, body, carry)

    def last_pair(far_bias, carry):
        scores(nblk - 1, s_odd)
        carry = consume(nblk - 2, s_even, carry, far_bias)
        return consume(nblk - 1, s_odd, carry, far_bias)

    npairs = nblk // 2
    near_lo = jnp.maximum((i - 1) // 2, 0)
    near_hi = jnp.minimum((i + 1) // 2, npairs - 1)
    bias_before = bias_ref[0, 0, 0:1, 0:1]
    bias_after = bias_ref[0, 4, 0:1, 0:1]
    init = (jnp.full((1, x), -jnp.inf, F32), jnp.zeros((DIFF_V_DIM + DIFF_ONES_ROWS, x), F32)) * 2
    scores(0, s_even)
    carry = pair_loop(0, jnp.minimum(near_lo, npairs - 1), bias_before, init)
    carry = pair_loop(near_lo, jnp.minimum(near_hi + 1, npairs - 1), None, carry)
    carry = pair_loop(near_hi + 1, npairs - 1, bias_after, carry)
    _, ext1, _, ext2 = lax.cond(near_hi == npairs - 1,
                                functools.partial(last_pair, None),
                                functools.partial(last_pair, bias_after), carry)
    acc1, l1 = ext1[0:DIFF_V_DIM], ext1[DIFF_V_DIM:DIFF_V_DIM + 1]
    acc2, l2 = ext2[0:DIFF_V_DIM], ext2[DIFF_V_DIM:DIFF_V_DIM + 1]
    lam = (jnp.exp(jnp.sum(lam_ref[0:1, :] * lam_ref[1:2, :], axis=-1, keepdims=True))
           - jnp.exp(jnp.sum(lam_ref[2:3, :] * lam_ref[3:4, :], axis=-1, keepdims=True)) + LAMBDA_INIT)
    o = acc1 / l1 - lam * (acc2 / l2)
    ms = jnp.mean(o * o, axis=0, keepdims=True)
    y = ((o * lax.rsqrt(ms + EPS)) * sw_ref[...]) * (1.0 - LAMBDA_INIT)
    o_ref[0] = y.astype(BF16)


def _diff_attention(proj, qd_t, vd_t, bias_tab, lam_vecs, subln_col, batch, seq):
    x = DIFF_BLOCK
    assert seq % (2 * x) == 0 and x == TOKEN_TILE
    nblk = seq // x
    n = batch * seq
    kcol = 3 * W_NA // LANES
    return pl.pallas_call(
        functools.partial(_diff_body, nblk=nblk),
        grid=(batch, DIFF_HEADS, nblk),
        in_specs=[
            pl.BlockSpec((1, LANES, x), lambda b, h, i: (b * nblk + i, h, 0)),
            pl.BlockSpec((seq, LANES), lambda b, h, i: (b, kcol + h)),
            pl.BlockSpec((nblk, LANES, x), lambda b, h, i: (b, h, 0)),
            pl.BlockSpec((1, 5, x, x), lambda b, h, i: (h, 0, 0, 0)),
            pl.BlockSpec((4, DIFF_HALF_DIM), lambda b, h, i: (0, 0)),
            pl.BlockSpec((DIFF_V_DIM, 1), lambda b, h, i: (0, 0)),
        ],
        out_specs=pl.BlockSpec((1, LANES, x), lambda b, h, i: (b * nblk + i, h, 0)),
        out_shape=jax.ShapeDtypeStruct((n // x, W_DIFF, x), BF16),
        scratch_shapes=[pltpu.VMEM((x, 2 * x), F32), pltpu.VMEM((x, 2 * x), F32)],
        compiler_params=_params(("parallel", "parallel", "parallel")),
        name="diff_attention",
    )(qd_t, proj, vd_t, bias_tab, lam_vecs, subln_col)


def _outproj_body(x_ref, ya_ref, yb_ref, wa_ref, wb_ref, nw_ref, wr_ref, x2_ref, h2_ref, aff_ref):
    attn = (jnp.dot(ya_ref[...], wa_ref[...], preferred_element_type=F32)
            + lax.dot_general(yb_ref[0], wb_ref[...], (((0,), (0,)), ((), ())), preferred_element_type=F32))
    x2 = x_ref[...] + attn
    x2_ref[...] = x2
    ms = jnp.mean(x2 * x2, axis=-1, keepdims=True)
    h2 = (x2 * lax.rsqrt(ms + EPS)) * nw_ref[...]
    for s in range(D_MODEL // LANES):
        h2_ref[:, s, :] = h2[:, s * LANES:(s + 1) * LANES]
    logits = lax.dot_general(wr_ref[...], h2.astype(BF16), _NT, preferred_element_type=F32)
    e = jnp.exp(logits - jnp.max(logits, axis=0, keepdims=True))
    aff_ref[...] = e / jnp.sum(e, axis=0, keepdims=True)


def _outproj(x2d, ya, yb, w_out, norm_w, w_router_t):
    n = x2d.shape[0]
    tm = TOKEN_TILE
    row = lambda i: (i, 0)
    fixed = lambda i: (0, 0)
    return pl.pallas_call(
        _outproj_body,
        grid=(n // tm,),
        in_specs=[
            pl.BlockSpec((tm, D_MODEL), row),
            pl.BlockSpec((tm, W_NA), row),
            pl.BlockSpec((1, W_DIFF, tm), lambda i: (i, 0, 0)),
            pl.BlockSpec((W_NA, D_MODEL), lambda i: (0, 0)),
            pl.BlockSpec((W_DIFF, D_MODEL), lambda i: (1, 0)),
            pl.BlockSpec((1, D_MODEL), fixed),
            pl.BlockSpec((N_EXPERTS, D_MODEL), fixed),
        ],
        out_specs=[
            pl.BlockSpec((tm, D_MODEL), row),
            pl.BlockSpec((tm, D_MODEL // LANES, LANES), lambda i: (i, 0, 0)),
            pl.BlockSpec((N_EXPERTS, tm), lambda i: (0, i)),
        ],
        out_shape=[
            jax.ShapeDtypeStruct((n, D_MODEL), F32),
            jax.ShapeDtypeStruct((n, D_MODEL // LANES, LANES), F32),
            jax.ShapeDtypeStruct((N_EXPERTS, n), F32),
        ],
        compiler_params=_params(("parallel",)),
        name="outproj_router",
    )(x2d, ya, yb, w_out, w_out, norm_w, w_router_t)


def _moe_body(idx_ref, idx_next_ref, h2_hbm, g_ref, wg_ref, wu_ref, wd_ref, o_ref, x_even, x_odd, sem, *, tm):
    nt = pl.num_programs(1)
    step = pl.program_id(0) * nt + pl.program_id(1)
    total = pl.num_programs(0) * nt

    def row_copy(tok, i, buf, sl):
        return pltpu.make_async_copy(h2_hbm.at[pl.ds(tok, 1)], buf.at[pl.ds(i, 1)], sem.at[sl])

    def wait_tile(buf, sl):
        pltpu.make_async_copy(h2_hbm.at[pl.ds(0, tm)], buf, sem.at[sl]).wait()

    @pl.when(step == 0)
    def _():
        def body(i, carry):
            row_copy(idx_ref[0, 0, i], i, x_even, 0).start()
            return carry
        lax.fori_loop(0, tm, body, 0, unroll=8)

    def run(cur, cur_sl, nxt, nxt_sl):
        wait_tile(cur, cur_sl)
        for i in range(tm):
            row_copy(idx_next_ref[0, 0, i], i, nxt, nxt_sl).start()
        x = jnp.concatenate([cur[:, s, :] for s in range(D_MODEL // LANES)], axis=1).astype(BF16)
        acc = jnp.zeros((tm, D_MODEL), F32)
        for f0, fc in FFN_CHUNKS:
            g = jnp.dot(x, wg_ref[0, :, f0:f0 + fc], preferred_element_type=F32)
            u = jnp.dot(x, wu_ref[0, :, f0:f0 + fc], preferred_element_type=F32)
            h = (jax.nn.silu(g) * u).astype(BF16)
            acc = acc + jnp.dot(h, wd_ref[0, f0:f0 + fc, :], preferred_element_type=F32)
        o_ref[...] = (acc * g_ref[...]).astype(o_ref.dtype)

        @pl.when(step == total - 1)
        def _():
            wait_tile(nxt, nxt_sl)

    @pl.when(step % 2 == 0)
    def _():
        run(x_even, 0, x_odd, 1)

    @pl.when(step % 2 == 1)
    def _():
        run(x_odd, 1, x_even, 0)


def _moe_experts(h2, idx, gates, w_gate, w_up, w_down):
    n_exp, cap = idx.shape
    tm = min(MOE_TILE, cap)
    assert cap % tm == 0
    nt = cap // tm
    idx3 = idx.reshape(n_exp * nt, 1, tm)

    def tile_ahead(k):
        return pl.BlockSpec((1, 1, tm), lambda e, t: (jnp.minimum(e * nt + t + k, n_exp * nt - 1), 0, 0),
                            memory_space=pltpu.SMEM)

    tile_shape = (tm, D_MODEL // LANES, LANES)
    return pl.pallas_call(
        functools.partial(_moe_body, tm=tm),
        grid=(n_exp, nt),
        in_specs=[
            tile_ahead(0), tile_ahead(1),
            pl.BlockSpec(memory_space=pl.ANY),
            pl.BlockSpec((tm, 1), lambda e, t: (e * nt + t, 0)),
            pl.BlockSpec((1, D_MODEL, D_EXPERT), lambda e, t: (e, 0, 0)),
            pl.BlockSpec((1, D_MODEL, D_EXPERT), lambda e, t: (e, 0, 0)),
            pl.BlockSpec((1, D_EXPERT, D_MODEL), lambda e, t: (e, 0, 0)),
        ],
        out_specs=pl.BlockSpec((tm, D_MODEL), lambda e, t: (e * nt + t, 0)),
        out_shape=jax.ShapeDtypeStruct((n_exp * cap, D_MODEL), BF16),
        scratch_shapes=[pltpu.VMEM(tile_shape, F32), pltpu.VMEM(tile_shape, F32), pltpu.SemaphoreType.DMA((2,))],
        compiler_params=_params(("arbitrary", "arbitrary")),
        name="moe_experts",
    )(idx3, idx3, h2, gates.reshape(n_exp * cap, 1), w_gate, w_up, w_down)


def _route_threshold_body(aff_ref, theta_ref, need_ref, *, cap):
    n_exp = aff_ref.shape[0]

    def count_ge(cand):
        bits = pltpu.bitcast(aff_ref[...], jnp.int32)
        return jnp.sum(jnp.where(bits >= cand, 1.0, 0.0), axis=1, keepdims=True)

    def step(b, theta):
        cand = theta | jnp.left_shift(jnp.int32(1), 30 - b)
        return jnp.where(count_ge(cand) >= cap, cand, theta)

    theta = lax.fori_loop(0, 31, step, jnp.zeros((n_exp, 1), jnp.int32))
    need = cap - count_ge(theta + 1)
    theta_ref[...] = jnp.broadcast_to(theta, theta_ref.shape)
    need_ref[...] = jnp.broadcast_to(need, need_ref.shape)


def _route_threshold(aff_t, cap):
    n_exp, n = aff_t.shape
    full = lambda i: (0, 0)
    return pl.pallas_call(
        functools.partial(_route_threshold_body, cap=cap),
        grid=(1,),
        in_specs=[pl.BlockSpec((n_exp, n), full)],
        out_specs=[pl.BlockSpec((n_exp, LANES), full), pl.BlockSpec((n_exp, LANES), full)],
        out_shape=[jax.ShapeDtypeStruct((n_exp, LANES), jnp.int32), jax.ShapeDtypeStruct((n_exp, LANES), F32)],
        compiler_params=_params(("arbitrary",)),
        name="route_threshold",
    )(aff_t)


def _route_prefix_body(aff_ref, tri_ref, theta_ref, need_ref, slot_ref, base_ref, run_sel, run_tie, *, cap):
    @pl.when(pl.program_id(0) == 0)
    def _():
        run_sel[...] = jnp.zeros_like(run_sel)
        run_tie[...] = jnp.zeros_like(run_tie)

    n_exp = aff_ref.shape[0]
    w = ROUTE_CHUNK
    theta = theta_ref[:, 0:1]
    need = need_ref[:, 0:1]
    expert_base = lax.broadcasted_iota(jnp.int32, (n_exp, 1), 0) * cap
    tri = tri_ref[...]
    for k in range(ROUTE_CHUNKS_PER_STEP):
        bits = pltpu.bitcast(aff_ref[:, k * w:(k + 1) * w], jnp.int32)
        tie = bits == theta
        tie_f = jnp.where(tie, 1.0, 0.0)
        tie_incl = jnp.dot(tie_f.astype(BF16), tri, preferred_element_type=F32)
        tie_rank = run_tie[:, 0:1] + tie_incl - tie_f
        sel = (bits > theta) | (tie & (tie_rank < need))
        sel_f = jnp.where(sel, 1.0, 0.0)
        incl = jnp.dot(sel_f.astype(BF16), tri, preferred_element_type=F32)
        rank = run_sel[:, 0:1] + incl - sel_f
        slot_ref[:, k * w:(k + 1) * w] = jnp.where(sel, rank.astype(jnp.int32) + expert_base, -1)
        base_ref[k] = run_sel[...].astype(jnp.int32)
        run_sel[...] = run_sel[...] + jnp.sum(sel_f, axis=1, keepdims=True)
        run_tie[...] = run_tie[...] + jnp.sum(tie_f, axis=1, keepdims=True)


def _route_prefix(aff_t, theta, need, cap):
    n_exp, n = aff_t.shape
    w = ROUTE_CHUNK
    span = w * ROUTE_CHUNKS_PER_STEP
    assert n % span == 0
    tri = (jnp.arange(w)[:, None] <= jnp.arange(w)[None, :]).astype(BF16)
    fixed = lambda i: (0, 0)
    return pl.pallas_call(
        functools.partial(_route_prefix_body, cap=cap),
        grid=(n // span,),
        in_specs=[
            pl.BlockSpec((n_exp, span), lambda i: (0, i)),
            pl.BlockSpec((w, w), fixed),
            pl.BlockSpec((n_exp, LANES), fixed),
            pl.BlockSpec((n_exp, LANES), fixed),
        ],
        out_specs=[
            pl.BlockSpec((n_exp, span), lambda i: (0, i)),
            pl.BlockSpec((ROUTE_CHUNKS_PER_STEP, n_exp, LANES), lambda i: (i, 0, 0)),
        ],
        out_shape=[
            jax.ShapeDtypeStruct((n_exp, n), jnp.int32),
            jax.ShapeDtypeStruct((n // w, n_exp, LANES), jnp.int32),
        ],
        scratch_shapes=[pltpu.VMEM((n_exp, LANES), F32), pltpu.VMEM((n_exp, LANES), F32)],
        compiler_params=_params(("arbitrary",)),
        name="route_prefix",
    )(aff_t, tri, theta, need)


def _route_compact_body(cb_ref, slot_ref, aff_ref, idx_ref, gate_ref, *, cap, nchunks):
    e = pl.program_id(0)
    w = ROUTE_CHUNK
    s = LANES
    slot_iota = lax.broadcasted_iota(jnp.int32, (s, w), 0)
    tok_iota = lax.broadcasted_iota(jnp.int32, (s, w), 1)
    eye = lax.broadcasted_iota(jnp.int32, (s, s), 0) == lax.broadcasted_iota(jnp.int32, (s, s), 1)

    def fold(v):
        out = v[:, 0:s]
        for q in range(1, w // s):
            out = out + v[:, q * s:(q + 1) * s]
        return out

    def to_row(acc):
        col = jnp.sum(acc, axis=1, keepdims=True)
        return jnp.sum(jnp.where(eye, col, 0.0), axis=0, keepdims=True)

    def one_tile(jt, c_first):
        j0 = jt * s
        c_first = lax.while_loop(lambda c: cb_ref[e, c + 1] <= j0, lambda c: c + 1, c_first)
        want = slot_iota + (e * cap + j0)

        def more(state):
            c = state[0]
            return jnp.logical_and(c < nchunks, cb_ref[e, jnp.minimum(c, nchunks - 1)] < j0 + s)

        def chunk(state):
            c, acc_i, acc_g = state
            hit = want == slot_ref[0, pl.ds(c, 1), :]
            tok = (tok_iota + c * w).astype(F32)
            acc_i = acc_i + fold(jnp.where(hit, tok, 0.0))
            acc_g = acc_g + fold(jnp.where(hit, aff_ref[0, pl.ds(c, 1), :], 0.0))
            return c + 1, acc_i, acc_g

        zero = jnp.zeros((s, s), F32)
        _, acc_i, acc_g = lax.while_loop(more, chunk, (c_first, zero, zero))
        idx_ref[0, pl.ds(jt, 1), :] = to_row(acc_i).astype(jnp.int32)
        gate_ref[0, pl.ds(jt, 1), :] = to_row(acc_g)
        return c_first

    lax.fori_loop(0, cap // s, one_tile, jnp.int32(0))


def _route_compact(chunk_base, slot3, aff3, cap):
    n_exp, nchunks, w = slot3.shape
    assert cap % LANES == 0
    per_e = lambda e, cb: (e, 0, 0)
    grid_spec = pltpu.PrefetchScalarGridSpec(
        num_scalar_prefetch=1,
        grid=(n_exp,),
        in_specs=[pl.BlockSpec((1, nchunks, w), per_e), pl.BlockSpec((1, nchunks, w), per_e)],
        out_specs=[pl.BlockSpec((1, cap // LANES, LANES), per_e), pl.BlockSpec((1, cap // LANES, LANES), per_e)],
    )
    return pl.pallas_call(
        functools.partial(_route_compact_body, cap=cap, nchunks=nchunks),
        grid_spec=grid_spec,
        out_shape=[
            jax.ShapeDtypeStruct((n_exp, cap // LANES, LANES), jnp.int32),
            jax.ShapeDtypeStruct((n_exp, cap // LANES, LANES), F32),
        ],
        compiler_params=_params(("arbitrary",)),
        name="route_compact",
    )(chunk_base, slot3, aff3)


def _combine_body(cb_ref, slot_ref, x2_ref, ye_hbm, nw_ref, o_ref, stage, chunk_row, chunk_count, sem,
                  *, cap, ntiles):
    tile = pl.program_id(0)
    cur = tile % 2
    tt = COMBINE_TILE
    ch = COMBINE_DMA_ROWS
    n_exp = slot_ref.shape[0]
    kblock = 256 // ch

    def chunk_copy(row, k, sl):
        return pltpu.make_async_copy(ye_hbm.at[pl.ds(row, ch), :], stage.at[sl, pl.ds(k * ch, ch), :], sem.at[sl])

    def issue(t, sl):
        def per_expert(e, k):
            lo = cb_ref[e, t]
            hi = cb_ref[e, t + 1]
            first = lo // ch
            nch = jnp.where(hi > lo, (hi - 1) // ch - first + 1, 0)

            def per_chunk(i, k):
                row = pl.multiple_of(e * cap + (first + i) * ch, ch)
                chunk_copy(row, k, sl).start()
                chunk_row[sl, k] = row
                return k + 1

            return lax.fori_loop(0, nch, per_chunk, k)

        chunk_count[sl] = lax.fori_loop(0, n_exp, per_expert, jnp.int32(0))

    @pl.when(tile == 0)
    def _():
        issue(0, 0)

    @pl.when(tile + 1 < ntiles)
    def _():
        issue(tile + 1, 1 - cur)

    nchunk = chunk_count[cur]

    def wait_one(k, carry):
        chunk_copy(0, k, cur).wait()
        return carry

    lax.fori_loop(0, nchunk, wait_one, 0)
    nblock = (nchunk + kblock - 1) // kblock

    def pad_one(k, carry):
        stage[cur, pl.ds(pl.multiple_of(k * ch, ch), ch), :] = jnp.zeros((ch, D_MODEL), BF16)
        chunk_row[cur, k] = -(1 << 30)
        return carry

    lax.fori_loop(nchunk, nblock * kblock, pad_one, 0)
    row_iota = lax.broadcasted_iota(jnp.int32, (ch, tt), 0)

    o_ref[...] = x2_ref[...]

    def one_block(b, carry):
        parts = []
        for i in range(kblock):
            row = chunk_row[cur, b * kblock + i]
            e = jnp.clip(row // cap, 0, n_exp - 1)
            hit = (row_iota + row) == slot_ref[pl.ds(e, 1), :]
            parts.append(jnp.where(hit, 1.0, 0.0).astype(BF16))
        onehot_t = jnp.concatenate(parts, axis=0)
        rows = stage[cur, pl.ds(pl.multiple_of(b * 256, 256), 256), :]
        o_ref[...] += lax.dot_general(onehot_t, rows, (((0,), (0,)), ((), ())), preferred_element_type=F32)
        return carry

    lax.fori_loop(0, nblock, one_block, 0)
    x3 = o_ref[...]
    ms = jnp.mean(x3 * x3, axis=-1, keepdims=True)
    o_ref[...] = (x3 * lax.rsqrt(ms + EPS)) * nw_ref[...]


def _combine(tile_base, slot, x2, ye, norm_w, cap):
    n_exp, n = slot.shape
    tt = COMBINE_TILE
    ch = COMBINE_DMA_ROWS
    ntiles = n // tt
    assert cap % ch == 0 and tt % ch == 0
    max_rows = n_exp * (tt + ch)
    max_rows = -(-max_rows // 256) * 256
    grid_spec = pltpu.PrefetchScalarGridSpec(
        num_scalar_prefetch=1,
        grid=(ntiles,),
        in_specs=[
            pl.BlockSpec((n_exp, tt), lambda t, cb: (0, t)),
            pl.BlockSpec((tt, D_MODEL), lambda t, cb: (t, 0)),
            pl.BlockSpec(memory_space=pl.ANY),
            pl.BlockSpec((1, D_MODEL), lambda t, cb: (0, 0)),
        ],
        out_specs=pl.BlockSpec((tt, D_MODEL), lambda t, cb: (t, 0)),
        scratch_shapes=[
            pltpu.VMEM((2, max_rows, D_MODEL), BF16),
            pltpu.SMEM((2, max_rows // ch), jnp.int32),
            pltpu.SMEM((2,), jnp.int32),
            pltpu.SemaphoreType.DMA((2,)),
        ],
    )
    return pl.pallas_call(
        functools.partial(_combine_body, cap=cap, ntiles=ntiles),
        grid_spec=grid_spec,
        out_shape=jax.ShapeDtypeStruct((n, D_MODEL), F32),
        compiler_params=_params(("arbitrary",)),
        name="combine_final",
    )(tile_base, slot, x2, ye, norm_w)


def _trunk(x, p):
    batch, seq = x.shape[0], x.shape[1]
    n = batch * seq
    x2d = x.reshape(n, D_MODEL)
    proj, qd_t, vd_t = _inproj(x2d, p["mix_norm_w"], p["w_tok"], p["w_qd_t"], p["w_vd_t"])
    ya = _na_attention(proj, p["na_bias"], p["na_norm_w2"], batch, seq)
    yb_t = _diff_attention(proj, qd_t, vd_t, p["diff_bias"], p["lam_vecs"], p["subln_col"], batch, seq)
    x2, h2, aff_t = _outproj(x2d, ya, yb_t, p["w_out"], p["ffn_norm_w"], p["w_router_t"])
    cap = CAPACITY_FACTOR * n // N_EXPERTS
    theta, need = _route_threshold(aff_t, cap)
    slot, base = _route_prefix(aff_t, theta, need, cap)
    nchunks = n // ROUTE_CHUNK
    chunk_base = jnp.concatenate([base[:, :, 0].T, jnp.full((N_EXPERTS, 1), cap, jnp.int32)], axis=1)
    idx, gates = _route_compact(chunk_base, slot.reshape(N_EXPERTS, nchunks, ROUTE_CHUNK),
                                aff_t.reshape(N_EXPERTS, nchunks, ROUTE_CHUNK), cap)
    ye = _moe_experts(h2, idx.reshape(N_EXPERTS, cap), gates.reshape(N_EXPERTS, cap),
                      p["w_gate"], p["w_up"], p["w_down"])
    tile_base = chunk_base[:, ::COMBINE_TILE // ROUTE_CHUNK]
    y = _combine(tile_base, slot, x2, ye, p["final_norm_w"], cap)
    return y.reshape(batch, seq, D_MODEL)


def _prepare_params(mix_norm_w, w_in, na_rpb, na_norm_w, t5_table, lambda_q1, lambda_k1, lambda_q2, lambda_k2,
                    subln_w, w_out, ffn_norm_w, w_router, w_gate, w_up, w_down, final_norm_w):
    w = w_in[0]
    qa, ka_va = w[:, 0:W_NA] * NA_HEAD_DIM ** -0.5, w[:, W_NA:3 * W_NA]
    qd = w[:, 3 * W_NA:3 * W_NA + W_DIFF] * DIFF_HALF_DIM ** -0.5
    kd, vd = w[:, 3 * W_NA + W_DIFF:3 * W_NA + 2 * W_DIFF], w[:, 3 * W_NA + 2 * W_DIFF:]
    return {
        "mix_norm_w": mix_norm_w[0].reshape(1, D_MODEL),
        "w_tok": jnp.concatenate([qa, ka_va, kd], axis=1).astype(BF16),
        "w_qd_t": qd.T.astype(BF16),
        "w_vd_t": vd.T.astype(BF16),
        "na_bias": _na_bias_table(na_rpb[0]),
        "na_norm_w2": jnp.tile(na_norm_w[0], 2).reshape(1, LANES),
        "diff_bias": _diff_bias_table(t5_table),
        "lam_vecs": jnp.stack([lambda_q1[0], lambda_k1[0], lambda_q2[0], lambda_k2[0]]).astype(F32),
        "subln_col": subln_w[0].reshape(DIFF_V_DIM, 1),
        "w_out": w_out[0].astype(BF16),
        "ffn_norm_w": ffn_norm_w[0].reshape(1, D_MODEL),
        "w_router_t": w_router[0].T.astype(BF16),
        "w_gate": w_gate[0].astype(BF16),
        "w_up": w_up[0].astype(BF16),
        "w_down": w_down[0].astype(BF16),
        "final_norm_w": final_norm_w.reshape(1, D_MODEL),
    }


def kernel(x_prompt, x_sample, mix_norm_w, w_in, na_rpb, na_norm_w, t5_table, lambda_q1, lambda_k1,
           lambda_q2, lambda_k2, subln_w, w_out, ffn_norm_w, w_router, w_gate, w_up, w_down, final_norm_w):
    p = _prepare_params(mix_norm_w, w_in, na_rpb, na_norm_w, t5_table, lambda_q1, lambda_k1, lambda_q2,
                        lambda_k2, subln_w, w_out, ffn_norm_w, w_router, w_gate, w_up, w_down, final_norm_w)
    return (_trunk(x_prompt, p), _trunk(x_sample, p))
```

```python
import functools
import math

import jax
import jax.numpy as jnp
from jax import lax
from jax.experimental import pallas as pl
from jax.experimental.pallas import tpu as pltpu

F32 = jnp.float32
BF16 = jnp.bfloat16

D_MODEL = 1024
GRID_W = 64
NA_HEADS = 8
NA_HEAD_DIM = 64
NA_KH = 8
NA_KW = 16
DIFF_HEADS = 4
DIFF_HALF_DIM = 64
DIFF_V_DIM = 128
W_NA = 512
W_DIFF = 512
D_IN = 3072
T5_BUCKETS = 32
T5_MAX_DIST = 128
N_EXPERTS = 16
CAPACITY_FACTOR = 2
D_EXPERT = 2816
EPS = 1e-6
LAMBDA_INIT = 0.8 - 0.6 * math.exp(-0.3 * 0)
LOG2E = math.log2(math.e)
TOKEN_MAJOR_COLS = 3 * W_NA + W_DIFF

LANES = 128
TOKEN_TILE = 512
NA_ROWS_PER_STEP = 8
DIFF_BLOCK = 512
DIFF_ONES_ROWS = 16
DIFF_SPLIT_MIN_PAIRS = 4
MOE_TILE = 512
ROUTE_CHUNK = 256
ROUTE_CHUNKS_PER_STEP = 8
COMBINE_TILE = 512
COMBINE_DMA_ROWS = 32
FFN_CHUNKS = ((0, 512), (512, 512), (1024, 512), (1536, 512), (2048, 512), (2560, 256))
MASKED = -1e30
VMEM_LIMIT = 56 * 1024 * 1024

_NT = (((1,), (1,)), ((), ()))


def _params(sem, vmem=VMEM_LIMIT):
    return pltpu.CompilerParams(dimension_semantics=sem, vmem_limit_bytes=vmem)


def _inproj_body(x_ref, nw_ref, w_ref, wqt_ref, wvt_ref, o_ref, qt_ref, vt_ref):
    x = x_ref[...]
    ms = jnp.mean(x * x, axis=-1, keepdims=True)
    h = ((x * lax.rsqrt(ms + EPS)) * nw_ref[...]).astype(BF16)
    for c in range(TOKEN_MAJOR_COLS // W_NA):
        cols = slice(c * W_NA, (c + 1) * W_NA)
        o_ref[:, cols] = jnp.dot(h, w_ref[:, cols], preferred_element_type=F32).astype(BF16)
    qt = lax.dot_general(wqt_ref[...], h, _NT, preferred_element_type=F32)
    qt_ref[0] = (qt * LOG2E).astype(BF16)
    vt_ref[0] = lax.dot_general(wvt_ref[...], h, _NT, preferred_element_type=F32).astype(BF16)


def _inproj(x2d, norm_w, w_tok, w_qd_t, w_vd_t):
    n = x2d.shape[0]
    tm = TOKEN_TILE
    fixed = lambda i: (0, 0)
    return pl.pallas_call(
        _inproj_body,
        grid=(n // tm,),
        in_specs=[
            pl.BlockSpec((tm, D_MODEL), lambda i: (i, 0)),
            pl.BlockSpec((1, D_MODEL), fixed),
            pl.BlockSpec((D_MODEL, TOKEN_MAJOR_COLS), fixed),
            pl.BlockSpec((W_DIFF, D_MODEL), fixed),
            pl.BlockSpec((W_DIFF, D_MODEL), fixed),
        ],
        out_specs=[
            pl.BlockSpec((tm, TOKEN_MAJOR_COLS), lambda i: (i, 0)),
            pl.BlockSpec((1, W_DIFF, tm), lambda i: (i, 0, 0)),
            pl.BlockSpec((1, W_DIFF, tm), lambda i: (i, 0, 0)),
        ],
        out_shape=[
            jax.ShapeDtypeStruct((n, TOKEN_MAJOR_COLS), BF16),
            jax.ShapeDtypeStruct((n // tm, W_DIFF, tm), BF16),
            jax.ShapeDtypeStruct((n // tm, W_DIFF, tm), BF16),
        ],
        compiler_params=_params(("parallel",)),
        name="inproj",
    )(x2d, norm_w, w_tok, w_qd_t, w_vd_t)


def _na_bias_table(rpb):
    c = jnp.arange(GRID_W)
    col_start = jnp.clip(c - NA_KW // 2, 0, GRID_W - NA_KW)
    in_win = (c[None, :] >= col_start[:, None]) & (c[None, :] < col_start[:, None] + NA_KW)
    dc = jnp.clip(c[None, :] - c[:, None], -(NA_KW - 1), NA_KW - 1) + (NA_KW - 1)
    bc = rpb.astype(F32)[:, :, dc]
    bc = jnp.where(in_win[None, None], bc, MASKED)
    tabs = jnp.stack([bc[:, d0:d0 + NA_KH] for d0 in range(NA_KH)])
    tabs = tabs.transpose(0, 1, 3, 2, 4)
    return tabs.reshape(NA_KH, NA_HEADS // 2, 2 * GRID_W, NA_KH * GRID_W)


def _na_body(q_ref, kp_ref, kc_ref, kn_ref, vp_ref, vc_ref, vn_ref, bias_ref, nw_ref, o_ref,
             kbuf, vbuf, *, rows):
    j = pl.program_id(1)
    blk = NA_ROWS_PER_STEP * GRID_W
    kbuf[0:blk] = kp_ref[...]
    kbuf[blk:2 * blk] = kc_ref[...]
    kbuf[2 * blk:3 * blk] = kn_ref[...]
    vbuf[0:blk] = vp_ref[...]
    vbuf[blk:2 * blk] = vc_ref[...]
    vbuf[2 * blk:3 * blk] = vn_ref[...]
    lo = lax.broadcasted_iota(jnp.int32, (GRID_W, LANES), 1) < NA_HEAD_DIM
    nw = nw_ref[...]
    nkeys = NA_KH * GRID_W

    def softmax_parts(s):
        m = jnp.max(s, axis=-1, keepdims=True)
        p = jnp.exp(s - m)
        return p.astype(BF16), jnp.sum(p, axis=-1, keepdims=True)

    def one_row(a, carry):
        r = j * NA_ROWS_PER_STEP + a
        row_start = jnp.clip(r - NA_KH // 2, 0, rows - NA_KH)
        d0 = row_start - r + (NA_KH - 1)
        koff = pl.multiple_of((row_start - (j - 1) * NA_ROWS_PER_STEP) * GRID_W, GRID_W)
        qoff = pl.multiple_of(a * GRID_W, GRID_W)
        pairs = range(NA_HEADS // 2)
        cols = [slice(hp * LANES, (hp + 1) * LANES) for hp in pairs]
        scores = []
        for hp in pairs:
            qp = q_ref[pl.ds(qoff, GRID_W), cols[hp]]
            zero = jnp.zeros_like(qp)
            q2 = jnp.concatenate([jnp.where(lo, qp, zero), jnp.where(lo, zero, qp)], axis=0)
            kk = kbuf[pl.ds(koff, nkeys), cols[hp]]
            scores.append(lax.dot_general(q2, kk, _NT, preferred_element_type=F32) + bias_ref[d0, hp])
        probs = [softmax_parts(s) for s in scores]
        outs = [jnp.dot(p, vbuf[pl.ds(koff, nkeys), cols[hp]], preferred_element_type=F32) / l
                for hp, (p, l) in zip(pairs, probs)]
        for hp in pairs:
            cs = cols[hp]
            o = jnp.where(lo, outs[hp][0:GRID_W], outs[hp][GRID_W:2 * GRID_W])
            sq = o * o
            msa = jnp.sum(jnp.where(lo, sq, 0.0), axis=-1, keepdims=True) * (1.0 / NA_HEAD_DIM)
            msb = jnp.sum(jnp.where(lo, 0.0, sq), axis=-1, keepdims=True) * (1.0 / NA_HEAD_DIM)
            inv = jnp.where(lo, lax.rsqrt(msa + EPS), lax.rsqrt(msb + EPS))
            o_ref[pl.ds(qoff, GRID_W), cs] = ((o * inv) * nw).astype(BF16)
        return carry

    lax.fori_loop(0, NA_ROWS_PER_STEP, one_row, 0, unroll=8)


def _na_attention(proj, bias_tab, norm_w2, batch, seq):
    rows = seq // GRID_W
    assert rows % NA_ROWS_PER_STEP == 0 and rows >= NA_KH
    nb = rows // NA_ROWS_PER_STEP
    blk = NA_ROWS_PER_STEP * GRID_W
    n = batch * seq

    def at(col, shift):
        return pl.BlockSpec(
            (blk, W_NA), lambda b, j: (b * nb + jnp.clip(j + shift, 0, nb - 1), col))

    return pl.pallas_call(
        functools.partial(_na_body, rows=rows),
        grid=(batch, nb),
        in_specs=[
            at(0, 0),
            at(1, -1), at(1, 0), at(1, 1),
            at(2, -1), at(2, 0), at(2, 1),
            pl.BlockSpec((NA_KH, NA_HEADS // 2, 2 * GRID_W, NA_KH * GRID_W), lambda b, j: (0, 0, 0, 0)),
            pl.BlockSpec((1, LANES), lambda b, j: (0, 0)),
        ],
        out_specs=pl.BlockSpec((blk, W_NA), lambda b, j: (b * nb + j, 0)),
        out_shape=jax.ShapeDtypeStruct((n, W_NA), BF16),
        scratch_shapes=[pltpu.VMEM((3 * blk, W_NA), BF16), pltpu.VMEM((3 * blk, W_NA), BF16)],
        compiler_params=_params(("parallel", "parallel")),
        name="na_attention",
    )(proj, proj, proj, proj, proj, proj, proj, bias_tab, norm_w2)


def _t5_bucket(rel):
    nb = T5_BUCKETS // 2
    max_exact = nb // 2
    n = jnp.abs(rel)
    large = max_exact + (jnp.log(jnp.maximum(n, 1).astype(F32) / max_exact)
                         / math.log(T5_MAX_DIST / max_exact) * (nb - max_exact)).astype(jnp.int32)
    large = jnp.minimum(large, nb - 1)
    return jnp.where(rel > 0, nb, 0) + jnp.where(n < max_exact, n, large)


def _diff_bias_table(t5_table):
    x = DIFF_BLOCK
    assert x + 1 >= T5_MAX_DIST
    span = 2 * x
    rel = (x - 1) - jnp.arange(span)[None, :] + jnp.arange(-2, 3)[:, None] * x
    diag = (t5_table.astype(F32) * LOG2E)[_t5_bucket(rel)].transpose(2, 0, 1)
    skew = jnp.tile(diag, (1, 1, x))[:, :, :x * (span - 1)].reshape(DIFF_HEADS, 5, x, span - 1)
    return skew[:, :, :, x - 1:2 * x - 1]


def _diff_body(qt_ref, k_ref, vt_ref, bias_ref, lam_ref, sw_ref, o_ref, s_even, s_odd, *, nblk):
    i = pl.program_id(2)
    x = DIFF_BLOCK
    qt = qt_ref[0]
    first = lax.broadcasted_iota(jnp.int32, (LANES, x), 0) < DIFF_HALF_DIM
    zero = jnp.zeros_like(qt)
    qt_both = jnp.concatenate([jnp.where(first, qt, zero), jnp.where(first, zero, qt)], axis=1)

    def scores(j, s_ref):
        kc = k_ref[pl.ds(pl.multiple_of(j * x, x), x), :]
        s_ref[...] = jnp.dot(kc, qt_both, preferred_element_type=F32)

    ones_rows = jnp.ones((DIFF_ONES_ROWS, x), BF16)

    def consume(j, s_ref, carry, far_bias):
        vt_ext = jnp.concatenate([vt_ref[j], ones_rows], axis=0)
        out = []
        for c in range(2):
            m, acc = carry[2 * c:2 * c + 2]
            s = s_ref[:, c * x:(c + 1) * x]
            if far_bias is None:
                s = s + bias_ref[0, jnp.clip(j - i, -2, 2) + 2]
                shift = 0.0
            else:
                shift = far_bias
            m_in = m - shift
            mn = jnp.maximum(m_in, jnp.max(s, axis=0, keepdims=True))
            p = jnp.exp2(s - mn).astype(BF16)
            acc = jnp.exp2(m_in - mn) * acc + jnp.dot(vt_ext, p, preferred_element_type=F32)
            out += [mn + shift, acc]
        return tuple(out)

    def pair_loop(lo, hi, far_bias, carry):
        def body(jj, carry):
            j = 2 * jj
            scores(j + 1, s_odd)
            carry = consume(j, s_even, carry, far_bias)
            scores(j + 2, s_even)
            return consume(j + 1, s_odd, carry, far_bias)
        return lax.fori_loop(lo, hi, body, carry)

    def last_pair(far_bias, carry):
        scores(nblk - 1, s_odd)
        carry = consume(nblk - 2, s_even, carry, far_bias)
        return consume(nblk - 1, s_odd, carry, far_bias)

    npairs = nblk // 2
    near_lo = jnp.maximum((i - 1) // 2, 0)
    near_hi = jnp.minimum((i + 1) // 2, npairs - 1)
    bias_before = bias_ref[0, 0, 0:1, 0:1]
    bias_after = bias_ref[0, 4, 0:1, 0:1]
    init = (jnp.full((1, x), -jnp.inf, F32), jnp.zeros((DIFF_V_DIM + DIFF_ONES_ROWS, x), F32)) * 2
    scores(0, s_even)
    if npairs >= DIFF_SPLIT_MIN_PAIRS:
        carry = pair_loop(0, jnp.minimum(near_lo, npairs - 1), bias_before, init)
        carry = pair_loop(near_lo, jnp.minimum(near_hi + 1, npairs - 1), None, carry)
        carry = pair_loop(near_hi + 1, npairs - 1, bias_after, carry)
        _, ext1, _, ext2 = lax.cond(near_hi == npairs - 1,
                                    functools.partial(last_pair, None),
                                    functools.partial(last_pair, bias_after), carry)
    else:
        _, ext1, _, ext2 = last_pair(None, pair_loop(0, npairs - 1, None, init))
    acc1, l1 = ext1[0:DIFF_V_DIM], ext1[DIFF_V_DIM:DIFF_V_DIM + 1]
    acc2, l2 = ext2[0:DIFF_V_DIM], ext2[DIFF_V_DIM:DIFF_V_DIM + 1]
    lam = (jnp.exp(jnp.sum(lam_ref[0:1, :] * lam_ref[1:2, :], axis=-1, keepdims=True))
           - jnp.exp(jnp.sum(lam_ref[2:3, :] * lam_ref[3:4, :], axis=-1, keepdims=True)) + LAMBDA_INIT)
    o = acc1 / l1 - lam * (acc2 / l2)
    ms = jnp.mean(o * o, axis=0, keepdims=True)
    y = ((o * lax.rsqrt(ms + EPS)) * sw_ref[...]) * (1.0 - LAMBDA_INIT)
    o_ref[0] = y.astype(BF16)


def _diff_attention(proj, qd_t, vd_t, bias_tab, lam_vecs, subln_col, batch, seq):
    x = DIFF_BLOCK
    assert seq % (2 * x) == 0 and x == TOKEN_TILE
    nblk = seq // x
    n = batch * seq
    kcol = 3 * W_NA // LANES
    return pl.pallas_call(
        functools.partial(_diff_body, nblk=nblk),
        grid=(batch, DIFF_HEADS, nblk),
        in_specs=[
            pl.BlockSpec((1, LANES, x), lambda b, h, i: (b * nblk + i, h, 0)),
            pl.BlockSpec((seq, LANES), lambda b, h, i: (b, kcol + h)),
            pl.BlockSpec((nblk, LANES, x), lambda b, h, i: (b, h, 0)),
            pl.BlockSpec((1, 5, x, x), lambda b, h, i: (h, 0, 0, 0)),
            pl.BlockSpec((4, DIFF_HALF_DIM), lambda b, h, i: (0, 0)),
            pl.BlockSpec((DIFF_V_DIM, 1), lambda b, h, i: (0, 0)),
        ],
        out_specs=pl.BlockSpec((1, LANES, x), lambda b, h, i: (b * nblk + i, h, 0)),
        out_shape=jax.ShapeDtypeStruct((n // x, W_DIFF, x), BF16),
        scratch_shapes=[pltpu.VMEM((x, 2 * x), F32), pltpu.VMEM((x, 2 * x), F32)],
        compiler_params=_params(("parallel", "parallel", "parallel")),
        name="diff_attention",
    )(qd_t, proj, vd_t, bias_tab, lam_vecs, subln_col)


def _outproj_body(x_ref, ya_ref, yb_ref, wa_ref, wb_ref, nw_ref, wr_ref, x2_ref, h2_ref, aff_ref):
    attn = (jnp.dot(ya_ref[...], wa_ref[...], preferred_element_type=F32)
            + lax.dot_general(yb_ref[0], wb_ref[...], (((0,), (0,)), ((), ())), preferred_element_type=F32))
    x2 = x_ref[...] + attn
    x2_ref[...] = x2
    ms = jnp.mean(x2 * x2, axis=-1, keepdims=True)
    h2 = (x2 * lax.rsqrt(ms + EPS)) * nw_ref[...]
    for s in range(D_MODEL // LANES):
        h2_ref[:, s, :] = h2[:, s * LANES:(s + 1) * LANES]
    logits = lax.dot_general(wr_ref[...], h2.astype(BF16), _NT, preferred_element_type=F32)
    e = jnp.exp(logits - jnp.max(logits, axis=0, keepdims=True))
    aff_ref[...] = e / jnp.sum(e, axis=0, keepdims=True)


def _outproj(x2d, ya, yb, w_out, norm_w, w_router_t):
    n = x2d.shape[0]
    tm = TOKEN_TILE
    row = lambda i: (i, 0)
    fixed = lambda i: (0, 0)
    return pl.pallas_call(
        _outproj_body,
        grid=(n // tm,),
        in_specs=[
            pl.BlockSpec((tm, D_MODEL), row),
            pl.BlockSpec((tm, W_NA), row),
            pl.BlockSpec((1, W_DIFF, tm), lambda i: (i, 0, 0)),
            pl.BlockSpec((W_NA, D_MODEL), lambda i: (0, 0)),
            pl.BlockSpec((W_DIFF, D_MODEL), lambda i: (1, 0)),
            pl.BlockSpec((1, D_MODEL), fixed),
            pl.BlockSpec((N_EXPERTS, D_MODEL), fixed),
        ],
        out_specs=[
            pl.BlockSpec((tm, D_MODEL), row),
            pl.BlockSpec((tm, D_MODEL // LANES, LANES), lambda i: (i, 0, 0)),
            pl.BlockSpec((N_EXPERTS, tm), lambda i: (0, i)),
        ],
        out_shape=[
            jax.ShapeDtypeStruct((n, D_MODEL), F32),
            jax.ShapeDtypeStruct((n, D_MODEL // LANES, LANES), F32),
            jax.ShapeDtypeStruct((N_EXPERTS, n), F32),
        ],
        compiler_params=_params(("parallel",)),
        name="outproj_router",
    )(x2d, ya, yb, w_out, w_out, norm_w, w_router_t)


def _moe_body(idx_ref, idx_next_ref, h2_hbm, g_ref, wg_ref, wu_ref, wd_ref, o_ref, x_even, x_odd, sem, *, tm):
    nt = pl.num_programs(1)
    step = pl.program_id(0) * nt + pl.program_id(1)
    total = pl.num_programs(0) * nt

    def row_copy(tok, i, buf, sl):
        return pltpu.make_async_copy(h2_hbm.at[pl.ds(tok, 1)], buf.at[pl.ds(i, 1)], sem.at[sl])

    def wait_tile(buf, sl):
        pltpu.make_async_copy(h2_hbm.at[pl.ds(0, tm)], buf, sem.at[sl]).wait()

    @pl.when(step == 0)
    def _():
        def body(i, carry):
            row_copy(idx_ref[0, 0, i], i, x_even, 0).start()
            return carry
        lax.fori_loop(0, tm, body, 0, unroll=8)

    def run(cur, cur_sl, nxt, nxt_sl):
        wait_tile(cur, cur_sl)
        for i in range(tm):
            row_copy(idx_next_ref[0, 0, i], i, nxt, nxt_sl).start()
        x = jnp.concatenate([cur[:, s, :] for s in range(D_MODEL // LANES)], axis=1).astype(BF16)
        acc = jnp.zeros((tm, D_MODEL), F32)
        for f0, fc in FFN_CHUNKS:
            g = jnp.dot(x, wg_ref[0, :, f0:f0 + fc], preferred_element_type=F32)
            u = jnp.dot(x, wu_ref[0, :, f0:f0 + fc], preferred_element_type=F32)
            h = (jax.nn.silu(g) * u).astype(BF16)
            acc = acc + jnp.dot(h, wd_ref[0, f0:f0 + fc, :], preferred_element_type=F32)
        o_ref[...] = (acc * g_ref[...]).astype(o_ref.dtype)

        @pl.when(step == total - 1)
        def _():
            wait_tile(nxt, nxt_sl)

    @pl.when(step % 2 == 0)
    def _():
        run(x_even, 0, x_odd, 1)

    @pl.when(step % 2 == 1)
    def _():
        run(x_odd, 1, x_even, 0)


def _moe_experts(h2, idx, gates, w_gate, w_up, w_down):
    n_exp, cap = idx.shape
    tm = min(MOE_TILE, cap)
    assert cap % tm == 0
    nt = cap // tm
    idx3 = idx.reshape(n_exp * nt, 1, tm)

    def tile_ahead(k):
        return pl.BlockSpec((1, 1, tm), lambda e, t: (jnp.minimum(e * nt + t + k, n_exp * nt - 1), 0, 0),
                            memory_space=pltpu.SMEM)

    tile_shape = (tm, D_MODEL // LANES, LANES)
    return pl.pallas_call(
        functools.partial(_moe_body, tm=tm),
        grid=(n_exp, nt),
        in_specs=[
            tile_ahead(0), tile_ahead(1),
            pl.BlockSpec(memory_space=pl.ANY),
            pl.BlockSpec((tm, 1), lambda e, t: (e * nt + t, 0)),
            pl.BlockSpec((1, D_MODEL, D_EXPERT), lambda e, t: (e, 0, 0)),
            pl.BlockSpec((1, D_MODEL, D_EXPERT), lambda e, t: (e, 0, 0)),
            pl.BlockSpec((1, D_EXPERT, D_MODEL), lambda e, t: (e, 0, 0)),
        ],
        out_specs=pl.BlockSpec((tm, D_MODEL), lambda e, t: (e * nt + t, 0)),
        out_shape=jax.ShapeDtypeStruct((n_exp * cap, D_MODEL), BF16),
        scratch_shapes=[pltpu.VMEM(tile_shape, F32), pltpu.VMEM(tile_shape, F32), pltpu.SemaphoreType.DMA((2,))],
        compiler_params=_params(("arbitrary", "arbitrary")),
        name="moe_experts",
    )(idx3, idx3, h2, gates.reshape(n_exp * cap, 1), w_gate, w_up, w_down)


def _route_threshold_body(aff_ref, theta_ref, need_ref, *, cap):
    n_exp = aff_ref.shape[0]

    def count_ge(cand):
        bits = pltpu.bitcast(aff_ref[...], jnp.int32)
        return jnp.sum(jnp.where(bits >= cand, 1.0, 0.0), axis=1, keepdims=True)

    def step(b, theta):
        cand = theta | jnp.left_shift(jnp.int32(1), 30 - b)
        return jnp.where(count_ge(cand) >= cap, cand, theta)

    theta = lax.fori_loop(0, 31, step, jnp.zeros((n_exp, 1), jnp.int32))
    need = cap - count_ge(theta + 1)
    theta_ref[...] = jnp.broadcast_to(theta, theta_ref.shape)
    need_ref[...] = jnp.broadcast_to(need, need_ref.shape)


def _route_threshold(aff_t, cap):
    n_exp, n = aff_t.shape
    full = lambda i: (0, 0)
    return pl.pallas_call(
        functools.partial(_route_threshold_body, cap=cap),
        grid=(1,),
        in_specs=[pl.BlockSpec((n_exp, n), full)],
        out_specs=[pl.BlockSpec((n_exp, LANES), full), pl.BlockSpec((n_exp, LANES), full)],
        out_shape=[jax.ShapeDtypeStruct((n_exp, LANES), jnp.int32), jax.ShapeDtypeStruct((n_exp, LANES), F32)],
        compiler_params=_params(("arbitrary",)),
        name="route_threshold",
    )(aff_t)


def _route_prefix_body(aff_ref, tri_ref, theta_ref, need_ref, slot_ref, base_ref, run_sel, run_tie, *, cap):
    @pl.when(pl.program_id(0) == 0)
    def _():
        run_sel[...] = jnp.zeros_like(run_sel)
        run_tie[...] = jnp.zeros_like(run_tie)

    n_exp = aff_ref.shape[0]
    w = ROUTE_CHUNK
    theta = theta_ref[:, 0:1]
    need = need_ref[:, 0:1]
    expert_base = lax.broadcasted_iota(jnp.int32, (n_exp, 1), 0) * cap
    tri = tri_ref[...]
    for k in range(ROUTE_CHUNKS_PER_STEP):
        bits = pltpu.bitcast(aff_ref[:, k * w:(k + 1) * w], jnp.int32)
        tie = bits == theta
        tie_f = jnp.where(tie, 1.0, 0.0)
        tie_incl = jnp.dot(tie_f.astype(BF16), tri, preferred_element_type=F32)
        tie_rank = run_tie[:, 0:1] + tie_incl - tie_f
        sel = (bits > theta) | (tie & (tie_rank < need))
        sel_f = jnp.where(sel, 1.0, 0.0)
        incl = jnp.dot(sel_f.astype(BF16), tri, preferred_element_type=F32)
        rank = run_sel[:, 0:1] + incl - sel_f
        slot_ref[:, k * w:(k + 1) * w] = jnp.where(sel, rank.astype(jnp.int32) + expert_base, -1)
        base_ref[k] = run_sel[...].astype(jnp.int32)
        run_sel[...] = run_sel[...] + jnp.sum(sel_f, axis=1, keepdims=True)
        run_tie[...] = run_tie[...] + jnp.sum(tie_f, axis=1, keepdims=True)


def _route_prefix(aff_t, theta, need, cap):
    n_exp, n = aff_t.shape
    w = ROUTE_CHUNK
    span = w * ROUTE_CHUNKS_PER_STEP
    assert n % span == 0
    tri = (jnp.arange(w)[:, None] <= jnp.arange(w)[None, :]).astype(BF16)
    fixed = lambda i: (0, 0)
    return pl.pallas_call(
        functools.partial(_route_prefix_body, cap=cap),
        grid=(n // span,),
        in_specs=[
            pl.BlockSpec((n_exp, span), lambda i: (0, i)),
            pl.BlockSpec((w, w), fixed),
            pl.BlockSpec((n_exp, LANES), fixed),
            pl.BlockSpec((n_exp, LANES), fixed),
        ],
        out_specs=[
            pl.BlockSpec((n_exp, span), lambda i: (0, i)),
            pl.BlockSpec((ROUTE_CHUNKS_PER_STEP, n_exp, LANES), lambda i: (i, 0, 0)),
        ],
        out_shape=[
            jax.ShapeDtypeStruct((n_exp, n), jnp.int32),
            jax.ShapeDtypeStruct((n // w, n_exp, LANES), jnp.int32),
        ],
        scratch_shapes=[pltpu.VMEM((n_exp, LANES), F32), pltpu.VMEM((n_exp, LANES), F32)],
        compiler_params=_params(("arbitrary",)),
        name="route_prefix",
    )(aff_t, tri, theta, need)


def _route_compact_body(cb_ref, slot_ref, aff_ref, idx_ref, gate_ref, *, cap, nchunks):
    e = pl.program_id(0)
    w = ROUTE_CHUNK
    s = LANES
    slot_iota = lax.broadcasted_iota(jnp.int32, (s, w), 0)
    tok_iota = lax.broadcasted_iota(jnp.int32, (s, w), 1)
    eye = lax.broadcasted_iota(jnp.int32, (s, s), 0) == lax.broadcasted_iota(jnp.int32, (s, s), 1)

    def fold(v):
        out = v[:, 0:s]
        for q in range(1, w // s):
            out = out + v[:, q * s:(q + 1) * s]
        return out

    def to_row(acc):
        col = jnp.sum(acc, axis=1, keepdims=True)
        return jnp.sum(jnp.where(eye, col, 0.0), axis=0, keepdims=True)

    def one_tile(jt, c_first):
        j0 = jt * s
        c_first = lax.while_loop(lambda c: cb_ref[e, c + 1] <= j0, lambda c: c + 1, c_first)
        want = slot_iota + (e * cap + j0)

        def more(state):
            c = state[0]
            return jnp.logical_and(c < nchunks, cb_ref[e, jnp.minimum(c, nchunks - 1)] < j0 + s)

        def chunk(state):
            c, acc_i, acc_g = state
            hit = want == slot_ref[0, pl.ds(c, 1), :]
            tok = (tok_iota + c * w).astype(F32)
            acc_i = acc_i + fold(jnp.where(hit, tok, 0.0))
            acc_g = acc_g + fold(jnp.where(hit, aff_ref[0, pl.ds(c, 1), :], 0.0))
            return c + 1, acc_i, acc_g

        zero = jnp.zeros((s, s), F32)
        _, acc_i, acc_g = lax.while_loop(more, chunk, (c_first, zero, zero))
        idx_ref[0, pl.ds(jt, 1), :] = to_row(acc_i).astype(jnp.int32)
        gate_ref[0, pl.ds(jt, 1), :] = to_row(acc_g)
        return c_first

    lax.fori_loop(0, cap // s, one_tile, jnp.int32(0))


def _route_compact(chunk_base, slot3, aff3, cap):
    n_exp, nchunks, w = slot3.shape
    assert cap % LANES == 0
    per_e = lambda e, cb: (e, 0, 0)
    grid_spec = pltpu.PrefetchScalarGridSpec(
        num_scalar_prefetch=1,
        grid=(n_exp,),
        in_specs=[pl.BlockSpec((1, nchunks, w), per_e), pl.BlockSpec((1, nchunks, w), per_e)],
        out_specs=[pl.BlockSpec((1, cap // LANES, LANES), per_e), pl.BlockSpec((1, cap // LANES, LANES), per_e)],
    )
    return pl.pallas_call(
        functools.partial(_route_compact_body, cap=cap, nchunks=nchunks),
        grid_spec=grid_spec,
        out_shape=[
            jax.ShapeDtypeStruct((n_exp, cap // LANES, LANES), jnp.int32),
            jax.ShapeDtypeStruct((n_exp, cap // LANES, LANES), F32),
        ],
        compiler_params=_params(("arbitrary",)),
        name="route_compact",
    )(chunk_base, slot3, aff3)


def _combine_body(cb_ref, slot_ref, x2_ref, ye_hbm, nw_ref, o_ref, stage, chunk_row, chunk_count, sem,
                  *, cap, ntiles):
    tile = pl.program_id(0)
    cur = tile % 2
    tt = COMBINE_TILE
    ch = COMBINE_DMA_ROWS
    n_exp = slot_ref.shape[0]
    kblock = 256 // ch

    def chunk_copy(row, k, sl):
        return pltpu.make_async_copy(ye_hbm.at[pl.ds(row, ch), :], stage.at[sl, pl.ds(k * ch, ch), :], sem.at[sl])

    def issue(t, sl):
        def per_expert(e, k):
            lo = cb_ref[e, t]
            hi = cb_ref[e, t + 1]
            first = lo // ch
            nch = jnp.where(hi > lo, (hi - 1) // ch - first + 1, 0)

            def per_chunk(i, k):
                row = pl.multiple_of(e * cap + (first + i) * ch, ch)
                chunk_copy(row, k, sl).start()
                chunk_row[sl, k] = row
                return k + 1

            return lax.fori_loop(0, nch, per_chunk, k)

        chunk_count[sl] = lax.fori_loop(0, n_exp, per_expert, jnp.int32(0))

    @pl.when(tile == 0)
    def _():
        issue(0, 0)

    @pl.when(tile + 1 < ntiles)
    def _():
        issue(tile + 1, 1 - cur)

    nchunk = chunk_count[cur]

    def wait_one(k, carry):
        chunk_copy(0, k, cur).wait()
        return carry

    lax.fori_loop(0, nchunk, wait_one, 0)
    nblock = (nchunk + kblock - 1) // kblock

    def pad_one(k, carry):
        stage[cur, pl.ds(pl.multiple_of(k * ch, ch), ch), :] = jnp.zeros((ch, D_MODEL), BF16)
        chunk_row[cur, k] = -(1 << 30)
        return carry

    lax.fori_loop(nchunk, nblock * kblock, pad_one, 0)
    row_iota = lax.broadcasted_iota(jnp.int32, (ch, tt), 0)

    o_ref[...] = x2_ref[...]

    def one_block(b, carry):
        parts = []
        for i in range(kblock):
            row = chunk_row[cur, b * kblock + i]
            e = jnp.clip(row // cap, 0, n_exp - 1)
            hit = (row_iota + row) == slot_ref[pl.ds(e, 1), :]
            parts.append(jnp.where(hit, 1.0, 0.0).astype(BF16))
        onehot_t = jnp.concatenate(parts, axis=0)
        rows = stage[cur, pl.ds(pl.multiple_of(b * 256, 256), 256), :]
        o_ref[...] += lax.dot_general(onehot_t, rows, (((0,), (0,)), ((), ())), preferred_element_type=F32)
        return carry

    lax.fori_loop(0, nblock, one_block, 0)
    x3 = o_ref[...]
    ms = jnp.mean(x3 * x3, axis=-1, keepdims=True)
    o_ref[...] = (x3 * lax.rsqrt(ms + EPS)) * nw_ref[...]


def _combine(tile_base, slot, x2, ye, norm_w, cap):
    n_exp, n = slot.shape
    tt = COMBINE_TILE
    ch = COMBINE_DMA_ROWS
    ntiles = n // tt
    assert cap % ch == 0 and tt % ch == 0
    max_rows = n_exp * (tt + ch)
    max_rows = -(-max_rows // 256) * 256
    grid_spec = pltpu.PrefetchScalarGridSpec(
        num_scalar_prefetch=1,
        grid=(ntiles,),
        in_specs=[
            pl.BlockSpec((n_exp, tt), lambda t, cb: (0, t)),
            pl.BlockSpec((tt, D_MODEL), lambda t, cb: (t, 0)),
            pl.BlockSpec(memory_space=pl.ANY),
            pl.BlockSpec((1, D_MODEL), lambda t, cb: (0, 0)),
        ],
        out_specs=pl.BlockSpec((tt, D_MODEL), lambda t, cb: (t, 0)),
        scratch_shapes=[
            pltpu.VMEM((2, max_rows, D_MODEL), BF16),
            pltpu.SMEM((2, max_rows // ch), jnp.int32),
            pltpu.SMEM((2,), jnp.int32),
            pltpu.SemaphoreType.DMA((2,)),
        ],
    )
    return pl.pallas_call(
        functools.partial(_combine_body, cap=cap, ntiles=ntiles),
        grid_spec=grid_spec,
        out_shape=jax.ShapeDtypeStruct((n, D_MODEL), F32),
        compiler_params=_params(("arbitrary",)),
        name="combine_final",
    )(tile_base, slot, x2, ye, norm_w)


def _trunk(x, p):
    batch, seq = x.shape[0], x.shape[1]
    n = batch * seq
    x2d = x.reshape(n, D_MODEL)
    proj, qd_t, vd_t = _inproj(x2d, p["mix_norm_w"], p["w_tok"], p["w_qd_t"], p["w_vd_t"])
    ya = _na_attention(proj, p["na_bias"], p["na_norm_w2"], batch, seq)
    yb_t = _diff_attention(proj, qd_t, vd_t, p["diff_bias"], p["lam_vecs"], p["subln_col"], batch, seq)
    x2, h2, aff_t = _outproj(x2d, ya, yb_t, p["w_out"], p["ffn_norm_w"], p["w_router_t"])
    cap = CAPACITY_FACTOR * n // N_EXPERTS
    theta, need = _route_threshold(aff_t, cap)
    slot, base = _route_prefix(aff_t, theta, need, cap)
    nchunks = n // ROUTE_CHUNK
    chunk_base = jnp.concatenate([base[:, :, 0].T, jnp.full((N_EXPERTS, 1), cap, jnp.int32)], axis=1)
    idx, gates = _route_compact(chunk_base, slot.reshape(N_EXPERTS, nchunks, ROUTE_CHUNK),
                                aff_t.reshape(N_EXPERTS, nchunks, ROUTE_CHUNK), cap)
    ye = _moe_experts(h2, idx.reshape(N_EXPERTS, cap), gates.reshape(N_EXPERTS, cap),
                      p["w_gate"], p["w_up"], p["w_down"])
    tile_base = chunk_base[:, ::COMBINE_TILE // ROUTE_CHUNK]
    y = _combine(tile_base, slot, x2, ye, p["final_norm_w"], cap)
    return y.reshape(batch, seq, D_MODEL)


def _prepare_params(mix_norm_w, w_in, na_rpb, na_norm_w, t5_table, lambda_q1, lambda_k1, lambda_q2, lambda_k2,
                    subln_w, w_out, ffn_norm_w, w_router, w_gate, w_up, w_down, final_norm_w):
    w = w_in[0]
    qa, ka_va = w[:, 0:W_NA] * NA_HEAD_DIM ** -0.5, w[:, W_NA:3 * W_NA]
    qd = w[:, 3 * W_NA:3 * W_NA + W_DIFF] * DIFF_HALF_DIM ** -0.5
    kd, vd = w[:, 3 * W_NA + W_DIFF:3 * W_NA + 2 * W_DIFF], w[:, 3 * W_NA + 2 * W_DIFF:]
    return {
        "mix_norm_w": mix_norm_w[0].reshape(1, D_MODEL),
        "w_tok": jnp.concatenate([qa, ka_va, kd], axis=1).astype(BF16),
        "w_qd_t": qd.T.astype(BF16),
        "w_vd_t": vd.T.astype(BF16),
        "na_bias": _na_bias_table(na_rpb[0]),
        "na_norm_w2": jnp.tile(na_norm_w[0], 2).reshape(1, LANES),
        "diff_bias": _diff_bias_table(t5_table),
        "lam_vecs": jnp.stack([lambda_q1[0], lambda_k1[0], lambda_q2[0], lambda_k2[0]]).astype(F32),
        "subln_col": subln_w[0].reshape(DIFF_V_DIM, 1),
        "w_out": w_out[0].astype(BF16),
        "ffn_norm_w": ffn_norm_w[0].reshape(1, D_MODEL),
        "w_router_t": w_router[0].T.astype(BF16),
        "w_gate": w_gate[0].astype(BF16),
        "w_up": w_up[0].astype(BF16),
        "w_down": w_down[0].astype(BF16),
        "final_norm_w": final_norm_w.reshape(1, D_MODEL),
    }


def kernel(x_prompt, x_sample, mix_norm_w, w_in, na_rpb, na_norm_w, t5_table, lambda_q1, lambda_k1,
           lambda_q2, lambda_k2, subln_w, w_out, ffn_norm_w, w_router, w_gate, w_up, w_down, final_norm_w):
    p = _prepare_params(mix_norm_w, w_in, na_rpb, na_norm_w, t5_table, lambda_q1, lambda_k1, lambda_q2,
                        lambda_k2, subln_w, w_out, ffn_norm_w, w_router, w_gate, w_up, w_down, final_norm_w)
    return (_trunk(x_prompt, p), _trunk(x_sample, p))
```

```python
import functools
import math

import jax
import jax.numpy as jnp
from jax import lax
from jax.experimental import pallas as pl
from jax.experimental.pallas import tpu as pltpu

F32 = jnp.float32
BF16 = jnp.bfloat16

D_MODEL = 1024
GRID_W = 64
NA_HEADS = 8
NA_HEAD_DIM = 64
NA_KH = 8
NA_KW = 16
DIFF_HEADS = 4
DIFF_HALF_DIM = 64
DIFF_V_DIM = 128
W_NA = 512
W_DIFF = 512
D_IN = 3072
T5_BUCKETS = 32
T5_MAX_DIST = 128
N_EXPERTS = 16
CAPACITY_FACTOR = 2
D_EXPERT = 2816
EPS = 1e-6
LAMBDA_INIT = 0.8 - 0.6 * math.exp(-0.3 * 0)
LOG2E = math.log2(math.e)
TOKEN_MAJOR_COLS = 3 * W_NA + W_DIFF

LANES = 128
MXU_DEPTH = 256
TOKEN_TILE = 512
NA_ROWS_PER_STEP = 8
DIFF_BLOCK = 512
DIFF_ONES_ROWS = 16
DIFF_SPLIT_MIN_PAIRS = 4
MOE_TILE = 512
ROUTE_CHUNK = 256
ROUTE_CHUNKS_PER_STEP = 8
COMBINE_TILE = 512
COMBINE_DMA_ROWS = 32
FFN_CHUNKS = ((0, 512), (512, 512), (1024, 512), (1536, 512), (2048, 512), (2560, 256))
MASKED = -1e30
VMEM_LIMIT = 56 * 1024 * 1024

_NT = (((1,), (1,)), ((), ()))


def _params(sem, vmem=VMEM_LIMIT):
    return pltpu.CompilerParams(dimension_semantics=sem, vmem_limit_bytes=vmem)


def _inproj_body(x_ref, nw_ref, w_ref, wqt_ref, wvt_ref, o_ref, qt_ref, vt_ref):
    x = x_ref[...]
    ms = jnp.mean(x * x, axis=-1, keepdims=True)
    h = ((x * lax.rsqrt(ms + EPS)) * nw_ref[...]).astype(BF16)
    for c in range(TOKEN_MAJOR_COLS // W_NA):
        cols = slice(c * W_NA, (c + 1) * W_NA)
        o_ref[:, cols] = jnp.dot(h, w_ref[:, cols], preferred_element_type=F32).astype(BF16)
    qt = lax.dot_general(wqt_ref[...], h, _NT, preferred_element_type=F32)
    qt_ref[0] = (qt * LOG2E).astype(BF16)
    vt_ref[0] = lax.dot_general(wvt_ref[...], h, _NT, preferred_element_type=F32).astype(BF16)


def _inproj(x2d, norm_w, w_tok, w_qd_t, w_vd_t):
    n = x2d.shape[0]
    tm = TOKEN_TILE
    fixed = lambda i: (0, 0)
    return pl.pallas_call(
        _inproj_body,
        grid=(n // tm,),
        in_specs=[
            pl.BlockSpec((tm, D_MODEL), lambda i: (i, 0)),
            pl.BlockSpec((1, D_MODEL), fixed),
            pl.BlockSpec((D_MODEL, TOKEN_MAJOR_COLS), fixed),
            pl.BlockSpec((W_DIFF, D_MODEL), fixed),
            pl.BlockSpec((W_DIFF, D_MODEL), fixed),
        ],
        out_specs=[
            pl.BlockSpec((tm, TOKEN_MAJOR_COLS), lambda i: (i, 0)),
            pl.BlockSpec((1, W_DIFF, tm), lambda i: (i, 0, 0)),
            pl.BlockSpec((1, W_DIFF, tm), lambda i: (i, 0, 0)),
        ],
        out_shape=[
            jax.ShapeDtypeStruct((n, TOKEN_MAJOR_COLS), BF16),
            jax.ShapeDtypeStruct((n // tm, W_DIFF, tm), BF16),
            jax.ShapeDtypeStruct((n // tm, W_DIFF, tm), BF16),
        ],
        compiler_params=_params(("parallel",)),
        name="inproj",
    )(x2d, norm_w, w_tok, w_qd_t, w_vd_t)


def _na_bias_table(rpb):
    c = jnp.arange(GRID_W)
    col_start = jnp.clip(c - NA_KW // 2, 0, GRID_W - NA_KW)
    in_win = (c[None, :] >= col_start[:, None]) & (c[None, :] < col_start[:, None] + NA_KW)
    dc = jnp.clip(c[None, :] - c[:, None], -(NA_KW - 1), NA_KW - 1) + (NA_KW - 1)
    bc = rpb.astype(F32)[:, :, dc]
    bc = jnp.where(in_win[None, None], bc, MASKED)
    tabs = jnp.stack([bc[:, d0:d0 + NA_KH] for d0 in range(NA_KH)])
    tabs = tabs.transpose(0, 1, 3, 2, 4)
    return tabs.reshape(NA_KH, NA_HEADS // 2, 2 * GRID_W, NA_KH * GRID_W)


def _na_body(q_ref, kp_ref, kc_ref, kn_ref, vp_ref, vc_ref, vn_ref, bias_ref, nw_ref, o_ref,
             kbuf, vbuf, *, rows):
    j = pl.program_id(1)
    blk = NA_ROWS_PER_STEP * GRID_W
    kbuf[0:blk] = kp_ref[...]
    kbuf[blk:2 * blk] = kc_ref[...]
    kbuf[2 * blk:3 * blk] = kn_ref[...]
    vbuf[0:blk] = vp_ref[...]
    vbuf[blk:2 * blk] = vc_ref[...]
    vbuf[2 * blk:3 * blk] = vn_ref[...]
    lo = lax.broadcasted_iota(jnp.int32, (GRID_W, LANES), 1) < NA_HEAD_DIM
    nw = nw_ref[...]
    nkeys = NA_KH * GRID_W

    def softmax_parts(s):
        m = jnp.max(s, axis=-1, keepdims=True)
        p = jnp.exp(s - m)
        return p.astype(BF16), jnp.sum(p, axis=-1, keepdims=True)

    def one_row(a, carry):
        r = j * NA_ROWS_PER_STEP + a
        row_start = jnp.clip(r - NA_KH // 2, 0, rows - NA_KH)
        d0 = row_start - r + (NA_KH - 1)
        koff = pl.multiple_of((row_start - (j - 1) * NA_ROWS_PER_STEP) * GRID_W, GRID_W)
        qoff = pl.multiple_of(a * GRID_W, GRID_W)
        pairs = range(NA_HEADS // 2)
        cols = [slice(hp * LANES, (hp + 1) * LANES) for hp in pairs]
        scores = []
        for hp in pairs:
            qp = q_ref[pl.ds(qoff, GRID_W), cols[hp]]
            zero = jnp.zeros_like(qp)
            q2 = jnp.concatenate([jnp.where(lo, qp, zero), jnp.where(lo, zero, qp)], axis=0)
            kk = kbuf[pl.ds(koff, nkeys), cols[hp]]
            scores.append(lax.dot_general(q2, kk, _NT, preferred_element_type=F32) + bias_ref[d0, hp])
        probs = [softmax_parts(s) for s in scores]
        outs = [jnp.dot(p, vbuf[pl.ds(koff, nkeys), cols[hp]], preferred_element_type=F32) / l
                for hp, (p, l) in zip(pairs, probs)]
        for hp in pairs:
            cs = cols[hp]
            o = jnp.where(lo, outs[hp][0:GRID_W], outs[hp][GRID_W:2 * GRID_W])
            sq = o * o
            msa = jnp.sum(jnp.where(lo, sq, 0.0), axis=-1, keepdims=True) * (1.0 / NA_HEAD_DIM)
            msb = jnp.sum(jnp.where(lo, 0.0, sq), axis=-1, keepdims=True) * (1.0 / NA_HEAD_DIM)
            inv = jnp.where(lo, lax.rsqrt(msa + EPS), lax.rsqrt(msb + EPS))
            o_ref[pl.ds(qoff, GRID_W), cs] = ((o * inv) * nw).astype(BF16)
        return carry

    lax.fori_loop(0, NA_ROWS_PER_STEP, one_row, 0, unroll=8)


def _na_attention(proj, bias_tab, norm_w2, batch, seq):
    rows = seq // GRID_W
    assert rows % NA_ROWS_PER_STEP == 0 and rows >= NA_KH
    nb = rows // NA_ROWS_PER_STEP
    blk = NA_ROWS_PER_STEP * GRID_W
    n = batch * seq

    def at(col, shift):
        return pl.BlockSpec(
            (blk, W_NA), lambda b, j: (b * nb + jnp.clip(j + shift, 0, nb - 1), col))

    return pl.pallas_call(
        functools.partial(_na_body, rows=rows),
        grid=(batch, nb),
        in_specs=[
            at(0, 0),
            at(1, -1), at(1, 0), at(1, 1),
            at(2, -1), at(2, 0), at(2, 1),
            pl.BlockSpec((NA_KH, NA_HEADS // 2, 2 * GRID_W, NA_KH * GRID_W), lambda b, j: (0, 0, 0, 0)),
            pl.BlockSpec((1, LANES), lambda b, j: (0, 0)),
        ],
        out_specs=pl.BlockSpec((blk, W_NA), lambda b, j: (b * nb + j, 0)),
        out_shape=jax.ShapeDtypeStruct((n, W_NA), BF16),
        scratch_shapes=[pltpu.VMEM((3 * blk, W_NA), BF16), pltpu.VMEM((3 * blk, W_NA), BF16)],
        compiler_params=_params(("parallel", "parallel")),
        name="na_attention",
    )(proj, proj, proj, proj, proj, proj, proj, bias_tab, norm_w2)


def _t5_bucket(rel):
    nb = T5_BUCKETS // 2
    max_exact = nb // 2
    n = jnp.abs(rel)
    large = max_exact + (jnp.log(jnp.maximum(n, 1).astype(F32) / max_exact)
                         / math.log(T5_MAX_DIST / max_exact) * (nb - max_exact)).astype(jnp.int32)
    large = jnp.minimum(large, nb - 1)
    return jnp.where(rel > 0, nb, 0) + jnp.where(n < max_exact, n, large)


def _diff_bias_table(t5_table):
    x = DIFF_BLOCK
    assert x + 1 >= T5_MAX_DIST
    span = 2 * x
    table = t5_table.astype(F32) * LOG2E
    rel = (x - 1) - jnp.arange(span)[None, :] + jnp.arange(-1, 2)[:, None] * x
    diag = table[_t5_bucket(rel)].transpose(2, 0, 1)
    skew = jnp.tile(diag, (1, 1, x))[:, :, :x * (span - 1)].reshape(DIFF_HEADS, 3, x, span - 1)
    near = skew[:, :, :, x - 1:2 * x - 1]
    far = table[_t5_bucket(jnp.array([-(x + 1), x + 1]))].T
    far = jnp.broadcast_to(far[:, :, None, None], (DIFF_HEADS, 2, x, x))
    return jnp.concatenate([far[:, 0:1], near, far[:, 1:2]], axis=1)


def _diff_body(qt_ref, k_ref, vt_ref, bias_ref, lam_ref, sw_ref, o_ref, s_even, s_odd, *, nblk):
    i = pl.program_id(2)
    x = DIFF_BLOCK
    qt = qt_ref[0]
    first = lax.broadcasted_iota(jnp.int32, (LANES, x), 0) < DIFF_HALF_DIM
    zero = jnp.zeros_like(qt)
    qt_both = jnp.concatenate([jnp.where(first, qt, zero), jnp.where(first, zero, qt)], axis=1)

    def scores(j, s_ref):
        kc = k_ref[pl.ds(pl.multiple_of(j * x, x), x), :]
        s_ref[...] = jnp.dot(kc, qt_both, preferred_element_type=F32)

    ones_rows = jnp.ones((DIFF_ONES_ROWS, x), BF16)

    def consume(j, s_ref, carry, far_bias):
        vt_ext = jnp.concatenate([vt_ref[j], ones_rows], axis=0)
        out = []
        for c in range(2):
            m, acc = carry[2 * c:2 * c + 2]
            s = s_ref[:, c * x:(c + 1) * x]
            if far_bias is None:
                s = s + bias_ref[0, jnp.clip(j - i, -2, 2) + 2]
                shift = 0.0
            else:
                shift = far_bias
            m_in = m - shift
            mn = jnp.maximum(m_in, jnp.max(s, axis=0, keepdims=True))
            p = jnp.exp2(s - mn).astype(BF16)
            acc = jnp.exp2(m_in - mn) * acc + jnp.dot(vt_ext, p, preferred_element_type=F32)
            out += [mn + shift, acc]
        return tuple(out)

    def pair_loop(lo, hi, far_bias, carry):
        def body(jj, carry):
            j = 2 * jj
            scores(j + 1, s_odd)
            carry = consume(j, s_even, carry, far_bias)
            scores(j + 2, s_even)
            return consume(j + 1, s_odd, carry, far_bias)
        return lax.fori_loop(lo, hi, body, carry)

    def last_pair(far_bias, carry):
        scores(nblk - 1, s_odd)
        carry = consume(nblk - 2, s_even, carry, far_bias)
        return consume(nblk - 1, s_odd, carry, far_bias)

    npairs = nblk // 2
    near_lo = jnp.maximum((i - 1) // 2, 0)
    near_hi = jnp.minimum((i + 1) // 2, npairs - 1)
    bias_before = bias_ref[0, 0, 0:1, 0:1]
    bias_after = bias_ref[0, 4, 0:1, 0:1]
    init = (jnp.full((1, x), -jnp.inf, F32), jnp.zeros((DIFF_V_DIM + DIFF_ONES_ROWS, x), F32)) * 2
    scores(0, s_even)
    if npairs >= DIFF_SPLIT_MIN_PAIRS:
        carry = pair_loop(0, jnp.minimum(near_lo, npairs - 1), bias_before, init)
        carry = pair_loop(near_lo, jnp.minimum(near_hi + 1, npairs - 1), None, carry)
        carry = pair_loop(near_hi + 1, npairs - 1, bias_after, carry)
        _, ext1, _, ext2 = lax.cond(near_hi == npairs - 1,
                                    functools.partial(last_pair, None),
                                    functools.partial(last_pair, bias_after), carry)
    else:
        _, ext1, _, ext2 = last_pair(None, pair_loop(0, npairs - 1, None, init))
    acc1, l1 = ext1[0:DIFF_V_DIM], ext1[DIFF_V_DIM:DIFF_V_DIM + 1]
    acc2, l2 = ext2[0:DIFF_V_DIM], ext2[DIFF_V_DIM:DIFF_V_DIM + 1]
    lam = (jnp.exp(jnp.sum(lam_ref[0:1, :] * lam_ref[1:2, :], axis=-1, keepdims=True))
           - jnp.exp(jnp.sum(lam_ref[2:3, :] * lam_ref[3:4, :], axis=-1, keepdims=True)) + LAMBDA_INIT)
    o = acc1 / l1 - lam * (acc2 / l2)
    ms = jnp.mean(o * o, axis=0, keepdims=True)
    y = ((o * lax.rsqrt(ms + EPS)) * sw_ref[...]) * (1.0 - LAMBDA_INIT)
    o_ref[0] = y.astype(BF16)


def _diff_attention(proj, qd_t, vd_t, bias_tab, lam_vecs, subln_col, batch, seq):
    x = DIFF_BLOCK
    assert seq % (2 * x) == 0 and x == TOKEN_TILE
    nblk = seq // x
    n = batch * seq
    kcol = 3 * W_NA // LANES
    return pl.pallas_call(
        functools.partial(_diff_body, nblk=nblk),
        grid=(batch, DIFF_HEADS, nblk),
        in_specs=[
            pl.BlockSpec((1, LANES, x), lambda b, h, i: (b * nblk + i, h, 0)),
            pl.BlockSpec((seq, LANES), lambda b, h, i: (b, kcol + h)),
            pl.BlockSpec((nblk, LANES, x), lambda b, h, i: (b, h, 0)),
            pl.BlockSpec((1, 5, x, x), lambda b, h, i: (h, 0, 0, 0)),
            pl.BlockSpec((4, DIFF_HALF_DIM), lambda b, h, i: (0, 0)),
            pl.BlockSpec((DIFF_V_DIM, 1), lambda b, h, i: (0, 0)),
        ],
        out_specs=pl.BlockSpec((1, LANES, x), lambda b, h, i: (b * nblk + i, h, 0)),
        out_shape=jax.ShapeDtypeStruct((n // x, W_DIFF, x), BF16),
        scratch_shapes=[pltpu.VMEM((x, 2 * x), F32), pltpu.VMEM((x, 2 * x), F32)],
        compiler_params=_params(("parallel", "parallel", "parallel")),
        name="diff_attention",
    )(qd_t, proj, vd_t, bias_tab, lam_vecs, subln_col)


def _outproj_body(x_ref, ya_ref, yb_ref, wa_ref, wb_ref, nw_ref, wr_ref, x2_ref, h2_ref, aff_ref):
    attn = (jnp.dot(ya_ref[...], wa_ref[...], preferred_element_type=F32)
            + lax.dot_general(yb_ref[0], wb_ref[...], (((0,), (0,)), ((), ())), preferred_element_type=F32))
    x2 = x_ref[...] + attn
    x2_ref[...] = x2
    ms = jnp.mean(x2 * x2, axis=-1, keepdims=True)
    h2 = (x2 * lax.rsqrt(ms + EPS)) * nw_ref[...]
    for s in range(D_MODEL // LANES):
        h2_ref[:, s, :] = h2[:, s * LANES:(s + 1) * LANES]
    logits = lax.dot_general(wr_ref[...], h2.astype(BF16), _NT, preferred_element_type=F32)
    e = jnp.exp(logits - jnp.max(logits, axis=0, keepdims=True))
    aff_ref[...] = e / jnp.sum(e, axis=0, keepdims=True)


def _outproj(x2d, ya, yb, w_out, norm_w, w_router_t):
    n = x2d.shape[0]
    tm = TOKEN_TILE
    row = lambda i: (i, 0)
    fixed = lambda i: (0, 0)
    return pl.pallas_call(
        _outproj_body,
        grid=(n // tm,),
        in_specs=[
            pl.BlockSpec((tm, D_MODEL), row),
            pl.BlockSpec((tm, W_NA), row),
            pl.BlockSpec((1, W_DIFF, tm), lambda i: (i, 0, 0)),
            pl.BlockSpec((W_NA, D_MODEL), lambda i: (0, 0)),
            pl.BlockSpec((W_DIFF, D_MODEL), lambda i: (1, 0)),
            pl.BlockSpec((1, D_MODEL), fixed),
            pl.BlockSpec((N_EXPERTS, D_MODEL), fixed),
        ],
        out_specs=[
            pl.BlockSpec((tm, D_MODEL), row),
            pl.BlockSpec((tm, D_MODEL // LANES, LANES), lambda i: (i, 0, 0)),
            pl.BlockSpec((N_EXPERTS, tm), lambda i: (0, i)),
        ],
        out_shape=[
            jax.ShapeDtypeStruct((n, D_MODEL), F32),
            jax.ShapeDtypeStruct((n, D_MODEL // LANES, LANES), F32),
            jax.ShapeDtypeStruct((N_EXPERTS, n), F32),
        ],
        compiler_params=_params(("parallel",)),
        name="outproj_router",
    )(x2d, ya, yb, w_out, w_out, norm_w, w_router_t)


def _moe_body(idx_ref, idx_next_ref, h2_hbm, g_ref, wg_ref, wu_ref, wd_ref, o_ref, x_even, x_odd, sem, *, tm):
    nt = pl.num_programs(1)
    step = pl.program_id(0) * nt + pl.program_id(1)
    total = pl.num_programs(0) * nt

    def row_copy(tok, i, buf, sl):
        return pltpu.make_async_copy(h2_hbm.at[pl.ds(tok, 1)], buf.at[pl.ds(i, 1)], sem.at[sl])

    def wait_tile(buf, sl):
        pltpu.make_async_copy(h2_hbm.at[pl.ds(0, tm)], buf, sem.at[sl]).wait()

    @pl.when(step == 0)
    def _():
        def body(i, carry):
            row_copy(idx_ref[0, 0, i], i, x_even, 0).start()
            return carry
        lax.fori_loop(0, tm, body, 0, unroll=8)

    def run(cur, cur_sl, nxt, nxt_sl):
        wait_tile(cur, cur_sl)
        for i in range(tm):
            row_copy(idx_next_ref[0, 0, i], i, nxt, nxt_sl).start()
        x = jnp.concatenate([cur[:, s, :] for s in range(D_MODEL // LANES)], axis=1).astype(BF16)
        acc = jnp.zeros((tm, D_MODEL), F32)
        for f0, fc in FFN_CHUNKS:
            g = jnp.dot(x, wg_ref[0, :, f0:f0 + fc], preferred_element_type=F32)
            u = jnp.dot(x, wu_ref[0, :, f0:f0 + fc], preferred_element_type=F32)
            h = (jax.nn.silu(g) * u).astype(BF16)
            acc = acc + jnp.dot(h, wd_ref[0, f0:f0 + fc, :], preferred_element_type=F32)
        o_ref[...] = (acc * g_ref[...]).astype(o_ref.dtype)

        @pl.when(step == total - 1)
        def _():
            wait_tile(nxt, nxt_sl)

    @pl.when(step % 2 == 0)
    def _():
        run(x_even, 0, x_odd, 1)

    @pl.when(step % 2 == 1)
    def _():
        run(x_odd, 1, x_even, 0)


def _moe_experts(h2, idx, gates, w_gate, w_up, w_down):
    n_exp, cap = idx.shape
    tm = min(MOE_TILE, cap)
    assert cap % tm == 0
    nt = cap // tm
    idx3 = idx.reshape(n_exp * nt, 1, tm)

    def tile_ahead(k):
        return pl.BlockSpec((1, 1, tm), lambda e, t: (jnp.minimum(e * nt + t + k, n_exp * nt - 1), 0, 0),
                            memory_space=pltpu.SMEM)

    tile_shape = (tm, D_MODEL // LANES, LANES)
    return pl.pallas_call(
        functools.partial(_moe_body, tm=tm),
        grid=(n_exp, nt),
        in_specs=[
            tile_ahead(0), tile_ahead(1),
            pl.BlockSpec(memory_space=pl.ANY),
            pl.BlockSpec((tm, 1), lambda e, t: (e * nt + t, 0)),
            pl.BlockSpec((1, D_MODEL, D_EXPERT), lambda e, t: (e, 0, 0)),
            pl.BlockSpec((1, D_MODEL, D_EXPERT), lambda e, t: (e, 0, 0)),
            pl.BlockSpec((1, D_EXPERT, D_MODEL), lambda e, t: (e, 0, 0)),
        ],
        out_specs=pl.BlockSpec((tm, D_MODEL), lambda e, t: (e * nt + t, 0)),
        out_shape=jax.ShapeDtypeStruct((n_exp * cap, D_MODEL), BF16),
        scratch_shapes=[pltpu.VMEM(tile_shape, F32), pltpu.VMEM(tile_shape, F32), pltpu.SemaphoreType.DMA((2,))],
        compiler_params=_params(("arbitrary", "arbitrary")),
        name="moe_experts",
    )(idx3, idx3, h2, gates.reshape(n_exp * cap, 1), w_gate, w_up, w_down)


def _route_threshold_body(aff_ref, theta_ref, need_ref, *, cap):
    n_exp = aff_ref.shape[0]

    def count_ge(cand):
        bits = pltpu.bitcast(aff_ref[...], jnp.int32)
        return jnp.sum(jnp.where(bits >= cand, 1.0, 0.0), axis=1, keepdims=True)

    def step(b, theta):
        cand = theta | jnp.left_shift(jnp.int32(1), 30 - b)
        return jnp.where(count_ge(cand) >= cap, cand, theta)

    theta = lax.fori_loop(0, 31, step, jnp.zeros((n_exp, 1), jnp.int32))
    need = cap - count_ge(theta + 1)
    theta_ref[...] = jnp.broadcast_to(theta, theta_ref.shape)
    need_ref[...] = jnp.broadcast_to(need, need_ref.shape)


def _route_threshold(aff_t, cap):
    n_exp, n = aff_t.shape
    full = lambda i: (0, 0)
    return pl.pallas_call(
        functools.partial(_route_threshold_body, cap=cap),
        grid=(1,),
        in_specs=[pl.BlockSpec((n_exp, n), full)],
        out_specs=[pl.BlockSpec((n_exp, LANES), full), pl.BlockSpec((n_exp, LANES), full)],
        out_shape=[jax.ShapeDtypeStruct((n_exp, LANES), jnp.int32), jax.ShapeDtypeStruct((n_exp, LANES), F32)],
        compiler_params=_params(("arbitrary",)),
        name="route_threshold",
    )(aff_t)


def _route_prefix_body(aff_ref, tri_ref, theta_ref, need_ref, slot_ref, base_ref, run_sel, run_tie, *, cap):
    @pl.when(pl.program_id(0) == 0)
    def _():
        run_sel[...] = jnp.zeros_like(run_sel)
        run_tie[...] = jnp.zeros_like(run_tie)

    n_exp = aff_ref.shape[0]
    w = ROUTE_CHUNK
    theta = theta_ref[:, 0:1]
    need = need_ref[:, 0:1]
    expert_base = lax.broadcasted_iota(jnp.int32, (n_exp, 1), 0) * cap
    tri = tri_ref[...]
    for k in range(ROUTE_CHUNKS_PER_STEP):
        bits = pltpu.bitcast(aff_ref[:, k * w:(k + 1) * w], jnp.int32)
        tie = bits == theta
        tie_f = jnp.where(tie, 1.0, 0.0)
        tie_incl = jnp.dot(tie_f.astype(BF16), tri, preferred_element_type=F32)
        tie_rank = run_tie[:, 0:1] + tie_incl - tie_f
        sel = (bits > theta) | (tie & (tie_rank < need))
        sel_f = jnp.where(sel, 1.0, 0.0)
        incl = jnp.dot(sel_f.astype(BF16), tri, preferred_element_type=F32)
        rank = run_sel[:, 0:1] + incl - sel_f
        slot_ref[:, k * w:(k + 1) * w] = jnp.where(sel, rank.astype(jnp.int32) + expert_base, -1)
        base_ref[k] = run_sel[...].astype(jnp.int32)
        run_sel[...] = run_sel[...] + jnp.sum(sel_f, axis=1, keepdims=True)
        run_tie[...] = run_tie[...] + jnp.sum(tie_f, axis=1, keepdims=True)


def _route_prefix(aff_t, theta, need, cap):
    n_exp, n = aff_t.shape
    w = ROUTE_CHUNK
    span = w * ROUTE_CHUNKS_PER_STEP
    assert n % span == 0
    tri = (jnp.arange(w)[:, None] <= jnp.arange(w)[None, :]).astype(BF16)
    fixed = lambda i: (0, 0)
    return pl.pallas_call(
        functools.partial(_route_prefix_body, cap=cap),
        grid=(n // span,),
        in_specs=[
            pl.BlockSpec((n_exp, span), lambda i: (0, i)),
            pl.BlockSpec((w, w), fixed),
            pl.BlockSpec((n_exp, LANES), fixed),
            pl.BlockSpec((n_exp, LANES), fixed),
        ],
        out_specs=[
            pl.BlockSpec((n_exp, span), lambda i: (0, i)),
            pl.BlockSpec((ROUTE_CHUNKS_PER_STEP, n_exp, LANES), lambda i: (i, 0, 0)),
        ],
        out_shape=[
            jax.ShapeDtypeStruct((n_exp, n), jnp.int32),
            jax.ShapeDtypeStruct((n // w, n_exp, LANES), jnp.int32),
        ],
        scratch_shapes=[pltpu.VMEM((n_exp, LANES), F32), pltpu.VMEM((n_exp, LANES), F32)],
        compiler_params=_params(("arbitrary",)),
        name="route_prefix",
    )(aff_t, tri, theta, need)


def _route_compact_body(cb_ref, slot_ref, aff_ref, idx_ref, gate_ref, *, cap, nchunks):
    e = pl.program_id(0)
    w = ROUTE_CHUNK
    s = LANES
    slot_iota = lax.broadcasted_iota(jnp.int32, (s, w), 0)
    tok_iota = lax.broadcasted_iota(jnp.int32, (s, w), 1)
    eye = lax.broadcasted_iota(jnp.int32, (s, s), 0) == lax.broadcasted_iota(jnp.int32, (s, s), 1)

    def fold(v):
        out = v[:, 0:s]
        for q in range(1, w // s):
            out = out + v[:, q * s:(q + 1) * s]
        return out

    def to_row(acc):
        col = jnp.sum(acc, axis=1, keepdims=True)
        return jnp.sum(jnp.where(eye, col, 0.0), axis=0, keepdims=True)

    def one_tile(jt, c_first):
        j0 = jt * s
        c_first = lax.while_loop(lambda c: cb_ref[e, c + 1] <= j0, lambda c: c + 1, c_first)
        want = slot_iota + (e * cap + j0)

        def more(state):
            c = state[0]
            return jnp.logical_and(c < nchunks, cb_ref[e, jnp.minimum(c, nchunks - 1)] < j0 + s)

        def chunk(state):
            c, acc_i, acc_g = state
            hit = want == slot_ref[0, pl.ds(c, 1), :]
            tok = (tok_iota + c * w).astype(F32)
            acc_i = acc_i + fold(jnp.where(hit, tok, 0.0))
            acc_g = acc_g + fold(jnp.where(hit, aff_ref[0, pl.ds(c, 1), :], 0.0))
            return c + 1, acc_i, acc_g

        zero = jnp.zeros((s, s), F32)
        _, acc_i, acc_g = lax.while_loop(more, chunk, (c_first, zero, zero))
        idx_ref[0, pl.ds(jt, 1), :] = to_row(acc_i).astype(jnp.int32)
        gate_ref[0, pl.ds(jt, 1), :] = to_row(acc_g)
        return c_first

    lax.fori_loop(0, cap // s, one_tile, jnp.int32(0))


def _route_compact(chunk_base, slot3, aff3, cap):
    n_exp, nchunks, w = slot3.shape
    assert cap % LANES == 0
    per_e = lambda e, cb: (e, 0, 0)
    grid_spec = pltpu.PrefetchScalarGridSpec(
        num_scalar_prefetch=1,
        grid=(n_exp,),
        in_specs=[pl.BlockSpec((1, nchunks, w), per_e), pl.BlockSpec((1, nchunks, w), per_e)],
        out_specs=[pl.BlockSpec((1, cap // LANES, LANES), per_e), pl.BlockSpec((1, cap // LANES, LANES), per_e)],
    )
    return pl.pallas_call(
        functools.partial(_route_compact_body, cap=cap, nchunks=nchunks),
        grid_spec=grid_spec,
        out_shape=[
            jax.ShapeDtypeStruct((n_exp, cap // LANES, LANES), jnp.int32),
            jax.ShapeDtypeStruct((n_exp, cap // LANES, LANES), F32),
        ],
        compiler_params=_params(("arbitrary",)),
        name="route_compact",
    )(chunk_base, slot3, aff3)


def _combine_body(cb_ref, slot_ref, x2_ref, ye_hbm, nw_ref, o_ref, stage, chunk_row, chunk_count, sem,
                  *, cap, ntiles):
    tile = pl.program_id(0)
    cur = tile % 2
    tt = COMBINE_TILE
    ch = COMBINE_DMA_ROWS
    n_exp = slot_ref.shape[0]
    kblock = MXU_DEPTH // ch

    def chunk_copy(row, k, sl):
        return pltpu.make_async_copy(ye_hbm.at[pl.ds(row, ch), :], stage.at[sl, pl.ds(k * ch, ch), :], sem.at[sl])

    def issue(t, sl):
        def per_expert(e, k):
            lo = cb_ref[e, t]
            hi = cb_ref[e, t + 1]
            first = lo // ch
            nch = jnp.where(hi > lo, (hi - 1) // ch - first + 1, 0)

            def per_chunk(i, k):
                row = pl.multiple_of(e * cap + (first + i) * ch, ch)
                chunk_copy(row, k, sl).start()
                chunk_row[sl, k] = row
                return k + 1

            return lax.fori_loop(0, nch, per_chunk, k)

        chunk_count[sl] = lax.fori_loop(0, n_exp, per_expert, jnp.int32(0))

    @pl.when(tile == 0)
    def _():
        issue(0, 0)

    @pl.when(tile + 1 < ntiles)
    def _():
        issue(tile + 1, 1 - cur)

    nchunk = chunk_count[cur]

    def wait_one(k, carry):
        chunk_copy(0, k, cur).wait()
        return carry

    lax.fori_loop(0, nchunk, wait_one, 0)
    nblock = (nchunk + kblock - 1) // kblock

    def pad_one(k, carry):
        stage[cur, pl.ds(pl.multiple_of(k * ch, ch), ch), :] = jnp.zeros((ch, D_MODEL), BF16)
        chunk_row[cur, k] = -(1 << 30)
        return carry

    lax.fori_loop(nchunk, nblock * kblock, pad_one, 0)
    row_iota = lax.broadcasted_iota(jnp.int32, (ch, tt), 0)

    o_ref[...] = x2_ref[...]

    def one_block(b, carry):
        parts = []
        for i in range(kblock):
            row = chunk_row[cur, b * kblock + i]
            e = jnp.clip(row // cap, 0, n_exp - 1)
            hit = (row_iota + row) == slot_ref[pl.ds(e, 1), :]
            parts.append(jnp.where(hit, 1.0, 0.0).astype(BF16))
        onehot_t = jnp.concatenate(parts, axis=0)
        rows = stage[cur, pl.ds(pl.multiple_of(b * MXU_DEPTH, MXU_DEPTH), MXU_DEPTH), :]
        o_ref[...] += lax.dot_general(onehot_t, rows, (((0,), (0,)), ((), ())), preferred_element_type=F32)
        return carry

    lax.fori_loop(0, nblock, one_block, 0)
    x3 = o_ref[...]
    ms = jnp.mean(x3 * x3, axis=-1, keepdims=True)
    o_ref[...] = (x3 * lax.rsqrt(ms + EPS)) * nw_ref[...]


def _combine(tile_base, slot, x2, ye, norm_w, cap):
    n_exp, n = slot.shape
    tt = COMBINE_TILE
    ch = COMBINE_DMA_ROWS
    ntiles = n // tt
    assert cap % ch == 0 and tt % ch == 0
    max_rows = n_exp * (tt + ch)
    max_rows = -(-max_rows // MXU_DEPTH) * MXU_DEPTH
    grid_spec = pltpu.PrefetchScalarGridSpec(
        num_scalar_prefetch=1,
        grid=(ntiles,),
        in_specs=[
            pl.BlockSpec((n_exp, tt), lambda t, cb: (0, t)),
            pl.BlockSpec((tt, D_MODEL), lambda t, cb: (t, 0)),
            pl.BlockSpec(memory_space=pl.ANY),
            pl.BlockSpec((1, D_MODEL), lambda t, cb: (0, 0)),
        ],
        out_specs=pl.BlockSpec((tt, D_MODEL), lambda t, cb: (t, 0)),
        scratch_shapes=[
            pltpu.VMEM((2, max_rows, D_MODEL), BF16),
            pltpu.SMEM((2, max_rows // ch), jnp.int32),
            pltpu.SMEM((2,), jnp.int32),
            pltpu.SemaphoreType.DMA((2,)),
        ],
    )
    return pl.pallas_call(
        functools.partial(_combine_body, cap=cap, ntiles=ntiles),
        grid_spec=grid_spec,
        out_shape=jax.ShapeDtypeStruct((n, D_MODEL), F32),
        compiler_params=_params(("arbitrary",)),
        name="combine_final",
    )(tile_base, slot, x2, ye, norm_w)


def _trunk(x, p):
    batch, seq = x.shape[0], x.shape[1]
    n = batch * seq
    x2d = x.reshape(n, D_MODEL)
    proj, qd_t, vd_t = _inproj(x2d, p["mix_norm_w"], p["w_tok"], p["w_qd_t"], p["w_vd_t"])
    ya = _na_attention(proj, p["na_bias"], p["na_norm_w2"], batch, seq)
    yb_t = _diff_attention(proj, qd_t, vd_t, p["diff_bias"], p["lam_vecs"], p["subln_col"], batch, seq)
    x2, h2, aff_t = _outproj(x2d, ya, yb_t, p["w_out"], p["ffn_norm_w"], p["w_router_t"])
    cap = CAPACITY_FACTOR * n // N_EXPERTS
    theta, need = _route_threshold(aff_t, cap)
    slot, base = _route_prefix(aff_t, theta, need, cap)
    nchunks = n // ROUTE_CHUNK
    chunk_base = jnp.concatenate([base[:, :, 0].T, jnp.full((N_EXPERTS, 1), cap, jnp.int32)], axis=1)
    idx, gates = _route_compact(chunk_base, slot.reshape(N_EXPERTS, nchunks, ROUTE_CHUNK),
                                aff_t.reshape(N_EXPERTS, nchunks, ROUTE_CHUNK), cap)
    ye = _moe_experts(h2, idx.reshape(N_EXPERTS, cap), gates.reshape(N_EXPERTS, cap),
                      p["w_gate"], p["w_up"], p["w_down"])
    tile_base = chunk_base[:, ::COMBINE_TILE // ROUTE_CHUNK]
    y = _combine(tile_base, slot, x2, ye, p["final_norm_w"], cap)
    return y.reshape(batch, seq, D_MODEL)


def _prepare_params(mix_norm_w, w_in, na_rpb, na_norm_w, t5_table, lambda_q1, lambda_k1, lambda_q2, lambda_k2,
                    subln_w, w_out, ffn_norm_w, w_router, w_gate, w_up, w_down, final_norm_w):
    w = w_in[0]
    qa, ka_va = w[:, 0:W_NA] * NA_HEAD_DIM ** -0.5, w[:, W_NA:3 * W_NA]
    qd = w[:, 3 * W_NA:3 * W_NA + W_DIFF] * DIFF_HALF_DIM ** -0.5
    kd, vd = w[:, 3 * W_NA + W_DIFF:3 * W_NA + 2 * W_DIFF], w[:, 3 * W_NA + 2 * W_DIFF:]
    return {
        "mix_norm_w": mix_norm_w[0].reshape(1, D_MODEL),
        "w_tok": jnp.concatenate([qa, ka_va, kd], axis=1).astype(BF16),
        "w_qd_t": qd.T.astype(BF16),
        "w_vd_t": vd.T.astype(BF16),
        "na_bias": _na_bias_table(na_rpb[0]),
        "na_norm_w2": jnp.tile(na_norm_w[0], 2).reshape(1, LANES),
        "diff_bias": _diff_bias_table(t5_table),
        "lam_vecs": jnp.stack([lambda_q1[0], lambda_k1[0], lambda_q2[0], lambda_k2[0]]).astype(F32),
        "subln_col": subln_w[0].reshape(DIFF_V_DIM, 1),
        "w_out": w_out[0].astype(BF16),
        "ffn_norm_w": ffn_norm_w[0].reshape(1, D_MODEL),
        "w_router_t": w_router[0].T.astype(BF16),
        "w_gate": w_gate[0].astype(BF16),
        "w_up": w_up[0].astype(BF16),
        "w_down": w_down[0].astype(BF16),
        "final_norm_w": final_norm_w.reshape(1, D_MODEL),
    }


def kernel(x_prompt, x_sample, mix_norm_w, w_in, na_rpb, na_norm_w, t5_table, lambda_q1, lambda_k1,
           lambda_q2, lambda_k2, subln_w, w_out, ffn_norm_w, w_router, w_gate, w_up, w_down, final_norm_w):
    p = _prepare_params(mix_norm_w, w_in, na_rpb, na_norm_w, t5_table, lambda_q1, lambda_k1, lambda_q2,
                        lambda_k2, subln_w, w_out, ffn_norm_w, w_router, w_gate, w_up, w_down, final_norm_w)
    return (_trunk(x_prompt, p), _trunk(x_sample, p))
```

```python
import functools
import math

import jax
import jax.numpy as jnp
from jax import lax
from jax.experimental import pallas as pl
from jax.experimental.pallas import tpu as pltpu

F32 = jnp.float32
BF16 = jnp.bfloat16

D_MODEL = 1024
GRID_W = 64
NA_HEADS = 8
NA_HEAD_DIM = 64
NA_KH = 8
NA_KW = 16
DIFF_HEADS = 4
DIFF_HALF_DIM = 64
DIFF_V_DIM = 128
W_NA = 512
W_DIFF = 512
D_IN = 3072
T5_BUCKETS = 32
T5_MAX_DIST = 128
N_EXPERTS = 16
CAPACITY_FACTOR = 2
D_EXPERT = 2816
EPS = 1e-6
LAMBDA_INIT = 0.8 - 0.6 * math.exp(-0.3 * 0)
LOG2E = math.log2(math.e)
TOKEN_MAJOR_COLS = 3 * W_NA + W_DIFF

LANES = 128
MXU_DEPTH = 256
TOKEN_TILE = 512
NA_ROWS_PER_STEP = 8
DIFF_BLOCK = 512
DIFF_ONES_ROWS = 16
DIFF_SPLIT_MIN_PAIRS = 4
MOE_TILE = 512
MOE_GATHER_AT_CHUNK = 2
ROUTE_CHUNK = 256
ROUTE_CHUNKS_PER_STEP = 8
COMBINE_TILE = 512
COMBINE_DMA_ROWS = 32
FFN_CHUNKS = ((0, 512), (512, 512), (1024, 512), (1536, 512), (2048, 512), (2560, 256))
MASKED = -1e30
VMEM_LIMIT = 56 * 1024 * 1024

_NT = (((1,), (1,)), ((), ()))


def _params(sem, vmem=VMEM_LIMIT):
    return pltpu.CompilerParams(dimension_semantics=sem, vmem_limit_bytes=vmem)


def _inproj_body(x_ref, nw_ref, w_ref, wqt_ref, wvt_ref, o_ref, qt_ref, vt_ref):
    x = x_ref[...]
    ms = jnp.mean(x * x, axis=-1, keepdims=True)
    h = ((x * lax.rsqrt(ms + EPS)) * nw_ref[...]).astype(BF16)
    for c in range(TOKEN_MAJOR_COLS // W_NA):
        cols = slice(c * W_NA, (c + 1) * W_NA)
        o_ref[:, cols] = jnp.dot(h, w_ref[:, cols], preferred_element_type=F32).astype(BF16)
    qt = lax.dot_general(wqt_ref[...], h, _NT, preferred_element_type=F32)
    qt_ref[0] = (qt * LOG2E).astype(BF16)
    vt_ref[0] = lax.dot_general(wvt_ref[...], h, _NT, preferred_element_type=F32).astype(BF16)


def _inproj(x2d, norm_w, w_tok, w_qd_t, w_vd_t):
    n = x2d.shape[0]
    tm = TOKEN_TILE
    fixed = lambda i: (0, 0)
    return pl.pallas_call(
        _inproj_body,
        grid=(n // tm,),
        in_specs=[
            pl.BlockSpec((tm, D_MODEL), lambda i: (i, 0)),
            pl.BlockSpec((1, D_MODEL), fixed),
            pl.BlockSpec((D_MODEL, TOKEN_MAJOR_COLS), fixed),
            pl.BlockSpec((W_DIFF, D_MODEL), fixed),
            pl.BlockSpec((W_DIFF, D_MODEL), fixed),
        ],
        out_specs=[
            pl.BlockSpec((tm, TOKEN_MAJOR_COLS), lambda i: (i, 0)),
            pl.BlockSpec((1, W_DIFF, tm), lambda i: (i, 0, 0)),
            pl.BlockSpec((1, W_DIFF, tm), lambda i: (i, 0, 0)),
        ],
        out_shape=[
            jax.ShapeDtypeStruct((n, TOKEN_MAJOR_COLS), BF16),
            jax.ShapeDtypeStruct((n // tm, W_DIFF, tm), BF16),
            jax.ShapeDtypeStruct((n // tm, W_DIFF, tm), BF16),
        ],
        compiler_params=_params(("parallel",)),
        name="inproj",
    )(x2d, norm_w, w_tok, w_qd_t, w_vd_t)


def _na_bias_table(rpb):
    c = jnp.arange(GRID_W)
    col_start = jnp.clip(c - NA_KW // 2, 0, GRID_W - NA_KW)
    in_win = (c[None, :] >= col_start[:, None]) & (c[None, :] < col_start[:, None] + NA_KW)
    dc = jnp.clip(c[None, :] - c[:, None], -(NA_KW - 1), NA_KW - 1) + (NA_KW - 1)
    bc = rpb.astype(F32)[:, :, dc]
    bc = jnp.where(in_win[None, None], bc, MASKED)
    tabs = jnp.stack([bc[:, d0:d0 + NA_KH] for d0 in range(NA_KH)])
    tabs = tabs.transpose(0, 1, 3, 2, 4)
    return tabs.reshape(NA_KH, NA_HEADS // 2, 2 * GRID_W, NA_KH * GRID_W)


def _na_body(q_ref, kp_ref, kc_ref, kn_ref, vp_ref, vc_ref, vn_ref, bias_ref, nw_ref, o_ref,
             kbuf, vbuf, *, rows):
    j = pl.program_id(1)
    blk = NA_ROWS_PER_STEP * GRID_W
    kbuf[0:blk] = kp_ref[...]
    kbuf[blk:2 * blk] = kc_ref[...]
    kbuf[2 * blk:3 * blk] = kn_ref[...]
    vbuf[0:blk] = vp_ref[...]
    vbuf[blk:2 * blk] = vc_ref[...]
    vbuf[2 * blk:3 * blk] = vn_ref[...]
    lo = lax.broadcasted_iota(jnp.int32, (GRID_W, LANES), 1) < NA_HEAD_DIM
    nw = nw_ref[...]
    nkeys = NA_KH * GRID_W

    def softmax_parts(s):
        m = jnp.max(s, axis=-1, keepdims=True)
        p = jnp.exp(s - m)
        return p.astype(BF16), jnp.sum(p, axis=-1, keepdims=True)

    def one_row(a, carry):
        r = j * NA_ROWS_PER_STEP + a
        row_start = jnp.clip(r - NA_KH // 2, 0, rows - NA_KH)
        d0 = row_start - r + (NA_KH - 1)
        koff = pl.multiple_of((row_start - (j - 1) * NA_ROWS_PER_STEP) * GRID_W, GRID_W)
        qoff = pl.multiple_of(a * GRID_W, GRID_W)
        pairs = range(NA_HEADS // 2)
        cols = [slice(hp * LANES, (hp + 1) * LANES) for hp in pairs]
        scores = []
        for hp in pairs:
            qp = q_ref[pl.ds(qoff, GRID_W), cols[hp]]
            zero = jnp.zeros_like(qp)
            q2 = jnp.concatenate([jnp.where(lo, qp, zero), jnp.where(lo, zero, qp)], axis=0)
            kk = kbuf[pl.ds(koff, nkeys), cols[hp]]
            scores.append(lax.dot_general(q2, kk, _NT, preferred_element_type=F32) + bias_ref[d0, hp])
        probs = [softmax_parts(s) for s in scores]
        outs = [jnp.dot(p, vbuf[pl.ds(koff, nkeys), cols[hp]], preferred_element_type=F32) / l
                for hp, (p, l) in zip(pairs, probs)]
        for hp in pairs:
            cs = cols[hp]
            o = jnp.where(lo, outs[hp][0:GRID_W], outs[hp][GRID_W:2 * GRID_W])
            sq = o * o
            msa = jnp.sum(jnp.where(lo, sq, 0.0), axis=-1, keepdims=True) * (1.0 / NA_HEAD_DIM)
            msb = jnp.sum(jnp.where(lo, 0.0, sq), axis=-1, keepdims=True) * (1.0 / NA_HEAD_DIM)
            inv = jnp.where(lo, lax.rsqrt(msa + EPS), lax.rsqrt(msb + EPS))
            o_ref[pl.ds(qoff, GRID_W), cs] = ((o * inv) * nw).astype(BF16)
        return carry

    lax.fori_loop(0, NA_ROWS_PER_STEP, one_row, 0, unroll=8)


def _na_attention(proj, bias_tab, norm_w2, batch, seq):
    rows = seq // GRID_W
    assert rows % NA_ROWS_PER_STEP == 0 and rows >= NA_KH
    nb = rows // NA_ROWS_PER_STEP
    blk = NA_ROWS_PER_STEP * GRID_W
    n = batch * seq

    def at(col, shift):
        return pl.BlockSpec(
            (blk, W_NA), lambda b, j: (b * nb + jnp.clip(j + shift, 0, nb - 1), col))

    return pl.pallas_call(
        functools.partial(_na_body, rows=rows),
        grid=(batch, nb),
        in_specs=[
            at(0, 0),
            at(1, -1), at(1, 0), at(1, 1),
            at(2, -1), at(2, 0), at(2, 1),
            pl.BlockSpec((NA_KH, NA_HEADS // 2, 2 * GRID_W, NA_KH * GRID_W), lambda b, j: (0, 0, 0, 0)),
            pl.BlockSpec((1, LANES), lambda b, j: (0, 0)),
        ],
        out_specs=pl.BlockSpec((blk, W_NA), lambda b, j: (b * nb + j, 0)),
        out_shape=jax.ShapeDtypeStruct((n, W_NA), BF16),
        scratch_shapes=[pltpu.VMEM((3 * blk, W_NA), BF16), pltpu.VMEM((3 * blk, W_NA), BF16)],
        compiler_params=_params(("parallel", "parallel")),
        name="na_attention",
    )(proj, proj, proj, proj, proj, proj, proj, bias_tab, norm_w2)


def _t5_bucket(rel):
    nb = T5_BUCKETS // 2
    max_exact = nb // 2
    n = jnp.abs(rel)
    large = max_exact + (jnp.log(jnp.maximum(n, 1).astype(F32) / max_exact)
                         / math.log(T5_MAX_DIST / max_exact) * (nb - max_exact)).astype(jnp.int32)
    large = jnp.minimum(large, nb - 1)
    return jnp.where(rel > 0, nb, 0) + jnp.where(n < max_exact, n, large)


def _diff_bias_table(t5_table):
    x = DIFF_BLOCK
    assert x + 1 >= T5_MAX_DIST
    span = 2 * x
    table = t5_table.astype(F32) * LOG2E
    rel = (x - 1) - jnp.arange(span)[None, :] + jnp.arange(-1, 2)[:, None] * x
    diag = table[_t5_bucket(rel)].transpose(2, 0, 1)
    skew = jnp.tile(diag, (1, 1, x))[:, :, :x * (span - 1)].reshape(DIFF_HEADS, 3, x, span - 1)
    near = skew[:, :, :, x - 1:2 * x - 1]
    far = table[_t5_bucket(jnp.array([-(x + 1), x + 1]))].T
    far = jnp.broadcast_to(far[:, :, None, None], (DIFF_HEADS, 2, x, x))
    return jnp.concatenate([far[:, 0:1], near, far[:, 1:2]], axis=1)


def _diff_body(qt_ref, k_ref, vt_ref, bias_ref, lam_ref, sw_ref, o_ref, s_even, s_odd, *, nblk):
    i = pl.program_id(2)
    x = DIFF_BLOCK
    qt = qt_ref[0]
    first = lax.broadcasted_iota(jnp.int32, (LANES, x), 0) < DIFF_HALF_DIM
    zero = jnp.zeros_like(qt)
    qt_both = jnp.concatenate([jnp.where(first, qt, zero), jnp.where(first, zero, qt)], axis=1)

    def scores(j, s_ref):
        kc = k_ref[pl.ds(pl.multiple_of(j * x, x), x), :]
        s_ref[...] = jnp.dot(kc, qt_both, preferred_element_type=F32)

    ones_rows = jnp.ones((DIFF_ONES_ROWS, x), BF16)

    def consume(j, s_ref, carry, far_bias):
        vt_ext = jnp.concatenate([vt_ref[j], ones_rows], axis=0)
        out = []
        for c in range(2):
            m, acc = carry[2 * c:2 * c + 2]
            s = s_ref[:, c * x:(c + 1) * x]
            if far_bias is None:
                s = s + bias_ref[0, jnp.clip(j - i, -2, 2) + 2]
                shift = 0.0
            else:
                shift = far_bias
            m_in = m - shift
            mn = jnp.maximum(m_in, jnp.max(s, axis=0, keepdims=True))
            p = jnp.exp2(s - mn).astype(BF16)
            acc = jnp.exp2(m_in - mn) * acc + jnp.dot(vt_ext, p, preferred_element_type=F32)
            out += [mn + shift, acc]
        return tuple(out)

    def pair_loop(lo, hi, far_bias, carry):
        def body(jj, carry):
            j = 2 * jj
            scores(j + 1, s_odd)
            carry = consume(j, s_even, carry, far_bias)
            scores(j + 2, s_even)
            return consume(j + 1, s_odd, carry, far_bias)
        return lax.fori_loop(lo, hi, body, carry)

    def last_pair(far_bias, carry):
        scores(nblk - 1, s_odd)
        carry = consume(nblk - 2, s_even, carry, far_bias)
        return consume(nblk - 1, s_odd, carry, far_bias)

    npairs = nblk // 2
    near_lo = jnp.maximum((i - 1) // 2, 0)
    near_hi = jnp.minimum((i + 1) // 2, npairs - 1)
    bias_before = bias_ref[0, 0, 0:1, 0:1]
    bias_after = bias_ref[0, 4, 0:1, 0:1]
    init = (jnp.full((1, x), -jnp.inf, F32), jnp.zeros((DIFF_V_DIM + DIFF_ONES_ROWS, x), F32)) * 2
    scores(0, s_even)
    if npairs >= DIFF_SPLIT_MIN_PAIRS:
        carry = pair_loop(0, jnp.minimum(near_lo, npairs - 1), bias_before, init)
        carry = pair_loop(near_lo, jnp.minimum(near_hi + 1, npairs - 1), None, carry)
        carry = pair_loop(near_hi + 1, npairs - 1, bias_after, carry)
        _, ext1, _, ext2 = lax.cond(near_hi == npairs - 1,
                                    functools.partial(last_pair, None),
                                    functools.partial(last_pair, bias_after), carry)
    else:
        _, ext1, _, ext2 = last_pair(None, pair_loop(0, npairs - 1, None, init))
    acc1, l1 = ext1[0:DIFF_V_DIM], ext1[DIFF_V_DIM:DIFF_V_DIM + 1]
    acc2, l2 = ext2[0:DIFF_V_DIM], ext2[DIFF_V_DIM:DIFF_V_DIM + 1]
    lam = (jnp.exp(jnp.sum(lam_ref[0:1, :] * lam_ref[1:2, :], axis=-1, keepdims=True))
           - jnp.exp(jnp.sum(lam_ref[2:3, :] * lam_ref[3:4, :], axis=-1, keepdims=True)) + LAMBDA_INIT)
    o = acc1 / l1 - lam * (acc2 / l2)
    ms = jnp.mean(o * o, axis=0, keepdims=True)
    y = ((o * lax.rsqrt(ms + EPS)) * sw_ref[...]) * (1.0 - LAMBDA_INIT)
    o_ref[0] = y.astype(BF16)


def _diff_attention(proj, qd_t, vd_t, bias_tab, lam_vecs, subln_col, batch, seq):
    x = DIFF_BLOCK
    assert seq % (2 * x) == 0 and x == TOKEN_TILE
    nblk = seq // x
    n = batch * seq
    kcol = 3 * W_NA // LANES
    return pl.pallas_call(
        functools.partial(_diff_body, nblk=nblk),
        grid=(batch, DIFF_HEADS, nblk),
        in_specs=[
            pl.BlockSpec((1, LANES, x), lambda b, h, i: (b * nblk + i, h, 0)),
            pl.BlockSpec((seq, LANES), lambda b, h, i: (b, kcol + h)),
            pl.BlockSpec((nblk, LANES, x), lambda b, h, i: (b, h, 0)),
            pl.BlockSpec((1, 5, x, x), lambda b, h, i: (h, 0, 0, 0)),
            pl.BlockSpec((4, DIFF_HALF_DIM), lambda b, h, i: (0, 0)),
            pl.BlockSpec((DIFF_V_DIM, 1), lambda b, h, i: (0, 0)),
        ],
        out_specs=pl.BlockSpec((1, LANES, x), lambda b, h, i: (b * nblk + i, h, 0)),
        out_shape=jax.ShapeDtypeStruct((n // x, W_DIFF, x), BF16),
        scratch_shapes=[pltpu.VMEM((x, 2 * x), F32), pltpu.VMEM((x, 2 * x), F32)],
        compiler_params=_params(("parallel", "parallel", "parallel")),
        name="diff_attention",
    )(qd_t, proj, vd_t, bias_tab, lam_vecs, subln_col)


def _outproj_body(x_ref, ya_ref, yb_ref, wa_ref, wb_ref, nw_ref, wr_ref, x2_ref, h2_ref, aff_ref):
    attn = (jnp.dot(ya_ref[...], wa_ref[...], preferred_element_type=F32)
            + lax.dot_general(yb_ref[0], wb_ref[...], (((0,), (0,)), ((), ())), preferred_element_type=F32))
    x2 = x_ref[...] + attn
    x2_ref[...] = x2
    ms = jnp.mean(x2 * x2, axis=-1, keepdims=True)
    h2 = (x2 * lax.rsqrt(ms + EPS)) * nw_ref[...]
    for s in range(D_MODEL // LANES):
        h2_ref[:, s, :] = h2[:, s * LANES:(s + 1) * LANES]
    logits = lax.dot_general(wr_ref[...], h2.astype(BF16), _NT, preferred_element_type=F32)
    e = jnp.exp(logits - jnp.max(logits, axis=0, keepdims=True))
    aff_ref[...] = e / jnp.sum(e, axis=0, keepdims=True)


def _outproj(x2d, ya, yb, w_out, norm_w, w_router_t):
    n = x2d.shape[0]
    tm = TOKEN_TILE
    row = lambda i: (i, 0)
    fixed = lambda i: (0, 0)
    return pl.pallas_call(
        _outproj_body,
        grid=(n // tm,),
        in_specs=[
            pl.BlockSpec((tm, D_MODEL), row),
            pl.BlockSpec((tm, W_NA), row),
            pl.BlockSpec((1, W_DIFF, tm), lambda i: (i, 0, 0)),
            pl.BlockSpec((W_NA, D_MODEL), lambda i: (0, 0)),
            pl.BlockSpec((W_DIFF, D_MODEL), lambda i: (1, 0)),
            pl.BlockSpec((1, D_MODEL), fixed),
            pl.BlockSpec((N_EXPERTS, D_MODEL), fixed),
        ],
        out_specs=[
            pl.BlockSpec((tm, D_MODEL), row),
            pl.BlockSpec((tm, D_MODEL // LANES, LANES), lambda i: (i, 0, 0)),
            pl.BlockSpec((N_EXPERTS, tm), lambda i: (0, i)),
        ],
        out_shape=[
            jax.ShapeDtypeStruct((n, D_MODEL), F32),
            jax.ShapeDtypeStruct((n, D_MODEL // LANES, LANES), F32),
            jax.ShapeDtypeStruct((N_EXPERTS, n), F32),
        ],
        compiler_params=_params(("parallel",)),
        name="outproj_router",
    )(x2d, ya, yb, w_out, w_out, norm_w, w_router_t)


def _moe_body(idx_ref, idx_next_ref, h2_hbm, g_ref, wg_ref, wu_ref, wd_ref, o_ref, x_even, x_odd, sem, *, tm):
    nt = pl.num_programs(1)
    step = pl.program_id(0) * nt + pl.program_id(1)
    total = pl.num_programs(0) * nt

    def row_copy(tok, i, buf, sl):
        return pltpu.make_async_copy(h2_hbm.at[pl.ds(tok, 1)], buf.at[pl.ds(i, 1)], sem.at[sl])

    def wait_tile(buf, sl):
        pltpu.make_async_copy(h2_hbm.at[pl.ds(0, tm)], buf, sem.at[sl]).wait()

    @pl.when(step == 0)
    def _():
        def body(i, carry):
            row_copy(idx_ref[0, 0, i], i, x_even, 0).start()
            return carry
        lax.fori_loop(0, tm, body, 0, unroll=8)

    def run(cur, cur_sl, nxt, nxt_sl):
        wait_tile(cur, cur_sl)
        x = jnp.concatenate([cur[:, s, :] for s in range(D_MODEL // LANES)], axis=1).astype(BF16)
        acc = jnp.zeros((tm, D_MODEL), F32)
        for ci, (f0, fc) in enumerate(FFN_CHUNKS):
            if ci == MOE_GATHER_AT_CHUNK:
                @pl.when(step >= 0)
                def _():
                    for i in range(tm):
                        row_copy(idx_next_ref[0, 0, i], i, nxt, nxt_sl).start()
            g = jnp.dot(x, wg_ref[0, :, f0:f0 + fc], preferred_element_type=F32)
            u = jnp.dot(x, wu_ref[0, :, f0:f0 + fc], preferred_element_type=F32)
            h = (jax.nn.silu(g) * u).astype(BF16)
            acc = acc + jnp.dot(h, wd_ref[0, f0:f0 + fc, :], preferred_element_type=F32)
        o_ref[...] = (acc * g_ref[...]).astype(o_ref.dtype)

        @pl.when(step == total - 1)
        def _():
            wait_tile(nxt, nxt_sl)

    @pl.when(step % 2 == 0)
    def _():
        run(x_even, 0, x_odd, 1)

    @pl.when(step % 2 == 1)
    def _():
        run(x_odd, 1, x_even, 0)


def _moe_experts(h2, idx, gates, w_gate, w_up, w_down):
    n_exp, cap = idx.shape
    tm = min(MOE_TILE, cap)
    assert cap % tm == 0
    nt = cap // tm
    idx3 = idx.reshape(n_exp * nt, 1, tm)

    def tile_ahead(k):
        return pl.BlockSpec((1, 1, tm), lambda e, t: (jnp.minimum(e * nt + t + k, n_exp * nt - 1), 0, 0),
                            memory_space=pltpu.SMEM)

    tile_shape = (tm, D_MODEL // LANES, LANES)
    return pl.pallas_call(
        functools.partial(_moe_body, tm=tm),
        grid=(n_exp, nt),
        in_specs=[
            tile_ahead(0), tile_ahead(1),
            pl.BlockSpec(memory_space=pl.ANY),
            pl.BlockSpec((tm, 1), lambda e, t: (e * nt + t, 0)),
            pl.BlockSpec((1, D_MODEL, D_EXPERT), lambda e, t: (e, 0, 0)),
            pl.BlockSpec((1, D_MODEL, D_EXPERT), lambda e, t: (e, 0, 0)),
            pl.BlockSpec((1, D_EXPERT, D_MODEL), lambda e, t: (e, 0, 0)),
        ],
        out_specs=pl.BlockSpec((tm, D_MODEL), lambda e, t: (e * nt + t, 0)),
        out_shape=jax.ShapeDtypeStruct((n_exp * cap, D_MODEL), BF16),
        scratch_shapes=[pltpu.VMEM(tile_shape, F32), pltpu.VMEM(tile_shape, F32), pltpu.SemaphoreType.DMA((2,))],
        compiler_params=_params(("arbitrary", "arbitrary")),
        name="moe_experts",
    )(idx3, idx3, h2, gates.reshape(n_exp * cap, 1), w_gate, w_up, w_down)


def _route_threshold_body(aff_ref, theta_ref, need_ref, *, cap):
    n_exp = aff_ref.shape[0]

    def count_ge(cand):
        bits = pltpu.bitcast(aff_ref[...], jnp.int32)
        return jnp.sum(jnp.where(bits >= cand, 1.0, 0.0), axis=1, keepdims=True)

    def step(b, theta):
        cand = theta | jnp.left_shift(jnp.int32(1), 30 - b)
        return jnp.where(count_ge(cand) >= cap, cand, theta)

    theta = lax.fori_loop(0, 31, step, jnp.zeros((n_exp, 1), jnp.int32))
    need = cap - count_ge(theta + 1)
    theta_ref[...] = jnp.broadcast_to(theta, theta_ref.shape)
    need_ref[...] = jnp.broadcast_to(need, need_ref.shape)


def _route_threshold(aff_t, cap):
    n_exp, n = aff_t.shape
    full = lambda i: (0, 0)
    return pl.pallas_call(
        functools.partial(_route_threshold_body, cap=cap),
        grid=(1,),
        in_specs=[pl.BlockSpec((n_exp, n), full)],
        out_specs=[pl.BlockSpec((n_exp, LANES), full), pl.BlockSpec((n_exp, LANES), full)],
        out_shape=[jax.ShapeDtypeStruct((n_exp, LANES), jnp.int32), jax.ShapeDtypeStruct((n_exp, LANES), F32)],
        compiler_params=_params(("arbitrary",)),
        name="route_threshold",
    )(aff_t)


def _route_prefix_body(aff_ref, tri_ref, theta_ref, need_ref, slot_ref, base_ref, run_sel, run_tie, *, cap):
    @pl.when(pl.program_id(0) == 0)
    def _():
        run_sel[...] = jnp.zeros_like(run_sel)
        run_tie[...] = jnp.zeros_like(run_tie)

    n_exp = aff_ref.shape[0]
    w = ROUTE_CHUNK
    theta = theta_ref[:, 0:1]
    need = need_ref[:, 0:1]
    expert_base = lax.broadcasted_iota(jnp.int32, (n_exp, 1), 0) * cap
    tri = tri_ref[...]
    for k in range(ROUTE_CHUNKS_PER_STEP):
        bits = pltpu.bitcast(aff_ref[:, k * w:(k + 1) * w], jnp.int32)
        tie = bits == theta
        tie_f = jnp.where(tie, 1.0, 0.0)
        tie_incl = jnp.dot(tie_f.astype(BF16), tri, preferred_element_type=F32)
        tie_rank = run_tie[:, 0:1] + tie_incl - tie_f
        sel = (bits > theta) | (tie & (tie_rank < need))
        sel_f = jnp.where(sel, 1.0, 0.0)
        incl = jnp.dot(sel_f.astype(BF16), tri, preferred_element_type=F32)
        rank = run_sel[:, 0:1] + incl - sel_f
        slot_ref[:, k * w:(k + 1) * w] = jnp.where(sel, rank.astype(jnp.int32) + expert_base, -1)
        base_ref[k] = run_sel[...].astype(jnp.int32)
        run_sel[...] = run_sel[...] + jnp.sum(sel_f, axis=1, keepdims=True)
        run_tie[...] = run_tie[...] + jnp.sum(tie_f, axis=1, keepdims=True)


def _route_prefix(aff_t, theta, need, cap):
    n_exp, n = aff_t.shape
    w = ROUTE_CHUNK
    span = w * ROUTE_CHUNKS_PER_STEP
    assert n % span == 0
    tri = (jnp.arange(w)[:, None] <= jnp.arange(w)[None, :]).astype(BF16)
    fixed = lambda i: (0, 0)
    return pl.pallas_call(
        functools.partial(_route_prefix_body, cap=cap),
        grid=(n // span,),
        in_specs=[
            pl.BlockSpec((n_exp, span), lambda i: (0, i)),
            pl.BlockSpec((w, w), fixed),
            pl.BlockSpec((n_exp, LANES), fixed),
            pl.BlockSpec((n_exp, LANES), fixed),
        ],
        out_specs=[
            pl.BlockSpec((n_exp, span), lambda i: (0, i)),
            pl.BlockSpec((ROUTE_CHUNKS_PER_STEP, n_exp, LANES), lambda i: (i, 0, 0)),
        ],
        out_shape=[
            jax.ShapeDtypeStruct((n_exp, n), jnp.int32),
            jax.ShapeDtypeStruct((n // w, n_exp, LANES), jnp.int32),
        ],
        scratch_shapes=[pltpu.VMEM((n_exp, LANES), F32), pltpu.VMEM((n_exp, LANES), F32)],
        compiler_params=_params(("arbitrary",)),
        name="route_prefix",
    )(aff_t, tri, theta, need)


def _route_compact_body(cb_ref, slot_ref, aff_ref, idx_ref, gate_ref, *, cap, nchunks):
    e = pl.program_id(0)
    w = ROUTE_CHUNK
    s = LANES
    slot_iota = lax.broadcasted_iota(jnp.int32, (s, w), 0)
    tok_iota = lax.broadcasted_iota(jnp.int32, (s, w), 1)
    eye = lax.broadcasted_iota(jnp.int32, (s, s), 0) == lax.broadcasted_iota(jnp.int32, (s, s), 1)

    def fold(v):
        out = v[:, 0:s]
        for q in range(1, w // s):
            out = out + v[:, q * s:(q + 1) * s]
        return out

    def to_row(acc):
        col = jnp.sum(acc, axis=1, keepdims=True)
        return jnp.sum(jnp.where(eye, col, 0.0), axis=0, keepdims=True)

    def one_tile(jt, c_first):
        j0 = jt * s
        c_first = lax.while_loop(lambda c: cb_ref[e, c + 1] <= j0, lambda c: c + 1, c_first)
        want = slot_iota + (e * cap + j0)

        def more(state):
            c = state[0]
            return jnp.logical_and(c < nchunks, cb_ref[e, jnp.minimum(c, nchunks - 1)] < j0 + s)

        def chunk(state):
            c, acc_i, acc_g = state
            hit = want == slot_ref[0, pl.ds(c, 1), :]
            tok = (tok_iota + c * w).astype(F32)
            acc_i = acc_i + fold(jnp.where(hit, tok, 0.0))
            acc_g = acc_g + fold(jnp.where(hit, aff_ref[0, pl.ds(c, 1), :], 0.0))
            return c + 1, acc_i, acc_g

        zero = jnp.zeros((s, s), F32)
        _, acc_i, acc_g = lax.while_loop(more, chunk, (c_first, zero, zero))
        idx_ref[0, pl.ds(jt, 1), :] = to_row(acc_i).astype(jnp.int32)
        gate_ref[0, pl.ds(jt, 1), :] = to_row(acc_g)
        return c_first

    lax.fori_loop(0, cap // s, one_tile, jnp.int32(0))


def _route_compact(chunk_base, slot3, aff3, cap):
    n_exp, nchunks, w = slot3.shape
    assert cap % LANES == 0
    per_e = lambda e, cb: (e, 0, 0)
    grid_spec = pltpu.PrefetchScalarGridSpec(
        num_scalar_prefetch=1,
        grid=(n_exp,),
        in_specs=[pl.BlockSpec((1, nchunks, w), per_e), pl.BlockSpec((1, nchunks, w), per_e)],
        out_specs=[pl.BlockSpec((1, cap // LANES, LANES), per_e), pl.BlockSpec((1, cap // LANES, LANES), per_e)],
    )
    return pl.pallas_call(
        functools.partial(_route_compact_body, cap=cap, nchunks=nchunks),
        grid_spec=grid_spec,
        out_shape=[
            jax.ShapeDtypeStruct((n_exp, cap // LANES, LANES), jnp.int32),
            jax.ShapeDtypeStruct((n_exp, cap // LANES, LANES), F32),
        ],
        compiler_params=_params(("arbitrary",)),
        name="route_compact",
    )(chunk_base, slot3, aff3)


def _combine_body(cb_ref, slot_ref, x2_ref, ye_hbm, nw_ref, o_ref, stage, chunk_row, chunk_count, sem,
                  *, cap, ntiles):
    tile = pl.program_id(0)
    cur = tile % 2
    tt = COMBINE_TILE
    ch = COMBINE_DMA_ROWS
    n_exp = slot_ref.shape[0]
    kblock = MXU_DEPTH // ch

    def chunk_copy(row, k, sl):
        return pltpu.make_async_copy(ye_hbm.at[pl.ds(row, ch), :], stage.at[sl, pl.ds(k * ch, ch), :], sem.at[sl])

    def issue(t, sl):
        def per_expert(e, k):
            lo = cb_ref[e, t]
            hi = cb_ref[e, t + 1]
            first = lo // ch
            nch = jnp.where(hi > lo, (hi - 1) // ch - first + 1, 0)

            def per_chunk(i, k):
                row = pl.multiple_of(e * cap + (first + i) * ch, ch)
                chunk_copy(row, k, sl).start()
                chunk_row[sl, k] = row
                return k + 1

            return lax.fori_loop(0, nch, per_chunk, k)

        chunk_count[sl] = lax.fori_loop(0, n_exp, per_expert, jnp.int32(0))

    @pl.when(tile == 0)
    def _():
        issue(0, 0)

    @pl.when(tile + 1 < ntiles)
    def _():
        issue(tile + 1, 1 - cur)

    nchunk = chunk_count[cur]

    def wait_one(k, carry):
        chunk_copy(0, k, cur).wait()
        return carry

    lax.fori_loop(0, nchunk, wait_one, 0)
    nblock = (nchunk + kblock - 1) // kblock

    def pad_one(k, carry):
        stage[cur, pl.ds(pl.multiple_of(k * ch, ch), ch), :] = jnp.zeros((ch, D_MODEL), BF16)
        chunk_row[cur, k] = -(1 << 30)
        return carry

    lax.fori_loop(nchunk, nblock * kblock, pad_one, 0)
    row_iota = lax.broadcasted_iota(jnp.int32, (ch, tt), 0)

    o_ref[...] = x2_ref[...]

    def one_block(b, carry):
        parts = []
        for i in range(kblock):
            row = chunk_row[cur, b * kblock + i]
            e = jnp.clip(row // cap, 0, n_exp - 1)
            hit = (row_iota + row) == slot_ref[pl.ds(e, 1), :]
            parts.append(jnp.where(hit, 1.0, 0.0).astype(BF16))
        onehot_t = jnp.concatenate(parts, axis=0)
        rows = stage[cur, pl.ds(pl.multiple_of(b * MXU_DEPTH, MXU_DEPTH), MXU_DEPTH), :]
        o_ref[...] += lax.dot_general(onehot_t, rows, (((0,), (0,)), ((), ())), preferred_element_type=F32)
        return carry

    lax.fori_loop(0, nblock, one_block, 0)
    x3 = o_ref[...]
    ms = jnp.mean(x3 * x3, axis=-1, keepdims=True)
    o_ref[...] = (x3 * lax.rsqrt(ms + EPS)) * nw_ref[...]


def _combine(tile_base, slot, x2, ye, norm_w, cap):
    n_exp, n = slot.shape
    tt = COMBINE_TILE
    ch = COMBINE_DMA_ROWS
    ntiles = n // tt
    assert cap % ch == 0 and tt % ch == 0
    max_rows = n_exp * (tt + ch)
    max_rows = -(-max_rows // MXU_DEPTH) * MXU_DEPTH
    grid_spec = pltpu.PrefetchScalarGridSpec(
        num_scalar_prefetch=1,
        grid=(ntiles,),
        in_specs=[
            pl.BlockSpec((n_exp, tt), lambda t, cb: (0, t)),
            pl.BlockSpec((tt, D_MODEL), lambda t, cb: (t, 0)),
            pl.BlockSpec(memory_space=pl.ANY),
            pl.BlockSpec((1, D_MODEL), lambda t, cb: (0, 0)),
        ],
        out_specs=pl.BlockSpec((tt, D_MODEL), lambda t, cb: (t, 0)),
        scratch_shapes=[
            pltpu.VMEM((2, max_rows, D_MODEL), BF16),
            pltpu.SMEM((2, max_rows // ch), jnp.int32),
            pltpu.SMEM((2,), jnp.int32),
            pltpu.SemaphoreType.DMA((2,)),
        ],
    )
    return pl.pallas_call(
        functools.partial(_combine_body, cap=cap, ntiles=ntiles),
        grid_spec=grid_spec,
        out_shape=jax.ShapeDtypeStruct((n, D_MODEL), F32),
        compiler_params=_params(("arbitrary",)),
        name="combine_final",
    )(tile_base, slot, x2, ye, norm_w)


def _trunk(x, p):
    batch, seq = x.shape[0], x.shape[1]
    n = batch * seq
    x2d = x.reshape(n, D_MODEL)
    proj, qd_t, vd_t = _inproj(x2d, p["mix_norm_w"], p["w_tok"], p["w_qd_t"], p["w_vd_t"])
    ya = _na_attention(proj, p["na_bias"], p["na_norm_w2"], batch, seq)
    yb_t = _diff_attention(proj, qd_t, vd_t, p["diff_bias"], p["lam_vecs"], p["subln_col"], batch, seq)
    x2, h2, aff_t = _outproj(x2d, ya, yb_t, p["w_out"], p["ffn_norm_w"], p["w_router_t"])
    cap = CAPACITY_FACTOR * n // N_EXPERTS
    theta, need = _route_threshold(aff_t, cap)
    slot, base = _route_prefix(aff_t, theta, need, cap)
    nchunks = n // ROUTE_CHUNK
    chunk_base = jnp.concatenate([base[:, :, 0].T, jnp.full((N_EXPERTS, 1), cap, jnp.int32)], axis=1)
    idx, gates = _route_compact(chunk_base, slot.reshape(N_EXPERTS, nchunks, ROUTE_CHUNK),
                                aff_t.reshape(N_EXPERTS, nchunks, ROUTE_CHUNK), cap)
    ye = _moe_experts(h2, idx.reshape(N_EXPERTS, cap), gates.reshape(N_EXPERTS, cap),
                      p["w_gate"], p["w_up"], p["w_down"])
    tile_base = chunk_base[:, ::COMBINE_TILE // ROUTE_CHUNK]
    y = _combine(tile_base, slot, x2, ye, p["final_norm_w"], cap)
    return y.reshape(batch, seq, D_MODEL)


def _prepare_params(mix_norm_w, w_in, na_rpb, na_norm_w, t5_table, lambda_q1, lambda_k1, lambda_q2, lambda_k2,
                    subln_w, w_out, ffn_norm_w, w_router, w_gate, w_up, w_down, final_norm_w):
    w = w_in[0]
    qa, ka_va = w[:, 0:W_NA] * NA_HEAD_DIM ** -0.5, w[:, W_NA:3 * W_NA]
    qd = w[:, 3 * W_NA:3 * W_NA + W_DIFF] * DIFF_HALF_DIM ** -0.5
    kd, vd = w[:, 3 * W_NA + W_DIFF:3 * W_NA + 2 * W_DIFF], w[:, 3 * W_NA + 2 * W_DIFF:]
    return {
        "mix_norm_w": mix_norm_w[0].reshape(1, D_MODEL),
        "w_tok": jnp.concatenate([qa, ka_va, kd], axis=1).astype(BF16),
        "w_qd_t": qd.T.astype(BF16),
        "w_vd_t": vd.T.astype(BF16),
        "na_bias": _na_bias_table(na_rpb[0]),
        "na_norm_w2": jnp.tile(na_norm_w[0], 2).reshape(1, LANES),
        "diff_bias": _diff_bias_table(t5_table),
        "lam_vecs": jnp.stack([lambda_q1[0], lambda_k1[0], lambda_q2[0], lambda_k2[0]]).astype(F32),
        "subln_col": subln_w[0].reshape(DIFF_V_DIM, 1),
        "w_out": w_out[0].astype(BF16),
        "ffn_norm_w": ffn_norm_w[0].reshape(1, D_MODEL),
        "w_router_t": w_router[0].T.astype(BF16),
        "w_gate": w_gate[0].astype(BF16),
        "w_up": w_up[0].astype(BF16),
        "w_down": w_down[0].astype(BF16),
        "final_norm_w": final_norm_w.reshape(1, D_MODEL),
    }


def kernel(x_prompt, x_sample, mix_norm_w, w_in, na_rpb, na_norm_w, t5_table, lambda_q1, lambda_k1,
           lambda_q2, lambda_k2, subln_w, w_out, ffn_norm_w, w_router, w_gate, w_up, w_down, final_norm_w):
    p = _prepare_params(mix_norm_w, w_in, na_rpb, na_norm_w, t5_table, lambda_q1, lambda_k1, lambda_q2,
                        lambda_k2, subln_w, w_out, ffn_norm_w, w_router, w_gate, w_up, w_down, final_norm_w)
    return (_trunk(x_prompt, p), _trunk(x_sample, p))
```

```python
import functools
import math

import jax
import jax.numpy as jnp
from jax import lax
from jax.experimental import pallas as pl
from jax.experimental.pallas import tpu as pltpu

F32 = jnp.float32
BF16 = jnp.bfloat16

D_MODEL = 1024
GRID_W = 64
NA_HEADS = 8
NA_HEAD_DIM = 64
NA_KH = 8
NA_KW = 16
DIFF_HEADS = 4
DIFF_HALF_DIM = 64
DIFF_V_DIM = 128
W_NA = 512
W_DIFF = 512
D_IN = 3072
T5_BUCKETS = 32
T5_MAX_DIST = 128
N_EXPERTS = 16
CAPACITY_FACTOR = 2
D_EXPERT = 2816
EPS = 1e-6
LAMBDA_INIT = 0.8 - 0.6 * math.exp(-0.3 * 0)
LOG2E = math.log2(math.e)
TOKEN_MAJOR_COLS = 3 * W_NA + W_DIFF

LANES = 128
MXU_DEPTH = 256
TOKEN_TILE = 512
NA_ROWS_PER_STEP = 8
DIFF_BLOCK = 512
DIFF_ONES_ROWS = 16
DIFF_SPLIT_MIN_PAIRS = 4
MOE_TILE = 512
MOE_GATHER_AT_CHUNK = 2
ROUTE_CHUNK = 256
ROUTE_CHUNKS_PER_STEP = 8
COMBINE_TILE = 512
COMBINE_DMA_ROWS = 32
FFN_CHUNKS = ((0, 512), (512, 512), (1024, 512), (1536, 512), (2048, 512), (2560, 256))
MASKED = -1e30
VMEM_LIMIT = 56 * 1024 * 1024

_NT = (((1,), (1,)), ((), ()))


def _params(sem, vmem=VMEM_LIMIT):
    return pltpu.CompilerParams(dimension_semantics=sem, vmem_limit_bytes=vmem)


def _inproj_body(x_ref, nw_ref, w_ref, wqt_ref, wvt_ref, o_ref, qt_ref, vt_ref):
    x = x_ref[...]
    ms = jnp.mean(x * x, axis=-1, keepdims=True)
    h = ((x * lax.rsqrt(ms + EPS)) * nw_ref[...]).astype(BF16)
    for c in range(TOKEN_MAJOR_COLS // W_NA):
        cols = slice(c * W_NA, (c + 1) * W_NA)
        o_ref[:, cols] = jnp.dot(h, w_ref[:, cols], preferred_element_type=F32).astype(BF16)
    qt = lax.dot_general(wqt_ref[...], h, _NT, preferred_element_type=F32)
    qt_ref[0] = (qt * LOG2E).astype(BF16)
    vt_ref[0] = lax.dot_general(wvt_ref[...], h, _NT, preferred_element_type=F32).astype(BF16)


def _inproj(x2d, norm_w, w_tok, w_qd_t, w_vd_t):
    n = x2d.shape[0]
    tm = TOKEN_TILE
    fixed = lambda i: (0, 0)
    return pl.pallas_call(
        _inproj_body,
        grid=(n // tm,),
        in_specs=[
            pl.BlockSpec((tm, D_MODEL), lambda i: (i, 0)),
            pl.BlockSpec((1, D_MODEL), fixed),
            pl.BlockSpec((D_MODEL, TOKEN_MAJOR_COLS), fixed),
            pl.BlockSpec((W_DIFF, D_MODEL), fixed),
            pl.BlockSpec((W_DIFF, D_MODEL), fixed),
        ],
        out_specs=[
            pl.BlockSpec((tm, TOKEN_MAJOR_COLS), lambda i: (i, 0)),
            pl.BlockSpec((1, W_DIFF, tm), lambda i: (i, 0, 0)),
            pl.BlockSpec((1, W_DIFF, tm), lambda i: (i, 0, 0)),
        ],
        out_shape=[
            jax.ShapeDtypeStruct((n, TOKEN_MAJOR_COLS), BF16),
            jax.ShapeDtypeStruct((n // tm, W_DIFF, tm), BF16),
            jax.ShapeDtypeStruct((n // tm, W_DIFF, tm), BF16),
        ],
        compiler_params=_params(("parallel",)),
        name="inproj",
    )(x2d, norm_w, w_tok, w_qd_t, w_vd_t)


def _na_bias_table(rpb):
    c = jnp.arange(GRID_W)
    col_start = jnp.clip(c - NA_KW // 2, 0, GRID_W - NA_KW)
    in_win = (c[None, :] >= col_start[:, None]) & (c[None, :] < col_start[:, None] + NA_KW)
    dc = jnp.clip(c[None, :] - c[:, None], -(NA_KW - 1), NA_KW - 1) + (NA_KW - 1)
    bc = rpb.astype(F32)[:, :, dc]
    bc = jnp.where(in_win[None, None], bc, MASKED)
    tabs = jnp.stack([bc[:, d0:d0 + NA_KH] for d0 in range(NA_KH)])
    tabs = tabs.transpose(0, 1, 3, 2, 4)
    return tabs.reshape(NA_KH, NA_HEADS // 2, 2 * GRID_W, NA_KH * GRID_W)


def _na_body(q_ref, kp_ref, kc_ref, kn_ref, vp_ref, vc_ref, vn_ref, bias_ref, nw_ref, o_ref,
             kbuf, vbuf, *, rows):
    j = pl.program_id(1)
    blk = NA_ROWS_PER_STEP * GRID_W
    kbuf[0:blk] = kp_ref[...]
    kbuf[blk:2 * blk] = kc_ref[...]
    kbuf[2 * blk:3 * blk] = kn_ref[...]
    vbuf[0:blk] = vp_ref[...]
    vbuf[blk:2 * blk] = vc_ref[...]
    vbuf[2 * blk:3 * blk] = vn_ref[...]
    lo = lax.broadcasted_iota(jnp.int32, (GRID_W, LANES), 1) < NA_HEAD_DIM
    nw = nw_ref[...]
    nkeys = NA_KH * GRID_W

    def softmax_parts(s):
        m = jnp.max(s, axis=-1, keepdims=True)
        p = jnp.exp(s - m)
        return p.astype(BF16), jnp.sum(p, axis=-1, keepdims=True)

    def one_row(a, carry):
        r = j * NA_ROWS_PER_STEP + a
        row_start = jnp.clip(r - NA_KH // 2, 0, rows - NA_KH)
        d0 = row_start - r + (NA_KH - 1)
        koff = pl.multiple_of((row_start - (j - 1) * NA_ROWS_PER_STEP) * GRID_W, GRID_W)
        qoff = pl.multiple_of(a * GRID_W, GRID_W)
        pairs = range(NA_HEADS // 2)
        cols = [slice(hp * LANES, (hp + 1) * LANES) for hp in pairs]
        scores = []
        for hp in pairs:
            qp = q_ref[pl.ds(qoff, GRID_W), cols[hp]]
            zero = jnp.zeros_like(qp)
            q2 = jnp.concatenate([jnp.where(lo, qp, zero), jnp.where(lo, zero, qp)], axis=0)
            kk = kbuf[pl.ds(koff, nkeys), cols[hp]]
            scores.append(lax.dot_general(q2, kk, _NT, preferred_element_type=F32) + bias_ref[d0, hp])
        probs = [softmax_parts(s) for s in scores]
        outs = [jnp.dot(p, vbuf[pl.ds(koff, nkeys), cols[hp]], preferred_element_type=F32) / l
                for hp, (p, l) in zip(pairs, probs)]
        for hp in pairs:
            cs = cols[hp]
            o = jnp.where(lo, outs[hp][0:GRID_W], outs[hp][GRID_W:2 * GRID_W])
            sq = o * o
            msa = jnp.sum(jnp.where(lo, sq, 0.0), axis=-1, keepdims=True) * (1.0 / NA_HEAD_DIM)
            msb = jnp.sum(jnp.where(lo, 0.0, sq), axis=-1, keepdims=True) * (1.0 / NA_HEAD_DIM)
            inv = jnp.where(lo, lax.rsqrt(msa + EPS), lax.rsqrt(msb + EPS))
            o_ref[pl.ds(qoff, GRID_W), cs] = ((o * inv) * nw).astype(BF16)
        return carry

    lax.fori_loop(0, NA_ROWS_PER_STEP, one_row, 0, unroll=8)


def _na_attention(proj, bias_tab, norm_w2, batch, seq):
    rows = seq // GRID_W
    assert rows % NA_ROWS_PER_STEP == 0 and rows >= NA_KH
    nb = rows // NA_ROWS_PER_STEP
    blk = NA_ROWS_PER_STEP * GRID_W
    n = batch * seq

    def at(col, shift):
        return pl.BlockSpec(
            (blk, W_NA), lambda b, j: (b * nb + jnp.clip(j + shift, 0, nb - 1), col))

    return pl.pallas_call(
        functools.partial(_na_body, rows=rows),
        grid=(batch, nb),
        in_specs=[
            at(0, 0),
            at(1, -1), at(1, 0), at(1, 1),
            at(2, -1), at(2, 0), at(2, 1),
            pl.BlockSpec((NA_KH, NA_HEADS // 2, 2 * GRID_W, NA_KH * GRID_W), lambda b, j: (0, 0, 0, 0)),
            pl.BlockSpec((1, LANES), lambda b, j: (0, 0)),
        ],
        out_specs=pl.BlockSpec((blk, W_NA), lambda b, j: (b * nb + j, 0)),
        out_shape=jax.ShapeDtypeStruct((n, W_NA), BF16),
        scratch_shapes=[pltpu.VMEM((3 * blk, W_NA), BF16), pltpu.VMEM((3 * blk, W_NA), BF16)],
        compiler_params=_params(("parallel", "parallel")),
        name="na_attention",
    )(proj, proj, proj, proj, proj, proj, proj, bias_tab, norm_w2)


def _t5_bucket(rel):
    nb = T5_BUCKETS // 2
    max_exact = nb // 2
    n = jnp.abs(rel)
    large = max_exact + (jnp.log(jnp.maximum(n, 1).astype(F32) / max_exact)
                         / math.log(T5_MAX_DIST / max_exact) * (nb - max_exact)).astype(jnp.int32)
    large = jnp.minimum(large, nb - 1)
    return jnp.where(rel > 0, nb, 0) + jnp.where(n < max_exact, n, large)


def _diff_bias_table(t5_table):
    x = DIFF_BLOCK
    assert x + 1 >= T5_MAX_DIST
    span = 2 * x
    table = t5_table.astype(F32) * LOG2E
    rel = (x - 1) - jnp.arange(span)[None, :] + jnp.arange(-1, 2)[:, None] * x
    diag = table[_t5_bucket(rel)].transpose(2, 0, 1)
    skew = jnp.tile(diag, (1, 1, x))[:, :, :x * (span - 1)].reshape(DIFF_HEADS, 3, x, span - 1)
    near = skew[:, :, :, x - 1:2 * x - 1]
    far = table[_t5_bucket(jnp.array([-(x + 1), x + 1]))].T
    far = jnp.broadcast_to(far[:, :, None, None], (DIFF_HEADS, 2, x, x))
    return jnp.concatenate([far[:, 0:1], near, far[:, 1:2]], axis=1)


def _diff_body(qt_ref, k_ref, vt_ref, bias_ref, lam_ref, sw_ref, o_ref, s_even, s_odd, *, nblk):
    i = pl.program_id(2)
    x = DIFF_BLOCK
    qt = qt_ref[0]
    first = lax.broadcasted_iota(jnp.int32, (LANES, x), 0) < DIFF_HALF_DIM
    zero = jnp.zeros_like(qt)
    qt_both = jnp.concatenate([jnp.where(first, qt, zero), jnp.where(first, zero, qt)], axis=1)

    def scores(j, s_ref):
        kc = k_ref[pl.ds(pl.multiple_of(j * x, x), x), :]
        s_ref[...] = jnp.dot(kc, qt_both, preferred_element_type=F32)

    ones_rows = jnp.ones((DIFF_ONES_ROWS, x), BF16)

    def consume(j, s_ref, carry, far_bias):
        vt_ext = jnp.concatenate([vt_ref[j], ones_rows], axis=0)
        out = []
        for c in range(2):
            m, acc = carry[2 * c:2 * c + 2]
            s = s_ref[:, c * x:(c + 1) * x]
            if far_bias is None:
                s = s + bias_ref[0, jnp.clip(j - i, -2, 2) + 2]
                shift = 0.0
            else:
                shift = far_bias
            m_in = m - shift
            mn = jnp.maximum(m_in, jnp.max(s, axis=0, keepdims=True))
            p = jnp.exp2(s - mn).astype(BF16)
            acc = jnp.exp2(m_in - mn) * acc + jnp.dot(vt_ext, p, preferred_element_type=F32)
            out += [mn + shift, acc]
        return tuple(out)

    def pair_loop(lo, hi, far_bias, carry):
        def body(jj, carry):
            j = 2 * jj
            scores(j + 1, s_odd)
            carry = consume(j, s_even, carry, far_bias)
            scores(j + 2, s_even)
            return consume(j + 1, s_odd, carry, far_bias)
        return lax.fori_loop(lo, hi, body, carry)

    def last_pair(far_bias, carry):
        scores(nblk - 1, s_odd)
        carry = consume(nblk - 2, s_even, carry, far_bias)
        return consume(nblk - 1, s_odd, carry, far_bias)

    npairs = nblk // 2
    near_lo = jnp.maximum((i - 1) // 2, 0)
    near_hi = jnp.minimum((i + 1) // 2, npairs - 1)
    bias_before = bias_ref[0, 0, 0:1, 0:1]
    bias_after = bias_ref[0, 4, 0:1, 0:1]
    init = (jnp.full((1, x), -jnp.inf, F32), jnp.zeros((DIFF_V_DIM + DIFF_ONES_ROWS, x), F32)) * 2
    scores(0, s_even)
    if npairs >= DIFF_SPLIT_MIN_PAIRS:
        carry = pair_loop(0, jnp.minimum(near_lo, npairs - 1), bias_before, init)
        carry = pair_loop(near_lo, jnp.minimum(near_hi + 1, npairs - 1), None, carry)
        carry = pair_loop(near_hi + 1, npairs - 1, bias_after, carry)
        _, ext1, _, ext2 = lax.cond(near_hi == npairs - 1,
                                    functools.partial(last_pair, None),
                                    functools.partial(last_pair, bias_after), carry)
    else:
        _, ext1, _, ext2 = last_pair(None, pair_loop(0, npairs - 1, None, init))
    acc1, l1 = ext1[0:DIFF_V_DIM], ext1[DIFF_V_DIM:DIFF_V_DIM + 1]
    acc2, l2 = ext2[0:DIFF_V_DIM], ext2[DIFF_V_DIM:DIFF_V_DIM + 1]
    lam = (jnp.exp(jnp.sum(lam_ref[0:1, :] * lam_ref[1:2, :], axis=-1, keepdims=True))
           - jnp.exp(jnp.sum(lam_ref[2:3, :] * lam_ref[3:4, :], axis=-1, keepdims=True)) + LAMBDA_INIT)
    o = acc1 / l1 - lam * (acc2 / l2)
    ms = jnp.mean(o * o, axis=0, keepdims=True)
    y = ((o * lax.rsqrt(ms + EPS)) * sw_ref[...]) * (1.0 - LAMBDA_INIT)
    o_ref[0] = y.astype(BF16)


def _diff_attention(proj, qd_t, vd_t, bias_tab, lam_vecs, subln_col, batch, seq):
    x = DIFF_BLOCK
    assert seq % (2 * x) == 0 and x == TOKEN_TILE
    nblk = seq // x
    n = batch * seq
    kcol = 3 * W_NA // LANES
    return pl.pallas_call(
        functools.partial(_diff_body, nblk=nblk),
        grid=(batch, DIFF_HEADS, nblk),
        in_specs=[
            pl.BlockSpec((1, LANES, x), lambda b, h, i: (b * nblk + i, h, 0)),
            pl.BlockSpec((seq, LANES), lambda b, h, i: (b, kcol + h)),
            pl.BlockSpec((nblk, LANES, x), lambda b, h, i: (b, h, 0)),
            pl.BlockSpec((1, 5, x, x), lambda b, h, i: (h, 0, 0, 0)),
            pl.BlockSpec((4, DIFF_HALF_DIM), lambda b, h, i: (0, 0)),
            pl.BlockSpec((DIFF_V_DIM, 1), lambda b, h, i: (0, 0)),
        ],
        out_specs=pl.BlockSpec((1, LANES, x), lambda b, h, i: (b * nblk + i, h, 0)),
        out_shape=jax.ShapeDtypeStruct((n // x, W_DIFF, x), BF16),
        scratch_shapes=[pltpu.VMEM((x, 2 * x), F32), pltpu.VMEM((x, 2 * x), F32)],
        compiler_params=_params(("parallel", "parallel", "parallel")),
        name="diff_attention",
    )(qd_t, proj, vd_t, bias_tab, lam_vecs, subln_col)


def _outproj_body(x_ref, ya_ref, yb_ref, wa_ref, wb_ref, nw_ref, wr_ref, x2_ref, h2_ref, aff_ref):
    attn = (jnp.dot(ya_ref[...], wa_ref[...], preferred_element_type=F32)
            + lax.dot_general(yb_ref[0], wb_ref[...], (((0,), (0,)), ((), ())), preferred_element_type=F32))
    x2 = x_ref[...] + attn
    x2_ref[...] = x2
    ms = jnp.mean(x2 * x2, axis=-1, keepdims=True)
    h2 = (x2 * lax.rsqrt(ms + EPS)) * nw_ref[...]
    for s in range(D_MODEL // LANES):
        h2_ref[:, s, :] = h2[:, s * LANES:(s + 1) * LANES]
    logits = lax.dot_general(wr_ref[...], h2.astype(BF16), _NT, preferred_element_type=F32)
    e = jnp.exp(logits - jnp.max(logits, axis=0, keepdims=True))
    aff_ref[...] = e / jnp.sum(e, axis=0, keepdims=True)


def _outproj(x2d, ya, yb, w_out, norm_w, w_router_t):
    n = x2d.shape[0]
    tm = TOKEN_TILE
    row = lambda i: (i, 0)
    fixed = lambda i: (0, 0)
    return pl.pallas_call(
        _outproj_body,
        grid=(n // tm,),
        in_specs=[
            pl.BlockSpec((tm, D_MODEL), row),
            pl.BlockSpec((tm, W_NA), row),
            pl.BlockSpec((1, W_DIFF, tm), lambda i: (i, 0, 0)),
            pl.BlockSpec((W_NA, D_MODEL), lambda i: (0, 0)),
            pl.BlockSpec((W_DIFF, D_MODEL), lambda i: (1, 0)),
            pl.BlockSpec((1, D_MODEL), fixed),
            pl.BlockSpec((N_EXPERTS, D_MODEL), fixed),
        ],
        out_specs=[
            pl.BlockSpec((tm, D_MODEL), row),
            pl.BlockSpec((tm, D_MODEL // LANES, LANES), lambda i: (i, 0, 0)),
            pl.BlockSpec((N_EXPERTS, tm), lambda i: (0, i)),
        ],
        out_shape=[
            jax.ShapeDtypeStruct((n, D_MODEL), F32),
            jax.ShapeDtypeStruct((n, D_MODEL // LANES, LANES), F32),
            jax.ShapeDtypeStruct((N_EXPERTS, n), F32),
        ],
        compiler_params=_params(("parallel",)),
        name="outproj_router",
    )(x2d, ya, yb, w_out, w_out, norm_w, w_router_t)


def _moe_body(idx_ref, idx_next_ref, h2_hbm, g_ref, wg_ref, wu_ref, wd_ref, o_ref, x_even, x_odd, sem, *, tm):
    nt = pl.num_programs(1)
    step = pl.program_id(0) * nt + pl.program_id(1)
    total = pl.num_programs(0) * nt

    def row_copy(tok, i, buf, sl):
        return pltpu.make_async_copy(h2_hbm.at[pl.ds(tok, 1)], buf.at[pl.ds(i, 1)], sem.at[sl])

    def wait_tile(buf, sl):
        pltpu.make_async_copy(h2_hbm.at[pl.ds(0, tm)], buf, sem.at[sl]).wait()

    @pl.when(step == 0)
    def _():
        def body(i, carry):
            row_copy(idx_ref[0, 0, i], i, x_even, 0).start()
            return carry
        lax.fori_loop(0, tm, body, 0, unroll=8)

    def run(cur, cur_sl, nxt, nxt_sl):
        wait_tile(cur, cur_sl)
        x = jnp.concatenate([cur[:, s, :] for s in range(D_MODEL // LANES)], axis=1).astype(BF16)
        acc = jnp.zeros((tm, D_MODEL), F32)
        for ci, (f0, fc) in enumerate(FFN_CHUNKS):
            if ci == MOE_GATHER_AT_CHUNK:
                @pl.when(step >= 0)
                def _():
                    for i in range(tm):
                        row_copy(idx_next_ref[0, 0, i], i, nxt, nxt_sl).start(priority=i % 2)
            g = jnp.dot(x, wg_ref[0, :, f0:f0 + fc], preferred_element_type=F32)
            u = jnp.dot(x, wu_ref[0, :, f0:f0 + fc], preferred_element_type=F32)
            h = (jax.nn.silu(g) * u).astype(BF16)
            acc = acc + jnp.dot(h, wd_ref[0, f0:f0 + fc, :], preferred_element_type=F32)
        o_ref[...] = (acc * g_ref[...]).astype(o_ref.dtype)

        @pl.when(step == total - 1)
        def _():
            wait_tile(nxt, nxt_sl)

    @pl.when(step % 2 == 0)
    def _():
        run(x_even, 0, x_odd, 1)

    @pl.when(step % 2 == 1)
    def _():
        run(x_odd, 1, x_even, 0)


def _moe_experts(h2, idx, gates, w_gate, w_up, w_down):
    n_exp, cap = idx.shape
    tm = min(MOE_TILE, cap)
    assert cap % tm == 0
    nt = cap // tm
    idx3 = idx.reshape(n_exp * nt, 1, tm)

    def tile_ahead(k):
        return pl.BlockSpec((1, 1, tm), lambda e, t: (jnp.minimum(e * nt + t + k, n_exp * nt - 1), 0, 0),
                            memory_space=pltpu.SMEM)

    tile_shape = (tm, D_MODEL // LANES, LANES)
    return pl.pallas_call(
        functools.partial(_moe_body, tm=tm),
        grid=(n_exp, nt),
        in_specs=[
            tile_ahead(0), tile_ahead(1),
            pl.BlockSpec(memory_space=pl.ANY),
            pl.BlockSpec((tm, 1), lambda e, t: (e * nt + t, 0)),
            pl.BlockSpec((1, D_MODEL, D_EXPERT), lambda e, t: (e, 0, 0)),
            pl.BlockSpec((1, D_MODEL, D_EXPERT), lambda e, t: (e, 0, 0)),
            pl.BlockSpec((1, D_EXPERT, D_MODEL), lambda e, t: (e, 0, 0)),
        ],
        out_specs=pl.BlockSpec((tm, D_MODEL), lambda e, t: (e * nt + t, 0)),
        out_shape=jax.ShapeDtypeStruct((n_exp * cap, D_MODEL), BF16),
        scratch_shapes=[pltpu.VMEM(tile_shape, F32), pltpu.VMEM(tile_shape, F32), pltpu.SemaphoreType.DMA((2,))],
        compiler_params=_params(("arbitrary", "arbitrary")),
        name="moe_experts",
    )(idx3, idx3, h2, gates.reshape(n_exp * cap, 1), w_gate, w_up, w_down)


def _route_threshold_body(aff_ref, theta_ref, need_ref, *, cap):
    n_exp = aff_ref.shape[0]

    def count_ge(cand):
        bits = pltpu.bitcast(aff_ref[...], jnp.int32)
        return jnp.sum(jnp.where(bits >= cand, 1.0, 0.0), axis=1, keepdims=True)

    def step(b, theta):
        cand = theta | jnp.left_shift(jnp.int32(1), 30 - b)
        return jnp.where(count_ge(cand) >= cap, cand, theta)

    theta = lax.fori_loop(0, 31, step, jnp.zeros((n_exp, 1), jnp.int32))
    need = cap - count_ge(theta + 1)
    theta_ref[...] = jnp.broadcast_to(theta, theta_ref.shape)
    need_ref[...] = jnp.broadcast_to(need, need_ref.shape)


def _route_threshold(aff_t, cap):
    n_exp, n = aff_t.shape
    full = lambda i: (0, 0)
    return pl.pallas_call(
        functools.partial(_route_threshold_body, cap=cap),
        grid=(1,),
        in_specs=[pl.BlockSpec((n_exp, n), full)],
        out_specs=[pl.BlockSpec((n_exp, LANES), full), pl.BlockSpec((n_exp, LANES), full)],
        out_shape=[jax.ShapeDtypeStruct((n_exp, LANES), jnp.int32), jax.ShapeDtypeStruct((n_exp, LANES), F32)],
        compiler_params=_params(("arbitrary",)),
        name="route_threshold",
    )(aff_t)


def _route_prefix_body(aff_ref, tri_ref, theta_ref, need_ref, slot_ref, base_ref, run_sel, run_tie, *, cap):
    @pl.when(pl.program_id(0) == 0)
    def _():
        run_sel[...] = jnp.zeros_like(run_sel)
        run_tie[...] = jnp.zeros_like(run_tie)

    n_exp = aff_ref.shape[0]
    w = ROUTE_CHUNK
    theta = theta_ref[:, 0:1]
    need = need_ref[:, 0:1]
    expert_base = lax.broadcasted_iota(jnp.int32, (n_exp, 1), 0) * cap
    tri = tri_ref[...]
    for k in range(ROUTE_CHUNKS_PER_STEP):
        bits = pltpu.bitcast(aff_ref[:, k * w:(k + 1) * w], jnp.int32)
        tie = bits == theta
        tie_f = jnp.where(tie, 1.0, 0.0)
        tie_incl = jnp.dot(tie_f.astype(BF16), tri, preferred_element_type=F32)
        tie_rank = run_tie[:, 0:1] + tie_incl - tie_f
        sel = (bits > theta) | (tie & (tie_rank < need))
        sel_f = jnp.where(sel, 1.0, 0.0)
        incl = jnp.dot(sel_f.astype(BF16), tri, preferred_element_type=F32)
        rank = run_sel[:, 0:1] + incl - sel_f
        slot_ref[:, k * w:(k + 1) * w] = jnp.where(sel, rank.astype(jnp.int32) + expert_base, -1)
        base_ref[k] = run_sel[...].astype(jnp.int32)
        run_sel[...] = run_sel[...] + jnp.sum(sel_f, axis=1, keepdims=True)
        run_tie[...] = run_tie[...] + jnp.sum(tie_f, axis=1, keepdims=True)


def _route_prefix(aff_t, theta, need, cap):
    n_exp, n = aff_t.shape
    w = ROUTE_CHUNK
    span = w * ROUTE_CHUNKS_PER_STEP
    assert n % span == 0
    tri = (jnp.arange(w)[:, None] <= jnp.arange(w)[None, :]).astype(BF16)
    fixed = lambda i: (0, 0)
    return pl.pallas_call(
        functools.partial(_route_prefix_body, cap=cap),
        grid=(n // span,),
        in_specs=[
            pl.BlockSpec((n_exp, span), lambda i: (0, i)),
            pl.BlockSpec((w, w), fixed),
            pl.BlockSpec((n_exp, LANES), fixed),
            pl.BlockSpec((n_exp, LANES), fixed),
        ],
        out_specs=[
            pl.BlockSpec((n_exp, span), lambda i: (0, i)),
            pl.BlockSpec((ROUTE_CHUNKS_PER_STEP, n_exp, LANES), lambda i: (i, 0, 0)),
        ],
        out_shape=[
            jax.ShapeDtypeStruct((n_exp, n), jnp.int32),
            jax.ShapeDtypeStruct((n // w, n_exp, LANES), jnp.int32),
        ],
        scratch_shapes=[pltpu.VMEM((n_exp, LANES), F32), pltpu.VMEM((n_exp, LANES), F32)],
        compiler_params=_params(("arbitrary",)),
        name="route_prefix",
    )(aff_t, tri, theta, need)


def _route_compact_body(cb_ref, slot_ref, aff_ref, idx_ref, gate_ref, *, cap, nchunks):
    e = pl.program_id(0)
    w = ROUTE_CHUNK
    s = LANES
    slot_iota = lax.broadcasted_iota(jnp.int32, (s, w), 0)
    tok_iota = lax.broadcasted_iota(jnp.int32, (s, w), 1)
    eye = lax.broadcasted_iota(jnp.int32, (s, s), 0) == lax.broadcasted_iota(jnp.int32, (s, s), 1)

    def fold(v):
        out = v[:, 0:s]
        for q in range(1, w // s):
            out = out + v[:, q * s:(q + 1) * s]
        return out

    def to_row(acc):
        col = jnp.sum(acc, axis=1, keepdims=True)
        return jnp.sum(jnp.where(eye, col, 0.0), axis=0, keepdims=True)

    def one_tile(jt, c_first):
        j0 = jt * s
        c_first = lax.while_loop(lambda c: cb_ref[e, c + 1] <= j0, lambda c: c + 1, c_first)
        want = slot_iota + (e * cap + j0)

        def more(state):
            c = state[0]
            return jnp.logical_and(c < nchunks, cb_ref[e, jnp.minimum(c, nchunks - 1)] < j0 + s)

        def chunk(state):
            c, acc_i, acc_g = state
            hit = want == slot_ref[0, pl.ds(c, 1), :]
            tok = (tok_iota + c * w).astype(F32)
            acc_i = acc_i + fold(jnp.where(hit, tok, 0.0))
            acc_g = acc_g + fold(jnp.where(hit, aff_ref[0, pl.ds(c, 1), :], 0.0))
            return c + 1, acc_i, acc_g

        zero = jnp.zeros((s, s), F32)
        _, acc_i, acc_g = lax.while_loop(more, chunk, (c_first, zero, zero))
        idx_ref[0, pl.ds(jt, 1), :] = to_row(acc_i).astype(jnp.int32)
        gate_ref[0, pl.ds(jt, 1), :] = to_row(acc_g)
        return c_first

    lax.fori_loop(0, cap // s, one_tile, jnp.int32(0))


def _route_compact(chunk_base, slot3, aff3, cap):
    n_exp, nchunks, w = slot3.shape
    assert cap % LANES == 0
    per_e = lambda e, cb: (e, 0, 0)
    grid_spec = pltpu.PrefetchScalarGridSpec(
        num_scalar_prefetch=1,
        grid=(n_exp,),
        in_specs=[pl.BlockSpec((1, nchunks, w), per_e), pl.BlockSpec((1, nchunks, w), per_e)],
        out_specs=[pl.BlockSpec((1, cap // LANES, LANES), per_e), pl.BlockSpec((1, cap // LANES, LANES), per_e)],
    )
    return pl.pallas_call(
        functools.partial(_route_compact_body, cap=cap, nchunks=nchunks),
        grid_spec=grid_spec,
        out_shape=[
            jax.ShapeDtypeStruct((n_exp, cap // LANES, LANES), jnp.int32),
            jax.ShapeDtypeStruct((n_exp, cap // LANES, LANES), F32),
        ],
        compiler_params=_params(("arbitrary",)),
        name="route_compact",
    )(chunk_base, slot3, aff3)


def _combine_body(cb_ref, slot_ref, x2_ref, ye_hbm, nw_ref, o_ref, stage, chunk_row, chunk_count, sem,
                  *, cap, ntiles):
    tile = pl.program_id(0)
    cur = tile % 2
    tt = COMBINE_TILE
    ch = COMBINE_DMA_ROWS
    n_exp = slot_ref.shape[0]
    kblock = MXU_DEPTH // ch

    def chunk_copy(row, k, sl):
        return pltpu.make_async_copy(ye_hbm.at[pl.ds(row, ch), :], stage.at[sl, pl.ds(k * ch, ch), :], sem.at[sl])

    def issue(t, sl):
        def per_expert(e, k):
            lo = cb_ref[e, t]
            hi = cb_ref[e, t + 1]
            first = lo // ch
            nch = jnp.where(hi > lo, (hi - 1) // ch - first + 1, 0)

            def per_chunk(i, k):
                row = pl.multiple_of(e * cap + (first + i) * ch, ch)
                chunk_copy(row, k, sl).start()
                chunk_row[sl, k] = row
                return k + 1

            return lax.fori_loop(0, nch, per_chunk, k)

        chunk_count[sl] = lax.fori_loop(0, n_exp, per_expert, jnp.int32(0))

    @pl.when(tile == 0)
    def _():
        issue(0, 0)

    @pl.when(tile + 1 < ntiles)
    def _():
        issue(tile + 1, 1 - cur)

    nchunk = chunk_count[cur]

    def wait_one(k, carry):
        chunk_copy(0, k, cur).wait()
        return carry

    lax.fori_loop(0, nchunk, wait_one, 0)
    nblock = (nchunk + kblock - 1) // kblock

    def pad_one(k, carry):
        stage[cur, pl.ds(pl.multiple_of(k * ch, ch), ch), :] = jnp.zeros((ch, D_MODEL), BF16)
        chunk_row[cur, k] = -(1 << 30)
        return carry

    lax.fori_loop(nchunk, nblock * kblock, pad_one, 0)
    row_iota = lax.broadcasted_iota(jnp.int32, (ch, tt), 0)

    o_ref[...] = x2_ref[...]

    def one_block(b, carry):
        parts = []
        for i in range(kblock):
            row = chunk_row[cur, b * kblock + i]
            e = jnp.clip(row // cap, 0, n_exp - 1)
            hit = (row_iota + row) == slot_ref[pl.ds(e, 1), :]
            parts.append(jnp.where(hit, 1.0, 0.0).astype(BF16))
        onehot_t = jnp.concatenate(parts, axis=0)
        rows = stage[cur, pl.ds(pl.multiple_of(b * MXU_DEPTH, MXU_DEPTH), MXU_DEPTH), :]
        o_ref[...] += lax.dot_general(onehot_t, rows, (((0,), (0,)), ((), ())), preferred_element_type=F32)
        return carry

    lax.fori_loop(0, nblock, one_block, 0)
    x3 = o_ref[...]
    ms = jnp.mean(x3 * x3, axis=-1, keepdims=True)
    o_ref[...] = (x3 * lax.rsqrt(ms + EPS)) * nw_ref[...]


def _combine(tile_base, slot, x2, ye, norm_w, cap):
    n_exp, n = slot.shape
    tt = COMBINE_TILE
    ch = COMBINE_DMA_ROWS
    ntiles = n // tt
    assert cap % ch == 0 and tt % ch == 0
    max_rows = n_exp * (tt + ch)
    max_rows = -(-max_rows // MXU_DEPTH) * MXU_DEPTH
    grid_spec = pltpu.PrefetchScalarGridSpec(
        num_scalar_prefetch=1,
        grid=(ntiles,),
        in_specs=[
            pl.BlockSpec((n_exp, tt), lambda t, cb: (0, t)),
            pl.BlockSpec((tt, D_MODEL), lambda t, cb: (t, 0)),
            pl.BlockSpec(memory_space=pl.ANY),
            pl.BlockSpec((1, D_MODEL), lambda t, cb: (0, 0)),
        ],
        out_specs=pl.BlockSpec((tt, D_MODEL), lambda t, cb: (t, 0)),
        scratch_shapes=[
            pltpu.VMEM((2, max_rows, D_MODEL), BF16),
            pltpu.SMEM((2, max_rows // ch), jnp.int32),
            pltpu.SMEM((2,), jnp.int32),
            pltpu.SemaphoreType.DMA((2,)),
        ],
    )
    return pl.pallas_call(
        functools.partial(_combine_body, cap=cap, ntiles=ntiles),
        grid_spec=grid_spec,
        out_shape=jax.ShapeDtypeStruct((n, D_MODEL), F32),
        compiler_params=_params(("arbitrary",)),
        name="combine_final",
    )(tile_base, slot, x2, ye, norm_w)


def _trunk(x, p):
    batch, seq = x.shape[0], x.shape[1]
    n = batch * seq
    x2d = x.reshape(n, D_MODEL)
    proj, qd_t, vd_t = _inproj(x2d, p["mix_norm_w"], p["w_tok"], p["w_qd_t"], p["w_vd_t"])
    ya = _na_attention(proj, p["na_bias"], p["na_norm_w2"], batch, seq)
    yb_t = _diff_attention(proj, qd_t, vd_t, p["diff_bias"], p["lam_vecs"], p["subln_col"], batch, seq)
    x2, h2, aff_t = _outproj(x2d, ya, yb_t, p["w_out"], p["ffn_norm_w"], p["w_router_t"])
    cap = CAPACITY_FACTOR * n // N_EXPERTS
    theta, need = _route_threshold(aff_t, cap)
    slot, base = _route_prefix(aff_t, theta, need, cap)
    nchunks = n // ROUTE_CHUNK
    chunk_base = jnp.concatenate([base[:, :, 0].T, jnp.full((N_EXPERTS, 1), cap, jnp.int32)], axis=1)
    idx, gates = _route_compact(chunk_base, slot.reshape(N_EXPERTS, nchunks, ROUTE_CHUNK),
                                aff_t.reshape(N_EXPERTS, nchunks, ROUTE_CHUNK), cap)
    ye = _moe_experts(h2, idx.reshape(N_EXPERTS, cap), gates.reshape(N_EXPERTS, cap),
                      p["w_gate"], p["w_up"], p["w_down"])
    tile_base = chunk_base[:, ::COMBINE_TILE // ROUTE_CHUNK]
    y = _combine(tile_base, slot, x2, ye, p["final_norm_w"], cap)
    return y.reshape(batch, seq, D_MODEL)


def _prepare_params(mix_norm_w, w_in, na_rpb, na_norm_w, t5_table, lambda_q1, lambda_k1, lambda_q2, lambda_k2,
                    subln_w, w_out, ffn_norm_w, w_router, w_gate, w_up, w_down, final_norm_w):
    w = w_in[0]
    qa, ka_va = w[:, 0:W_NA] * NA_HEAD_DIM ** -0.5, w[:, W_NA:3 * W_NA]
    qd = w[:, 3 * W_NA:3 * W_NA + W_DIFF] * DIFF_HALF_DIM ** -0.5
    kd, vd = w[:, 3 * W_NA + W_DIFF:3 * W_NA + 2 * W_DIFF], w[:, 3 * W_NA + 2 * W_DIFF:]
    return {
        "mix_norm_w": mix_norm_w[0].reshape(1, D_MODEL),
        "w_tok": jnp.concatenate([qa, ka_va, kd], axis=1).astype(BF16),
        "w_qd_t": qd.T.astype(BF16),
        "w_vd_t": vd.T.astype(BF16),
        "na_bias": _na_bias_table(na_rpb[0]),
        "na_norm_w2": jnp.tile(na_norm_w[0], 2).reshape(1, LANES),
        "diff_bias": _diff_bias_table(t5_table),
        "lam_vecs": jnp.stack([lambda_q1[0], lambda_k1[0], lambda_q2[0], lambda_k2[0]]).astype(F32),
        "subln_col": subln_w[0].reshape(DIFF_V_DIM, 1),
        "w_out": w_out[0].astype(BF16),
        "ffn_norm_w": ffn_norm_w[0].reshape(1, D_MODEL),
        "w_router_t": w_router[0].T.astype(BF16),
        "w_gate": w_gate[0].astype(BF16),
        "w_up": w_up[0].astype(BF16),
        "w_down": w_down[0].astype(BF16),
        "final_norm_w": final_norm_w.reshape(1, D_MODEL),
    }


def kernel(x_prompt, x_sample, mix_norm_w, w_in, na_rpb, na_norm_w, t5_table, lambda_q1, lambda_k1,
           lambda_q2, lambda_k2, subln_w, w_out, ffn_norm_w, w_router, w_gate, w_up, w_down, final_norm_w):
    p = _prepare_params(mix_norm_w, w_in, na_rpb, na_norm_w, t5_table, lambda_q1, lambda_k1, lambda_q2,
                        lambda_k2, subln_w, w_out, ffn_norm_w, w_router, w_gate, w_up, w_down, final_norm_w)
    return (_trunk(x_prompt, p), _trunk(x_sample, p))
```
